```python
import jax, jax.numpy as jnp
from jax import lax
import numpy as np

D_MODEL = 2048
BATCH = 8
SEQ = 2048
DEPTH = 1

MIX_WIDTH = D_MODEL
MLSTM_WIDTH = MIX_WIDTH // 2
ATTN_WIDTH = MIX_WIDTH - MLSTM_WIDTH
MLSTM_HEADS = 4
MLSTM_HEAD_DIM = MLSTM_WIDTH // MLSTM_HEADS
MLSTM_CHUNK = 128
CONV_WIDTH = 5
ATTN_HEAD_DIM = 128
ATTN_HEADS = ATTN_WIDTH // ATTN_HEAD_DIM
ATTN_KV_HEADS = 2
ATTN_KV_WIDTH = ATTN_KV_HEADS * ATTN_HEAD_DIM
WINDOW = 128
ATTN_BLOCK = 128
ROPE_THETA = 10000.0
D_FF = -(-(8 * D_MODEL) // (3 * 256)) * 256
N_GATES = 4 * MLSTM_HEADS
IN_PROJ = 4 * MLSTM_WIDTH + N_GATES + ATTN_WIDTH + 2 * ATTN_KV_WIDTH
EPS = 1e-6
NEG_INF = -1e30

kernel_name = 'hymba_mlstm_swa_adaln_block'


def rmsnorm(x, g):
    xf = x.astype(jnp.float32)
    y = xf * lax.rsqrt(jnp.mean(xf * xf, axis=-1, keepdims=True) + EPS)
    return (y * g.astype(jnp.float32)).astype(x.dtype)


def head_layernorm(h):
    mu = jnp.mean(h, axis=-1, keepdims=True)
    hc = h - mu
    return hc * lax.rsqrt(jnp.mean(hc * hc, axis=-1, keepdims=True) + EPS)


def rope(t, positions):
    d = t.shape[-1]
    half = d // 2
    inv = ROPE_THETA ** (-jnp.arange(half, dtype=jnp.float32) * 2.0 / d)
    ang = positions.astype(jnp.float32)[..., None] * inv
    cos = jnp.cos(ang)[:, :, None, :]
    sin = jnp.sin(ang)[:, :, None, :]
    tf = t.astype(jnp.float32)
    t1, t2 = tf[..., :half], tf[..., half:]
    return jnp.concatenate([t1 * cos - t2 * sin, t2 * cos + t1 * sin], axis=-1).astype(t.dtype)


def centred_depthwise_conv(u, w, b):
    C = u.shape[-1]
    pad = CONV_WIDTH // 2
    y = lax.conv_general_dilated(u, w.astype(u.dtype)[:, None, :], window_strides=(1,),
                                 padding=[(pad, pad)], dimension_numbers=('NWC', 'WIO', 'NWC'),
                                 feature_group_count=C)
    return y + b.astype(u.dtype)


def mlstm_chunkwise(q, k, v, log_i, log_f):
    B, H, S, d = q.shape
    L = MLSTM_CHUNK
    nc = S // L

    def to_chunks(t):
        return jnp.moveaxis(t.reshape(t.shape[:2] + (nc, L) + t.shape[3:]), 2, 0)

    xs = (to_chunks(q), to_chunks(k), to_chunks(v), to_chunks(log_i), to_chunks(log_f))
    tri = jnp.tril(jnp.ones((L, L), dtype=bool))

    def step(carry, inp):
        C, n, m = carry
        qj, kj, vj, li, lf = inp
        b = jnp.cumsum(lf, axis=-1)
        dmat = jnp.where(tri, b[..., :, None] - b[..., None, :] + li[..., None, :], NEG_INF)
        inter = b + m[..., None]
        m_t = jnp.maximum(inter, jnp.max(dmat, axis=-1))
        w = jnp.exp(dmat - m_t[..., None])
        a = jnp.exp(inter - m_t)
        qk = jnp.einsum('bhtd,bhsd->bhts', qj, kj) * w
        num = a[..., None] * jnp.einsum('bhtd,bhde->bhte', qj, C) + jnp.einsum('bhts,bhse->bhte', qk, vj)
        den = a * jnp.einsum('bhtd,bhd->bht', qj, n) + jnp.sum(qk, axis=-1)
        h = num / jnp.maximum(jnp.abs(den), jnp.exp(-m_t))[..., None]
        bl = b[..., -1]
        g = bl[..., None] - b + li
        m_new = jnp.maximum(bl + m, jnp.max(g, axis=-1))
        sc = jnp.exp(bl + m - m_new)
        wk = jnp.exp(g - m_new[..., None])[..., None] * kj
        C_new = sc[..., None, None] * C + jnp.einsum('bhsd,bhse->bhde', wk, vj)
        n_new = sc[..., None] * n + jnp.sum(wk, axis=-2)
        return (C_new, n_new, m_new), h

    init = (jnp.zeros((B, H, d, d), jnp.float32), jnp.zeros((B, H, d), jnp.float32),
            jnp.zeros((B, H), jnp.float32))
    _, hs = lax.scan(step, init, xs)
    return jnp.moveaxis(hs, 0, 2).reshape(B, H, S, d)


def windowed_gqa(q, k, v, sink):
    B, S, H, d = q.shape
    G = k.shape[2]
    R = H // G
    blk = ATTN_BLOCK
    nb = S // blk
    qb = q.reshape(B, nb, blk, G, R, d)
    pad = ((0, 0), (blk, blk), (0, 0), (0, 0))
    kp = jnp.pad(k, pad).reshape(B, nb + 2, blk, G, d)
    vp = jnp.pad(v, pad).reshape(B, nb + 2, blk, G, d)
    kb = jnp.concatenate([kp[:, :-2], kp[:, 1:-1], kp[:, 2:]], axis=2)
    vb = jnp.concatenate([vp[:, :-2], vp[:, 1:-1], vp[:, 2:]], axis=2)
    s = jnp.einsum('bnqgrd,bnkgd->bngrqk', qb, kb).astype(jnp.float32) * (d ** -0.5)
    qpos = jnp.arange(nb)[:, None] * blk + jnp.arange(blk)[None, :]
    kpos = (jnp.arange(nb)[:, None] - 1) * blk + jnp.arange(3 * blk)[None, :]
    rel = kpos[:, None, :] - qpos[:, :, None]
    valid = (jnp.abs(rel) <= WINDOW) & (kpos[:, None, :] >= 0) & (kpos[:, None, :] < S)
    s = jnp.where(valid[None, :, None, None, :, :], s, NEG_INF)
    sink_l = jnp.broadcast_to(sink.astype(jnp.float32).reshape(G, R)[None, None, :, :, None, None],
                              s.shape[:-1] + (1,))
    p = jax.nn.softmax(jnp.concatenate([s, sink_l], axis=-1), axis=-1)[..., :-1]
    o = jnp.einsum('bngrqk,bnkgd->bnqgrd', p.astype(v.dtype), vb)
    return o.reshape(B, S, H, d)


def hybrid_mixer(h, positions, w_in, b_gates, w_conv, b_conv, g_mlstm_out, sink, g_attn_out, w_out):
    B, S, _ = h.shape
    f32 = jnp.float32
    proj = h @ w_in.astype(h.dtype)
    sizes = [MLSTM_WIDTH, MLSTM_WIDTH, MLSTM_WIDTH, MLSTM_WIDTH, N_GATES, ATTN_WIDTH, ATTN_KV_WIDTH]
    cuts = [int(v) for v in np.cumsum(sizes)]
    mq, mk, mv, mo, gpre, aq, ak, av = jnp.split(proj, cuts, axis=-1)

    qk = jax.nn.silu(centred_depthwise_conv(jnp.concatenate([mq, mk], axis=-1), w_conv, b_conv))
    mq, mk = jnp.split(qk, 2, axis=-1)

    def heads(t):
        return t.reshape(B, S, MLSTM_HEADS, MLSTM_HEAD_DIM).transpose(0, 2, 1, 3).astype(f32)

    q = heads(mq)
    k = heads(mk) * (MLSTM_HEAD_DIM ** -0.5)
    v = heads(mv)
    gates = (gpre.astype(f32) + b_gates.astype(f32)).reshape(B, S, 4, MLSTM_HEADS).transpose(2, 0, 3, 1)
    li_f, lf_f = gates[0], jax.nn.log_sigmoid(gates[1])
    li_b, lf_b = gates[2], jax.nn.log_sigmoid(gates[3])
    h_fwd = mlstm_chunkwise(q, k, v, li_f, lf_f)
    flip = lambda t: jnp.flip(t, axis=2)
    h_bwd = flip(mlstm_chunkwise(flip(q), flip(k), flip(v), flip(li_b), flip(lf_b)))
    hm = head_layernorm(h_fwd + h_bwd).transpose(0, 2, 1, 3).reshape(B, S, MLSTM_WIDTH)
    m_out = (jax.nn.sigmoid(mo.astype(f32)) * hm * g_mlstm_out.astype(f32)).astype(h.dtype)

    aq = rope(aq.reshape(B, S, ATTN_HEADS, ATTN_HEAD_DIM), positions)
    ak = rope(ak.reshape(B, S, ATTN_KV_HEADS, ATTN_HEAD_DIM), positions)
    av = av.reshape(B, S, ATTN_KV_HEADS, ATTN_HEAD_DIM)
    a_out = windowed_gqa(aq, ak, av, sink).reshape(B, S, ATTN_WIDTH)
    a_out = rmsnorm(a_out, g_attn_out)

    return jnp.concatenate([m_out, a_out], axis=-1) @ w_out.astype(h.dtype)


def swiglu(h, w_gate, w_up, w_down):
    return (jax.nn.silu(h @ w_gate.astype(h.dtype)) * (h @ w_up.astype(h.dtype))) @ w_down.astype(h.dtype)


def setup_inputs(seed: int = 0) -> dict:
    key = jax.random.key(seed)
    ks = jax.random.split(key, 20)
    f32 = jnp.float32
    s_d = D_MODEL ** -0.5
    x = jax.random.normal(ks[0], (BATCH, SEQ, D_MODEL), f32)
    c = jax.random.normal(ks[1], (BATCH, D_MODEL), f32)
    positions = (jnp.arange(SEQ, dtype=jnp.int32)[None, :]
                 + jax.random.randint(ks[2], (BATCH, 1), 0, 4096, dtype=jnp.int32))
    w_ada = jax.random.normal(ks[3], (DEPTH, D_MODEL, 6 * D_MODEL), f32) * s_d
    b_ada = 0.02 * jax.random.normal(ks[4], (DEPTH, 6 * D_MODEL), f32)
    g_norm1 = 1.0 + 0.02 * jax.random.normal(ks[5], (DEPTH, D_MODEL), f32)
    g_norm2 = 1.0 + 0.02 * jax.random.normal(ks[6], (DEPTH, D_MODEL), f32)
    w_in = jax.random.normal(ks[7], (DEPTH, D_MODEL, IN_PROJ), f32) * s_d
    f_bias = jnp.linspace(3.0, 6.0, MLSTM_HEADS, dtype=f32)
    zero_b = jnp.zeros((MLSTM_HEADS,), f32)
    base = jnp.concatenate([zero_b, f_bias, zero_b, f_bias])
    b_gates = base[None, :] + 0.1 * jax.random.normal(ks[8], (DEPTH, N_GATES), f32)
    w_conv = jax.random.normal(ks[9], (DEPTH, CONV_WIDTH, 2 * MLSTM_WIDTH), f32) * (CONV_WIDTH ** -0.5)
    b_conv = 0.02 * jax.random.normal(ks[10], (DEPTH, 2 * MLSTM_WIDTH), f32)
    g_mlstm_out = 1.0 + 0.02 * jax.random.normal(ks[11], (DEPTH, MLSTM_WIDTH), f32)
    sink = 0.5 * jax.random.normal(ks[12], (DEPTH, ATTN_HEADS), f32)
    g_attn_out = 1.0 + 0.02 * jax.random.normal(ks[13], (DEPTH, ATTN_WIDTH), f32)
    w_out = jax.random.normal(ks[14], (DEPTH, MIX_WIDTH, D_MODEL), f32) * (MIX_WIDTH ** -0.5)
    w_gate = jax.random.normal(ks[15], (DEPTH, D_MODEL, D_FF), f32) * s_d
    w_up = jax.random.normal(ks[16], (DEPTH, D_MODEL, D_FF), f32) * s_d
    w_down = jax.random.normal(ks[17], (DEPTH, D_FF, D_MODEL), f32) * (D_FF ** -0.5)
    g_final = 1.0 + 0.02 * jax.random.normal(ks[18], (D_MODEL,), f32)
    return {'x': x, 'c': c, 'positions': positions, 'w_ada': w_ada, 'b_ada': b_ada,
            'g_norm1': g_norm1, 'g_norm2': g_norm2, 'w_in': w_in, 'b_gates': b_gates,
            'w_conv': w_conv, 'b_conv': b_conv, 'g_mlstm_out': g_mlstm_out, 'sink': sink,
            'g_attn_out': g_attn_out, 'w_out': w_out, 'w_gate': w_gate, 'w_up': w_up,
            'w_down': w_down, 'g_final': g_final}


def reference(x, c, positions, w_ada, b_ada, g_norm1, g_norm2, w_in, b_gates, w_conv, b_conv,
              g_mlstm_out, sink, g_attn_out, w_out, w_gate, w_up, w_down, g_final):
    for l in range(DEPTH):
        mod = jax.nn.silu(c) @ w_ada[l].astype(c.dtype) + b_ada[l].astype(c.dtype)
        sh1, sc1, gt1, sh2, sc2, gt2 = [t[:, None, :] for t in jnp.split(mod, 6, axis=-1)]
        h = rmsnorm(x, g_norm1[l]) * (1.0 + sc1) + sh1
        x = x + gt1 * hybrid_mixer(h, positions, w_in[l], b_gates[l], w_conv[l], b_conv[l],
                                   g_mlstm_out[l], sink[l], g_attn_out[l], w_out[l])
        h = rmsnorm(x, g_norm2[l]) * (1.0 + sc2) + sh2
        x = x + gt2 * swiglu(h, w_gate[l], w_up[l], w_down[l])
    return rmsnorm(x, g_final)
```

```python
import functools

import jax
import jax.numpy as jnp
from jax import lax
from jax.experimental import pallas as pl
from jax.experimental.pallas import tpu as pltpu

F32 = jnp.float32
BF16 = jnp.bfloat16

EPS = 1e-6
NEG_INF = -1e30
ROPE_THETA = 10000.0

MLSTM_HEADS = 4
MLSTM_CHUNK = 128
ATTN_HEAD_DIM = 128
ATTN_KV_HEADS = 2
WINDOW = 128
ATTN_BLOCK = 128

LANES = 128
VMEM_LIMIT = 56 * 1024 * 1024

_NT = (((1,), (1,)), ((), ()))


def _sigmoid(x):
    return 1.0 / (1.0 + jnp.exp(-x))


def _params(sem, vmem=VMEM_LIMIT):
    return pltpu.CompilerParams(dimension_semantics=sem, vmem_limit_bytes=vmem)


def _ada_kernel(c_ref, w_ref, b_ref, o_ref):
    c = c_ref[...]
    s = c * _sigmoid(c)
    o_ref[...] = jnp.dot(s, w_ref[...], preferred_element_type=F32,
                         precision=lax.Precision.HIGHEST) + b_ref[...]


def _ada(c, w_ada, b_ada, tn=1024):
    B, D = c.shape
    N = w_ada.shape[1]
    return pl.pallas_call(
        _ada_kernel,
        grid=(N // tn,),
        in_specs=[pl.BlockSpec((B, D), lambda j: (0, 0)),
                  pl.BlockSpec((D, tn), lambda j: (0, j)),
                  pl.BlockSpec((1, tn), lambda j: (0, j))],
        out_specs=pl.BlockSpec((B, tn), lambda j: (0, j)),
        out_shape=jax.ShapeDtypeStruct((B, N), F32),
        compiler_params=_params(("arbitrary",)),
        name="ada",
    )(c, w_ada, b_ada.reshape(1, N))


def _rope(t, cos2, sin2):
    return t * cos2 + pltpu.roll(t, ATTN_HEAD_DIM // 2, 1) * sin2


def _inproj_kernel(x_ref, mod_ref, g_ref, pos_ref, inv_ref, sgn_ref, w_ref, wg_ref,
                   o_ref, gt_ref, h_scr, cos_scr, sin_scr, *, D, rope_full, rope_half):
    j = pl.program_id(1)

    @pl.when(j == 0)
    def _():
        x = x_ref[...]
        y = x * lax.rsqrt(jnp.mean(x * x, axis=-1, keepdims=True) + EPS) * g_ref[...]
        sh = mod_ref[0, :, 0:D]
        sc = mod_ref[0, :, D:2 * D]
        hb = (y * (1.0 + sc) + sh).astype(BF16)
        h_scr[...] = hb
        gt_ref[...] = lax.dot_general(wg_ref[...], hb, _NT, preferred_element_type=F32)
        ang = pos_ref[...].astype(F32) * inv_ref[...]
        cos_scr[...] = jnp.cos(ang)
        sin_scr[...] = jnp.sin(ang) * sgn_ref[...]

    acc = jnp.dot(h_scr[...], w_ref[...], preferred_element_type=F32)
    tn = acc.shape[1]
    n_heads = tn // ATTN_HEAD_DIM
    is_full = jnp.logical_and(j >= rope_full[0], j < rope_full[1])
    is_half = j == rope_half

    @pl.when(jnp.logical_not(jnp.logical_or(is_full, is_half)))
    def _():
        o_ref[...] = acc.astype(BF16)

    def roped(n_rope):
        cos2 = cos_scr[...]
        sin2 = sin_scr[...]
        parts = []
        for hh in range(n_heads):
            t = acc[:, hh * ATTN_HEAD_DIM:(hh + 1) * ATTN_HEAD_DIM]
            parts.append(_rope(t, cos2, sin2) if hh < n_rope else t)
        return jnp.concatenate(parts, axis=1).astype(BF16)

    @pl.when(is_full)
    def _():
        o_ref[...] = roped(n_heads)

    @pl.when(is_half)
    def _():
        o_ref[...] = roped(n_heads // 2)


def _inproj(x2, mod3, g1, pos2, w_main, w_gates_t, S, tm=1024, tn=512):
    N, D = x2.shape
    P = w_main.shape[1]
    NG = w_gates_t.shape[0]
    half = ATTN_HEAD_DIM // 2
    inv = ROPE_THETA ** (-jnp.arange(half, dtype=F32) * 2.0 / ATTN_HEAD_DIM)
    inv2 = jnp.concatenate([inv, inv]).reshape(1, ATTN_HEAD_DIM)
    sgn = jnp.concatenate([-jnp.ones((half,), F32), jnp.ones((half,), F32)]).reshape(1, ATTN_HEAD_DIM)
    aq0 = 4 * (D // 2)
    ak0 = aq0 + D // 2
    kvw = ATTN_KV_HEADS * ATTN_HEAD_DIM
    assert aq0 % tn == 0 and ak0 % tn == 0 and 2 * kvw == tn and ak0 + 2 * kvw == P
    kern = functools.partial(_inproj_kernel, D=D, rope_full=(aq0 // tn, ak0 // tn), rope_half=ak0 // tn)
    per_b = S // tm
    return pl.pallas_call(
        kern,
        grid=(N // tm, P // tn),
        in_specs=[pl.BlockSpec((tm, D), lambda i, j: (i, 0)),
                  pl.BlockSpec((1, 1, mod3.shape[2]), lambda i, j: (i // per_b, 0, 0)),
                  pl.BlockSpec((1, D), lambda i, j: (0, 0)),
                  pl.BlockSpec((tm, 1), lambda i, j: (i, 0)),
                  pl.BlockSpec((1, ATTN_HEAD_DIM), lambda i, j: (0, 0)),
                  pl.BlockSpec((1, ATTN_HEAD_DIM), lambda i, j: (0, 0)),
                  pl.BlockSpec((D, tn), lambda i, j: (0, j)),
                  pl.BlockSpec((NG, D), lambda i, j: (0, 0))],
        out_specs=[pl.BlockSpec((tm, tn), lambda i, j: (i, j)),
                   pl.BlockSpec((NG, tm), lambda i, j: (0, i))],
        out_shape=[jax.ShapeDtypeStruct((N, P), BF16),
                   jax.ShapeDtypeStruct((NG, N), F32)],
        scratch_shapes=[pltpu.VMEM((tm, D), BF16),
                        pltpu.VMEM((tm, ATTN_HEAD_DIM), F32),
                        pltpu.VMEM((tm, ATTN_HEAD_DIM), F32)],
        compiler_params=_params(("arbitrary", "arbitrary")),
        name="inproj",
    )(x2, mod3, g1, pos2, inv2, sgn, w_main, w_gates_t)


_R, _LF, _E, _M, _SC = 0, 1, 2, 3, 4
_NVEC = 5


def _log_sigmoid(x):
    return jnp.minimum(x, 0.0) - jnp.log1p(jnp.exp(-jnp.abs(x)))


def _chunk_cumsum(x, reverse):
    L = x.shape[1]
    lane = lax.broadcasted_iota(jnp.int32, x.shape, 1)
    sh = 1
    while sh < L:
        if reverse:
            x = x + jnp.where(lane < L - sh, pltpu.roll(x, L - sh, 1), 0.0)
        else:
            x = x + jnp.where(lane >= sh, pltpu.roll(x, sh, 1), 0.0)
        sh *= 2
    return x


def _gate_vectors(li, lf_pre, reverse):
    nc, L = li.shape
    lf = _log_sigmoid(lf_pre)
    b = _chunk_cumsum(lf, reverse)
    r = li - b
    b_end = b[:, 0:1] if reverse else b[:, L - 1:L]
    bl = jnp.broadcast_to(b_end, (nc, L))
    rmax = jnp.broadcast_to(jnp.max(r, axis=1, keepdims=True), (nc, L))
    row = lax.broadcasted_iota(jnp.int32, (nc, L), 0)
    m = jnp.zeros((1, L), F32)
    m_all = jnp.zeros((nc, L), F32)
    order = range(nc - 1, -1, -1) if reverse else range(nc)
    for j in order:
        m_all = jnp.where(row == j, m, m_all)
        m = bl[j:j + 1, :] + jnp.maximum(m, rmax[j:j + 1, :])
    m_end = jnp.maximum(m_all, rmax)
    return r, lf, jnp.exp(r - m_end), m_all, jnp.exp(m_all - m_end)


def _mlstm_kernel(q_ref, k_ref, v_ref, mo_ref, gr_ref, gb_ref, wq_ref, wk_ref, bq_ref, bk_ref, go_ref,
                  o_ref, pad_scr, qs_scr, ks_scr, vt32_scr, vt16_scr, vec_scr, st_scr, ns_scr,
                  ct_scr, n_scr, *, S, d, L, CW):
    nc = S // L
    half_w = CW // 2
    PADR = 8

    g = gr_ref[0, 0] + gb_ref[0]
    for dr in range(2):
        vecs = _gate_vectors(g[2 * dr], g[2 * dr + 1], reverse=(dr == 1))
        for idx, val in enumerate(vecs):
            vec_scr[dr * _NVEC + idx] = val

    zero_rows = jnp.zeros((PADR, d), F32)

    def conv_silu(src_ref, w_ref, b_ref, dst_ref, scale):
        pad_scr[0:PADR, :] = zero_rows
        pad_scr[PADR + S:2 * PADR + S, :] = zero_rows

        def fill(j, c):
            r0 = pl.multiple_of(j * L, L)
            pad_scr[pl.ds(PADR + r0, L), :] = src_ref[0, pl.ds(r0, L), :].astype(F32)
            return c
        lax.fori_loop(0, nc, fill, 0)
        w = w_ref[...]
        bias = b_ref[...]

        def body(j, c):
            r0 = pl.multiple_of(j * L, L)
            win = pad_scr[pl.ds(r0, L + 2 * PADR), :]
            y = bias
            for i in range(CW):
                o = PADR - half_w + i
                y = y + win[o:o + L, :] * w[i:i + 1, :]
            y = y * _sigmoid(y)
            if scale != 1.0:
                y = y * scale
            dst_ref[pl.ds(r0, L), :] = y.astype(BF16)
            return c
        lax.fori_loop(0, nc, body, 0)

    conv_silu(q_ref, wq_ref, bq_ref, qs_scr, 1.0)
    conv_silu(k_ref, wk_ref, bk_ref, ks_scr, d ** -0.5)

    def vtrans(j, c):
        r0 = pl.multiple_of(j * L, L)
        vt = v_ref[0, pl.ds(r0, L), :].astype(F32).T
        vt32_scr[j] = vt
        vt16_scr[j] = vt.astype(BF16)
        return c
    lax.fori_loop(0, nc, vtrans, 0)

    ns_scr[...] = jnp.zeros(ns_scr.shape, F32)
    ct_scr[...] = jnp.zeros(ct_scr.shape, F32)
    n_scr[...] = jnp.zeros(n_scr.shape, F32)

    def state_step(dr, jc):
        base = dr * _NVEC
        e_row = vec_scr[base + _E, pl.ds(jc, 1), :]
        sc = vec_scr[base + _SC, pl.ds(jc, 1), :][:, 0:1]
        r0 = pl.multiple_of(jc * L, L)
        lhs = jnp.concatenate([vt32_scr[jc] * e_row, jnp.broadcast_to(e_row, (16, L))], axis=0)
        upd = jnp.dot(lhs.astype(BF16), ks_scr[pl.ds(r0, L), :], preferred_element_type=F32)
        ct = ct_scr[dr]
        n = n_scr[dr]
        st_scr[dr, jc] = ct.astype(BF16)
        ns_scr[jc, dr:dr + 1, :] = n[0:1, :]
        ct_scr[dr] = sc * ct + upd[0:d, :]
        n_scr[dr] = sc * n + upd[d:d + 8, :]

    def phase1(j, c):
        state_step(0, j)
        state_step(1, nc - 1 - j)
        return c
    lax.fori_loop(0, nc, phase1, 0)

    rr = lax.broadcasted_iota(jnp.int32, (L, L), 0)
    cc = lax.broadcasted_iota(jnp.int32, (L, L), 1)
    tri = (cc <= rr, cc >= rr)
    zpad = jnp.zeros((2 * L - L - 16, d), BF16)
    gout = go_ref[...]

    def phase2(j, c):
        r0 = pl.multiple_of(j * L, L)
        qj = qs_scr[pl.ds(r0, L), :]
        kj = ks_scr[pl.ds(r0, L), :]
        kn = jnp.concatenate([kj, ns_scr[j].astype(BF16), zpad], axis=0)
        s1 = lax.dot_general(qj, kn, _NT, preferred_element_type=F32)
        qk = s1[:, 0:L]
        hsum = None
        for dr in range(2):
            base = dr * _NVEC
            r_row = vec_scr[base + _R, pl.ds(j, 1), :]
            lf_row = vec_scr[base + _LF, pl.ds(j, 1), :]
            m11 = vec_scr[base + _M, pl.ds(j, 1), :][:, 0:1]
            dm = jnp.where(tri[dr], r_row, NEG_INF)
            mt = jnp.maximum(jnp.max(dm, axis=1, keepdims=True), m11)
            qkw = qk * jnp.exp(dm - mt)
            bcol = jnp.sum(jnp.where(tri[dr], lf_row, 0.0), axis=1, keepdims=True)
            a = jnp.exp(m11 - mt)
            den = a * s1[:, L + dr:L + dr + 1] + jnp.sum(qkw, axis=1, keepdims=True)
            inter = lax.dot_general(qj, st_scr[dr, j], _NT, preferred_element_type=F32)
            intra = lax.dot_general(qkw.astype(BF16), vt16_scr[j], _NT, preferred_element_type=F32)
            hd = (a * inter + intra) / jnp.maximum(jnp.abs(den), jnp.exp(-(bcol + mt)))
            hsum = hd if hsum is None else hsum + hd
        mu = jnp.mean(hsum, axis=-1, keepdims=True)
        hc = hsum - mu
        hn = hc * lax.rsqrt(jnp.mean(hc * hc, axis=-1, keepdims=True) + EPS)
        mo = mo_ref[0, pl.ds(r0, L), :].astype(F32)
        o_ref[0, pl.ds(r0, L), :] = (_sigmoid(mo) * hn * gout).astype(BF16)
        return c
    lax.fori_loop(0, nc, phase2, 0)


def _mlstm(proj3, gates5, gbias, w_conv, b_conv, g_out):
    B, S, _ = proj3.shape
    H = MLSTM_HEADS
    d = g_out.shape[1] // H
    L = MLSTM_CHUNK
    nc = S // L
    CW = w_conv.shape[0]
    assert L == LANES and d == 2 * L and CW // 2 <= 8
    kern = functools.partial(_mlstm_kernel, S=S, d=d, L=L, CW=CW)
    col = lambda off: (lambda b, h: (b, 0, off + h))
    return pl.pallas_call(
        kern,
        grid=(B, H),
        in_specs=[pl.BlockSpec((1, S, d), col(0)),
                  pl.BlockSpec((1, S, d), col(H)),
                  pl.BlockSpec((1, S, d), col(2 * H)),
                  pl.BlockSpec((1, S, d), col(3 * H)),
                  pl.BlockSpec((1, 1, 4, nc, L), lambda b, h: (b, h, 0, 0, 0)),
                  pl.BlockSpec((1, 4, 1, 1), lambda b, h: (h, 0, 0, 0)),
                  pl.BlockSpec((CW, d), lambda b, h: (0, h)),
                  pl.BlockSpec((CW, d), lambda b, h: (0, H + h)),
                  pl.BlockSpec((1, d), lambda b, h: (0, h)),
                  pl.BlockSpec((1, d), lambda b, h: (0, H + h)),
                  pl.BlockSpec((1, d), lambda b, h: (0, h))],
        out_specs=pl.BlockSpec((1, S, d), lambda b, h: (b, 0, h)),
        out_shape=jax.ShapeDtypeStruct((B, S, H * d), BF16),
        scratch_shapes=[pltpu.VMEM((S + 16, d), F32),
                        pltpu.VMEM((S, d), BF16),
                        pltpu.VMEM((S, d), BF16),
                        pltpu.VMEM((nc, d, L), F32),
                        pltpu.VMEM((nc, d, L), BF16),
                        pltpu.VMEM((2 * _NVEC, nc, L), F32),
                        pltpu.VMEM((2, nc, d, d), BF16),
                        pltpu.VMEM((nc, 16, d), F32),
                        pltpu.VMEM((2, d, d), F32),
                        pltpu.VMEM((2, 8, d), F32)],
        compiler_params=_params(("arbitrary", "arbitrary")),
        name="mlstm",
    )(proj3, proj3, proj3, proj3, gates5, gbias, w_conv, w_conv, b_conv, b_conv, g_out)


def _attn_kernel(sink_ref, q_ref, kp_ref, kc_ref, kn_ref, vp_ref, vc_ref, vn_ref, g_ref, o_ref, *, S, H, G):
    n = pl.program_id(1)
    blk = ATTN_BLOCK
    hd = ATTN_HEAD_DIM
    R = H // G
    rr = lax.broadcasted_iota(jnp.int32, (blk, 3 * blk), 0)
    cc = lax.broadcasted_iota(jnp.int32, (blk, 3 * blk), 1)
    rel = cc - blk - rr
    kpos = (n - 1) * blk + cc
    valid1 = (jnp.abs(rel) <= WINDOW) & (kpos >= 0) & (kpos < S)
    valid = jnp.concatenate([valid1] * R, axis=0)
    q = q_ref[0]
    outs = []
    for g in range(G):
        ksl = slice(g * hd, (g + 1) * hd)
        kcat = jnp.concatenate([kp_ref[0][:, ksl], kc_ref[0][:, ksl], kn_ref[0][:, ksl]], axis=0)
        vcat = jnp.concatenate([vp_ref[0][:, ksl], vc_ref[0][:, ksl], vn_ref[0][:, ksl]], axis=0)
        qg = jnp.concatenate([q[:, (g * R + r) * hd:(g * R + r + 1) * hd] for r in range(R)], axis=0)
        s = lax.dot_general(qg, kcat, _NT, preferred_element_type=F32) * (hd ** -0.5)
        s = jnp.where(valid, s, NEG_INF)
        sink = jnp.concatenate([jnp.full((blk, 1), sink_ref[g * R + r], F32) for r in range(R)], axis=0)
        m = jnp.maximum(jnp.max(s, axis=1, keepdims=True), sink)
        p = jnp.exp(s - m)
        denom = jnp.sum(p, axis=1, keepdims=True) + jnp.exp(sink - m)
        p = p / denom
        o = jnp.dot(p.astype(BF16), vcat, preferred_element_type=F32)
        outs.extend(o[r * blk:(r + 1) * blk, :] for r in range(R))
    oall = jnp.concatenate(outs, axis=1)
    y = oall * lax.rsqrt(jnp.mean(oall * oall, axis=-1, keepdims=True) + EPS) * g_ref[...]
    o_ref[0] = y.astype(BF16)


def _attn(proj3, sink, g_attn, q_col0):
    B, S, _ = proj3.shape
    blk = ATTN_BLOCK
    nb = S // blk
    H = sink.shape[0]
    G = ATTN_KV_HEADS
    AW = H * ATTN_HEAD_DIM
    KW = G * ATTN_HEAD_DIM
    qb = q_col0 // AW
    kb = (q_col0 + AW) // KW
    vb = kb + 1
    assert q_col0 % AW == 0 and (q_col0 + AW) % KW == 0
    kern = functools.partial(_attn_kernel, S=S, H=H, G=G)
    prev = lambda cb: (lambda b, n: (b, jnp.maximum(n - 1, 0), cb))
    cur = lambda cb: (lambda b, n: (b, n, cb))
    nxt = lambda cb: (lambda b, n: (b, jnp.minimum(n + 1, nb - 1), cb))
    return pl.pallas_call(
        kern,
        grid=(B, nb),
        in_specs=[pl.BlockSpec(memory_space=pltpu.SMEM),
                  pl.BlockSpec((1, blk, AW), cur(qb)),
                  pl.BlockSpec((1, blk, KW), prev(kb)),
                  pl.BlockSpec((1, blk, KW), cur(kb)),
                  pl.BlockSpec((1, blk, KW), nxt(kb)),
                  pl.BlockSpec((1, blk, KW), prev(vb)),
                  pl.BlockSpec((1, blk, KW), cur(vb)),
                  pl.BlockSpec((1, blk, KW), nxt(vb)),
                  pl.BlockSpec((1, AW), lambda b, n: (0, 0))],
        out_specs=pl.BlockSpec((1, blk, AW), lambda b, n: (b, n, 0)),
        out_shape=jax.ShapeDtypeStruct((B, S, AW), BF16),
        compiler_params=_params(("arbitrary", "arbitrary")),
        name="attn",
    )(sink, proj3, proj3, proj3, proj3, proj3, proj3, proj3, g_attn)


def _outproj_kernel(m_ref, a_ref, x_ref, mod_ref, wt_ref, wb_ref, o_ref, *, D):
    acc = jnp.dot(m_ref[...], wt_ref[...], preferred_element_type=F32)
    acc = acc + jnp.dot(a_ref[...], wb_ref[...], preferred_element_type=F32)
    gt = mod_ref[0, :, 2 * D:3 * D]
    o_ref[...] = x_ref[...] + gt * acc


def _outproj(m2, a2, x2, mod3, w_top, w_bot, S, tm=512):
    N, D = x2.shape
    K = m2.shape[1]
    per_b = S // tm
    return pl.pallas_call(
        functools.partial(_outproj_kernel, D=D),
        grid=(N // tm,),
        in_specs=[pl.BlockSpec((tm, K), lambda i: (i, 0)),
                  pl.BlockSpec((tm, K), lambda i: (i, 0)),
                  pl.BlockSpec((tm, D), lambda i: (i, 0)),
                  pl.BlockSpec((1, 1, mod3.shape[2]), lambda i: (i // per_b, 0, 0)),
                  pl.BlockSpec((K, D), lambda i: (0, 0)),
                  pl.BlockSpec((K, D), lambda i: (0, 0))],
        out_specs=pl.BlockSpec((tm, D), lambda i: (i, 0)),
        out_shape=jax.ShapeDtypeStruct((N, D), F32),
        compiler_params=_params(("arbitrary",)),
        name="outproj",
    )(m2, a2, x2, mod3, w_top, w_bot)


def _ffn_kernel(x_ref, mod_ref, g2_ref, gf_ref, wg_ref, wu_ref, wd_ref, o_ref, h_scr, *, D):
    j = pl.program_id(1)

    @pl.when(j == 0)
    def _():
        x = x_ref[...]
        y = x * lax.rsqrt(jnp.mean(x * x, axis=-1, keepdims=True) + EPS) * g2_ref[...]
        sh = mod_ref[0, :, 3 * D:4 * D]
        sc = mod_ref[0, :, 4 * D:5 * D]
        h_scr[...] = (y * (1.0 + sc) + sh).astype(BF16)

    hb = h_scr[...]
    gate = jnp.dot(hb, wg_ref[...], preferred_element_type=F32)
    up = jnp.dot(hb, wu_ref[...], preferred_element_type=F32)
    act = (gate * _sigmoid(gate) * up).astype(BF16)
    part = jnp.dot(act, wd_ref[...], preferred_element_type=F32)

    @pl.when(j == 0)
    def _():
        o_ref[...] = part

    @pl.when(j > 0)
    def _():
        o_ref[...] += part

    @pl.when(j == pl.num_programs(1) - 1)
    def _():
        gt = mod_ref[0, :, 5 * D:6 * D]
        y = x_ref[...] + gt * o_ref[...]
        o_ref[...] = y * lax.rsqrt(jnp.mean(y * y, axis=-1, keepdims=True) + EPS) * gf_ref[...]


def _ffn(x2, mod3, g2, gf, w_gate, w_up, w_down, S, tm=512, tf=512):
    N, D = x2.shape
    FF = w_gate.shape[1]
    per_b = S // tm
    return pl.pallas_call(
        functools.partial(_ffn_kernel, D=D),
        grid=(N // tm, FF // tf),
        in_specs=[pl.BlockSpec((tm, D), lambda i, j: (i, 0)),
                  pl.BlockSpec((1, 1, mod3.shape[2]), lambda i, j: (i // per_b, 0, 0)),
                  pl.BlockSpec((1, D), lambda i, j: (0, 0)),
                  pl.BlockSpec((1, D), lambda i, j: (0, 0)),
                  pl.BlockSpec((D, tf), lambda i, j: (0, j)),
                  pl.BlockSpec((D, tf), lambda i, j: (0, j)),
                  pl.BlockSpec((tf, D), lambda i, j: (j, 0))],
        out_specs=pl.BlockSpec((tm, D), lambda i, j: (i, 0)),
        out_shape=jax.ShapeDtypeStruct((N, D), F32),
        scratch_shapes=[pltpu.VMEM((tm, D), BF16)],
        compiler_params=_params(("arbitrary", "arbitrary")),
        name="ffn",
    )(x2, mod3, g2, gf, w_gate, w_up, w_down)


def kernel(x, c, positions, w_ada, b_ada, g_norm1, g_norm2, w_in, b_gates, w_conv, b_conv, g_mlstm_out,
           sink, g_attn_out, w_out, w_gate, w_up, w_down, g_final):
    B, S, D = x.shape
    depth = w_ada.shape[0]
    H = MLSTM_HEADS
    MW = D // 2
    NG = 4 * H
    L = MLSTM_CHUNK
    nc = S // L
    g0 = 4 * MW
    x2 = x.reshape(B * S, D)
    pos2 = positions.reshape(B * S, 1)
    for l in range(depth):
        mod3 = _ada(c, w_ada[l], b_ada[l]).reshape(B, 1, 6 * D)
        w_main = jnp.concatenate([w_in[l][:, :g0], w_in[l][:, g0 + NG:]], axis=1).astype(BF16)
        w_gates_t = w_in[l][:, g0:g0 + NG].T.astype(BF16)
        proj, gates_t = _inproj(x2, mod3, g_norm1[l].reshape(1, D), pos2, w_main, w_gates_t, S)
        proj3 = proj.reshape(B, S, proj.shape[1])
        gates5 = gates_t.reshape(4, H, B, nc, L).transpose(2, 1, 0, 3, 4)
        gbias = b_gates[l].reshape(4, H).T.reshape(H, 4, 1, 1)
        m_out = _mlstm(proj3, gates5, gbias, w_conv[l], b_conv[l].reshape(1, 2 * MW),
                       g_mlstm_out[l].reshape(1, MW))
        a_out = _attn(proj3, sink[l], g_attn_out[l].reshape(1, MW), q_col0=g0)
        w_o = w_out[l].astype(BF16)
        x2 = _outproj(m_out.reshape(B * S, MW), a_out.reshape(B * S, MW), x2, mod3, w_o[:MW], w_o[MW:], S)
        last = l == depth - 1
        assert last, "final norm is fused into the last layer's FFN"
        x2 = _ffn(x2, mod3, g_norm2[l].reshape(1, D), g_final.reshape(1, D),
                  w_gate[l].astype(BF16), w_up[l].astype(BF16), w_down[l].astype(BF16), S)
    return x2.reshape(B, S, D)
```

```python
import functools

import jax
import jax.numpy as jnp
from jax import lax
from jax.experimental import pallas as pl
from jax.experimental.pallas import tpu as pltpu

F32 = jnp.float32
BF16 = jnp.bfloat16

EPS = 1e-6
NEG_INF = -1e30
ROPE_THETA = 10000.0

MLSTM_HEADS = 4
MLSTM_CHUNK = 128
ATTN_HEAD_DIM = 128
ATTN_KV_HEADS = 2
WINDOW = 128
ATTN_BLOCK = 128

LANES = 128
BF16_ROWS = 16
VMEM_LIMIT = 56 * 1024 * 1024

_NT = (((1,), (1,)), ((), ()))


def _sigmoid(x):
    return 1.0 / (1.0 + jnp.exp(-x))


def _params(sem, vmem=VMEM_LIMIT):
    return pltpu.CompilerParams(dimension_semantics=sem, vmem_limit_bytes=vmem)


def _ada_kernel(c_ref, w_ref, b_ref, o_ref):
    c = c_ref[...]
    s = c * _sigmoid(c)
    o_ref[...] = jnp.dot(s, w_ref[...], preferred_element_type=F32,
                         precision=lax.Precision.HIGHEST) + b_ref[...]


def _ada(c, w_ada, b_ada, tn=1024):
    B, D = c.shape
    N = w_ada.shape[1]
    return pl.pallas_call(
        _ada_kernel,
        grid=(N // tn,),
        in_specs=[pl.BlockSpec((B, D), lambda j: (0, 0)),
                  pl.BlockSpec((D, tn), lambda j: (0, j)),
                  pl.BlockSpec((1, tn), lambda j: (0, j))],
        out_specs=pl.BlockSpec((B, tn), lambda j: (0, j)),
        out_shape=jax.ShapeDtypeStruct((B, N), F32),
        compiler_params=_params(("arbitrary",)),
        name="ada",
    )(c, w_ada, b_ada.reshape(1, N))


def _rope(t, cos2, sin2):
    return t * cos2 + pltpu.roll(t, ATTN_HEAD_DIM // 2, 1) * sin2


def _inproj_kernel(x_ref, mod_ref, g_ref, pos_ref, inv_ref, sgn_ref, w_ref, wv_ref, wg_ref,
                   o_ref, vt_ref, gt_ref, h_scr, cos_scr, sin_scr, *, D, n_main, rope_full, rope_half):
    j = pl.program_id(1)

    @pl.when(j == 0)
    def _():
        x = x_ref[...]
        y = x * lax.rsqrt(jnp.mean(x * x, axis=-1, keepdims=True) + EPS) * g_ref[...]
        sh = mod_ref[0, :, 0:D]
        sc = mod_ref[0, :, D:2 * D]
        hb = (y * (1.0 + sc) + sh).astype(BF16)
        h_scr[...] = hb
        gt_ref[...] = lax.dot_general(wg_ref[...], hb, _NT, preferred_element_type=F32)
        ang = pos_ref[...].astype(F32) * inv_ref[...]
        cos_scr[...] = jnp.cos(ang)
        sin_scr[...] = jnp.sin(ang) * sgn_ref[...]

    is_full = jnp.logical_and(j >= rope_full[0], j < rope_full[1])
    is_half = j == rope_half
    is_vt = j >= n_main

    def main_acc():
        return jnp.dot(h_scr[...], w_ref[...], preferred_element_type=F32)

    @pl.when(jnp.logical_not(jnp.logical_or(jnp.logical_or(is_full, is_half), is_vt)))
    def _():
        o_ref[...] = main_acc().astype(BF16)

    def roped(n_rope):
        acc = main_acc()
        n_heads = acc.shape[1] // ATTN_HEAD_DIM
        cos2 = cos_scr[...]
        sin2 = sin_scr[...]
        parts = []
        for hh in range(n_heads):
            t = acc[:, hh * ATTN_HEAD_DIM:(hh + 1) * ATTN_HEAD_DIM]
            parts.append(_rope(t, cos2, sin2) if hh < n_rope else t)
        return jnp.concatenate(parts, axis=1).astype(BF16)

    @pl.when(is_full)
    def _():
        o_ref[...] = roped(o_ref.shape[1] // ATTN_HEAD_DIM)

    @pl.when(is_half)
    def _():
        o_ref[...] = roped(o_ref.shape[1] // ATTN_HEAD_DIM // 2)

    @pl.when(is_vt)
    def _():
        vt_ref[0] = lax.dot_general(wv_ref[...], h_scr[...], _NT, preferred_element_type=F32).astype(BF16)


def _inproj(x2, mod3, g1, pos2, w_main, w_vt, w_gates_t, B, S, tm=1024, tn=512):
    N, D = x2.shape
    P = w_main.shape[1]
    VW = w_vt.shape[0]
    NG = w_gates_t.shape[0]
    half = ATTN_HEAD_DIM // 2
    inv = ROPE_THETA ** (-jnp.arange(half, dtype=F32) * 2.0 / ATTN_HEAD_DIM)
    inv2 = jnp.concatenate([inv, inv]).reshape(1, ATTN_HEAD_DIM)
    sgn = jnp.concatenate([-jnp.ones((half,), F32), jnp.ones((half,), F32)]).reshape(1, ATTN_HEAD_DIM)
    aq0 = 3 * (D // 2)
    ak0 = aq0 + D // 2
    kvw = ATTN_KV_HEADS * ATTN_HEAD_DIM
    assert aq0 % tn == 0 and ak0 % tn == 0 and 2 * kvw == tn and ak0 + 2 * kvw == P and VW % tn == 0
    n_main = P // tn
    n_vt = VW // tn
    kern = functools.partial(_inproj_kernel, D=D, n_main=n_main,
                             rope_full=(aq0 // tn, ak0 // tn), rope_half=ak0 // tn)
    per_b = S // tm
    main_j = lambda j: jnp.minimum(j, n_main - 1)
    vt_j = lambda j: jnp.maximum(j - n_main, 0)
    return pl.pallas_call(
        kern,
        grid=(N // tm, n_main + n_vt),
        in_specs=[pl.BlockSpec((tm, D), lambda i, j: (i, 0)),
                  pl.BlockSpec((1, 1, mod3.shape[2]), lambda i, j: (i // per_b, 0, 0)),
                  pl.BlockSpec((1, D), lambda i, j: (0, 0)),
                  pl.BlockSpec((tm, 1), lambda i, j: (i, 0)),
                  pl.BlockSpec((1, ATTN_HEAD_DIM), lambda i, j: (0, 0)),
                  pl.BlockSpec((1, ATTN_HEAD_DIM), lambda i, j: (0, 0)),
                  pl.BlockSpec((D, tn), lambda i, j: (0, main_j(j))),
                  pl.BlockSpec((tn, D), lambda i, j: (vt_j(j), 0)),
                  pl.BlockSpec((NG, D), lambda i, j: (0, 0))],
        out_specs=[pl.BlockSpec((tm, tn), lambda i, j: (i, main_j(j))),
                   pl.BlockSpec((1, tn, tm), lambda i, j: (i // per_b, vt_j(j), i % per_b)),
                   pl.BlockSpec((NG, tm), lambda i, j: (0, i))],
        out_shape=[jax.ShapeDtypeStruct((N, P), BF16),
                   jax.ShapeDtypeStruct((B, VW, S), BF16),
                   jax.ShapeDtypeStruct((NG, N), F32)],
        scratch_shapes=[pltpu.VMEM((tm, D), BF16),
                        pltpu.VMEM((tm, ATTN_HEAD_DIM), F32),
                        pltpu.VMEM((tm, ATTN_HEAD_DIM), F32)],
        compiler_params=_params(("arbitrary", "arbitrary")),
        name="inproj",
    )(x2, mod3, g1, pos2, inv2, sgn, w_main, w_vt, w_gates_t)


_R, _MT, _A, _LIM, _E, _SC = range(6)
_NVEC = 6


def _log_sigmoid(x):
    return jnp.minimum(x, 0.0) - jnp.log1p(jnp.exp(-jnp.abs(x)))


def _chunk_scan(x, reverse, op, ident):
    L = x.shape[1]
    lane = lax.broadcasted_iota(jnp.int32, x.shape, 1)
    sh = 1
    while sh < L:
        if reverse:
            x = op(x, jnp.where(lane < L - sh, pltpu.roll(x, L - sh, 1), ident))
        else:
            x = op(x, jnp.where(lane >= sh, pltpu.roll(x, sh, 1), ident))
        sh *= 2
    return x


def _gate_vectors(li, lf_pre, reverse):
    nc, L = li.shape
    lf = _log_sigmoid(lf_pre)
    b = _chunk_scan(lf, reverse, jnp.add, 0.0)
    r = li - b
    rcm = _chunk_scan(r, reverse, jnp.maximum, NEG_INF)
    end = slice(0, 1) if reverse else slice(L - 1, L)
    bl = jnp.broadcast_to(b[:, end], (nc, L))
    rmax = jnp.broadcast_to(rcm[:, end], (nc, L))
    row = lax.broadcasted_iota(jnp.int32, (nc, L), 0)
    m = jnp.zeros((1, L), F32)
    m_all = jnp.zeros((nc, L), F32)
    order = range(nc - 1, -1, -1) if reverse else range(nc)
    for j in order:
        m_all = jnp.where(row == j, m, m_all)
        m = bl[j:j + 1, :] + jnp.maximum(m, rmax[j:j + 1, :])
    mt = jnp.maximum(m_all, rcm)
    m_end = jnp.maximum(m_all, rmax)
    return r, mt, jnp.exp(m_all - mt), jnp.exp(-(b + mt)), jnp.exp(r - m_end), jnp.exp(m_all - m_end)


def _mlstm_kernel(q_ref, k_ref, vt_ref, mo_ref, gr_ref, gb_ref, wq_ref, wk_ref, bq_ref, bk_ref, go_ref,
                  o_ref, pad_scr, qs_scr, big_scr, vt_scr, vec_scr, ns_scr, ct_scr, n_scr,
                  *, S, d, L, CW):
    nc = S // L
    half_w = CW // 2
    PADR = 8
    K0 = 0
    N0 = L
    C0 = L + BF16_ROWS

    g = gr_ref[0, 0] + gb_ref[0]
    for dr in range(2):
        vecs = _gate_vectors(g[2 * dr], g[2 * dr + 1], reverse=(dr == 1))
        for idx, val in enumerate(vecs):
            vec_scr[dr * _NVEC + idx] = val

    zero_rows = jnp.zeros((PADR, d), F32)

    def conv_silu(src_ref, w_ref, b_ref, store, scale):
        pad_scr[0:PADR, :] = zero_rows
        pad_scr[PADR + S:2 * PADR + S, :] = zero_rows

        def fill(j, c):
            r0 = pl.multiple_of(j * L, L)
            pad_scr[pl.ds(PADR + r0, L), :] = src_ref[0, pl.ds(r0, L), :].astype(F32)
            return c
        lax.fori_loop(0, nc, fill, 0)
        w = w_ref[...]
        bias = b_ref[...]

        def body(j, c):
            r0 = pl.multiple_of(j * L, L)
            win = pad_scr[pl.ds(r0, L + 2 * PADR), :]
            y = bias
            for i in range(CW):
                o = PADR - half_w + i
                y = y + win[o:o + L, :] * w[i:i + 1, :]
            y = y * _sigmoid(y)
            if scale != 1.0:
                y = y * scale
            store(j, r0, y.astype(BF16))
            return c
        lax.fori_loop(0, nc, body, 0)

    def store_q(j, r0, y):
        qs_scr[pl.ds(r0, L), :] = y

    def store_k(j, r0, y):
        big_scr[j, K0:K0 + L, :] = y

    conv_silu(q_ref, wq_ref, bq_ref, store_q, 1.0)
    conv_silu(k_ref, wk_ref, bk_ref, store_k, d ** -0.5)

    for j in range(nc):
        vt_scr[j] = vt_ref[0, :, j * L:(j + 1) * L]

    ns_scr[...] = jnp.zeros(ns_scr.shape, F32)
    ct_scr[...] = jnp.zeros(ct_scr.shape, F32)
    n_scr[...] = jnp.zeros(n_scr.shape, F32)

    def state_step(dr, jc):
        base = dr * _NVEC
        e_row = vec_scr[base + _E, pl.ds(jc, 1), :]
        sc = vec_scr[base + _SC, pl.ds(jc, 1), :]
        sc2 = jnp.concatenate([sc] * (d // L), axis=1)
        lhs = jnp.concatenate([vt_scr[jc].astype(F32) * e_row, jnp.broadcast_to(e_row, (BF16_ROWS, L))], axis=0)
        upd = jnp.dot(lhs.astype(BF16), big_scr[jc, K0:K0 + L, :], preferred_element_type=F32)
        ct = ct_scr[dr]
        n = n_scr[dr]
        big_scr[jc, C0 + dr * d:C0 + (dr + 1) * d, :] = ct.astype(BF16)
        ns_scr[jc, dr:dr + 1, :] = n[0:1, :]
        ct_scr[dr] = sc2 * ct + upd[0:d, :]
        n_scr[dr] = sc2 * n + upd[d:d + 8, :]

    def phase1(j, c):
        state_step(0, j)
        state_step(1, nc - 1 - j)
        return c
    lax.fori_loop(0, nc, phase1, 0, unroll=2)

    def put_n(j, c):
        big_scr[j, N0:N0 + BF16_ROWS, :] = ns_scr[j].astype(BF16)
        return c
    lax.fori_loop(0, nc, put_n, 0)

    rr = lax.broadcasted_iota(jnp.int32, (L, L), 0)
    cc = lax.broadcasted_iota(jnp.int32, (L, L), 1)
    eye = rr == cc
    valid = (rr <= cc, rr >= cc)
    gout = go_ref[...]

    def phase2(j, c):
        r0 = pl.multiple_of(j * L, L)
        qj = qs_scr[pl.ds(r0, L), :]
        p = lax.dot_general(big_scr[j], qj, _NT, preferred_element_type=F32)
        st = p[K0:K0 + L, :]
        qkw = []
        inv_den = []
        a_rows = []
        for dr in range(2):
            base = dr * _NVEC
            r_row = vec_scr[base + _R, pl.ds(j, 1), :]
            mt_row = vec_scr[base + _MT, pl.ds(j, 1), :]
            a_row = vec_scr[base + _A, pl.ds(j, 1), :]
            lim_row = vec_scr[base + _LIM, pl.ds(j, 1), :]
            r_col = jnp.sum(jnp.where(eye, r_row, 0.0), axis=1, keepdims=True)
            w = jnp.exp(jnp.where(valid[dr], r_col - mt_row, NEG_INF))
            qkw_d = st * w
            den = a_row * p[N0 + dr:N0 + dr + 1, :] + jnp.sum(qkw_d, axis=0, keepdims=True)
            inv_den.append(1.0 / jnp.maximum(jnp.abs(den), lim_row))
            a_rows.append(a_row)
            qkw.append(qkw_d.astype(BF16))
        intra = jnp.dot(vt_scr[j], jnp.concatenate(qkw, axis=1), preferred_element_type=F32)
        hs = None
        for dr in range(2):
            inter = p[C0 + dr * d:C0 + (dr + 1) * d, :]
            hd = (a_rows[dr] * inter + intra[:, dr * L:(dr + 1) * L]) * inv_den[dr]
            hs = hd if hs is None else hs + hd
        mu = jnp.mean(hs, axis=0, keepdims=True)
        hc = hs - mu
        hn = (hc * lax.rsqrt(jnp.mean(hc * hc, axis=0, keepdims=True) + EPS)).T
        mo = mo_ref[0, pl.ds(r0, L), :].astype(F32)
        o_ref[0, pl.ds(r0, L), :] = (_sigmoid(mo) * hn * gout).astype(BF16)
        return c
    lax.fori_loop(0, nc, phase2, 0, unroll=2)


def _mlstm(proj3, vt3, gates5, gbias, w_conv, b_conv, g_out):
    B, S, _ = proj3.shape
    H = MLSTM_HEADS
    d = g_out.shape[1] // H
    L = MLSTM_CHUNK
    nc = S // L
    CW = w_conv.shape[0]
    assert L == LANES and d == 2 * L and CW // 2 <= 8
    kern = functools.partial(_mlstm_kernel, S=S, d=d, L=L, CW=CW)
    col = lambda off: (lambda b, h: (b, 0, off + h))
    return pl.pallas_call(
        kern,
        grid=(B, H),
        in_specs=[pl.BlockSpec((1, S, d), col(0)),
                  pl.BlockSpec((1, S, d), col(H)),
                  pl.BlockSpec((1, d, S), lambda b, h: (b, h, 0)),
                  pl.BlockSpec((1, S, d), col(2 * H)),
                  pl.BlockSpec((1, 1, 4, nc, L), lambda b, h: (b, h, 0, 0, 0)),
                  pl.BlockSpec((1, 4, 1, 1), lambda b, h: (h, 0, 0, 0)),
                  pl.BlockSpec((CW, d), lambda b, h: (0, h)),
                  pl.BlockSpec((CW, d), lambda b, h: (0, H + h)),
                  pl.BlockSpec((1, d), lambda b, h: (0, h)),
                  pl.BlockSpec((1, d), lambda b, h: (0, H + h)),
                  pl.BlockSpec((1, d), lambda b, h: (0, h))],
        out_specs=pl.BlockSpec((1, S, d), lambda b, h: (b, 0, h)),
        out_shape=jax.ShapeDtypeStruct((B, S, H * d), BF16),
        scratch_shapes=[pltpu.VMEM((S + 16, d), F32),
                        pltpu.VMEM((S, d), BF16),
                        pltpu.VMEM((nc, L + BF16_ROWS + 2 * d, d), BF16),
                        pltpu.VMEM((nc, d, L), BF16),
                        pltpu.VMEM((2 * _NVEC, nc, L), F32),
                        pltpu.VMEM((nc, BF16_ROWS, d), F32),
                        pltpu.VMEM((2, d, d), F32),
                        pltpu.VMEM((2, 8, d), F32)],
        compiler_params=_params(("arbitrary", "arbitrary")),
        name="mlstm",
    )(proj3, proj3, vt3, proj3, gates5, gbias, w_conv, w_conv, b_conv, b_conv, g_out)


def _attn_kernel(sink_ref, q_ref, kp_ref, kc_ref, kn_ref, vp_ref, vc_ref, vn_ref, g_ref, o_ref, *, S, H, G):
    n = pl.program_id(1)
    blk = ATTN_BLOCK
    hd = ATTN_HEAD_DIM
    R = H // G
    rr = lax.broadcasted_iota(jnp.int32, (blk, 3 * blk), 0)
    cc = lax.broadcasted_iota(jnp.int32, (blk, 3 * blk), 1)
    rel = cc - blk - rr
    kpos = (n - 1) * blk + cc
    valid1 = (jnp.abs(rel) <= WINDOW) & (kpos >= 0) & (kpos < S)
    valid = jnp.concatenate([valid1] * R, axis=0)
    q = q_ref[0]
    outs = []
    for g in range(G):
        ksl = slice(g * hd, (g + 1) * hd)
        kcat = jnp.concatenate([kp_ref[0][:, ksl], kc_ref[0][:, ksl], kn_ref[0][:, ksl]], axis=0)
        vcat = jnp.concatenate([vp_ref[0][:, ksl], vc_ref[0][:, ksl], vn_ref[0][:, ksl]], axis=0)
        qg = jnp.concatenate([q[:, (g * R + r) * hd:(g * R + r + 1) * hd] for r in range(R)], axis=0)
        s = lax.dot_general(qg, kcat, _NT, preferred_element_type=F32) * (hd ** -0.5)
        s = jnp.where(valid, s, NEG_INF)
        sink = jnp.concatenate([jnp.full((blk, 1), sink_ref[g * R + r], F32) for r in range(R)], axis=0)
        m = jnp.maximum(jnp.max(s, axis=1, keepdims=True), sink)
        p = jnp.exp(s - m)
        denom = jnp.sum(p, axis=1, keepdims=True) + jnp.exp(sink - m)
        p = p / denom
        o = jnp.dot(p.astype(BF16), vcat, preferred_element_type=F32)
        outs.extend(o[r * blk:(r + 1) * blk, :] for r in range(R))
    oall = jnp.concatenate(outs, axis=1)
    y = oall * lax.rsqrt(jnp.mean(oall * oall, axis=-1, keepdims=True) + EPS) * g_ref[...]
    o_ref[0] = y.astype(BF16)


def _attn(proj3, sink, g_attn, q_col0):
    B, S, _ = proj3.shape
    blk = ATTN_BLOCK
    nb = S // blk
    H = sink.shape[0]
    G = ATTN_KV_HEADS
    AW = H * ATTN_HEAD_DIM
    KW = G * ATTN_HEAD_DIM
    qb = q_col0 // AW
    kb = (q_col0 + AW) // KW
    vb = kb + 1
    assert q_col0 % AW == 0 and (q_col0 + AW) % KW == 0
    kern = functools.partial(_attn_kernel, S=S, H=H, G=G)
    prev = lambda cb: (lambda b, n: (b, jnp.maximum(n - 1, 0), cb))
    cur = lambda cb: (lambda b, n: (b, n, cb))
    nxt = lambda cb: (lambda b, n: (b, jnp.minimum(n + 1, nb - 1), cb))
    return pl.pallas_call(
        kern,
        grid=(B, nb),
        in_specs=[pl.BlockSpec(memory_space=pltpu.SMEM),
                  pl.BlockSpec((1, blk, AW), cur(qb)),
                  pl.BlockSpec((1, blk, KW), prev(kb)),
                  pl.BlockSpec((1, blk, KW), cur(kb)),
                  pl.BlockSpec((1, blk, KW), nxt(kb)),
                  pl.BlockSpec((1, blk, KW), prev(vb)),
                  pl.BlockSpec((1, blk, KW), cur(vb)),
                  pl.BlockSpec((1, blk, KW), nxt(vb)),
                  pl.BlockSpec((1, AW), lambda b, n: (0, 0))],
        out_specs=pl.BlockSpec((1, blk, AW), lambda b, n: (b, n, 0)),
        out_shape=jax.ShapeDtypeStruct((B, S, AW), BF16),
        compiler_params=_params(("arbitrary", "arbitrary")),
        name="attn",
    )(sink, proj3, proj3, proj3, proj3, proj3, proj3, proj3, g_attn)


def _outproj_kernel(m_ref, a_ref, x_ref, mod_ref, wt_ref, wb_ref, o_ref, *, D):
    acc = jnp.dot(m_ref[...], wt_ref[...], preferred_element_type=F32)
    acc = acc + jnp.dot(a_ref[...], wb_ref[...], preferred_element_type=F32)
    gt = mod_ref[0, :, 2 * D:3 * D]
    o_ref[...] = x_ref[...] + gt * acc


def _outproj(m2, a2, x2, mod3, w_top, w_bot, S, tm=512):
    N, D = x2.shape
    K = m2.shape[1]
    per_b = S // tm
    return pl.pallas_call(
        functools.partial(_outproj_kernel, D=D),
        grid=(N // tm,),
        in_specs=[pl.BlockSpec((tm, K), lambda i: (i, 0)),
                  pl.BlockSpec((tm, K), lambda i: (i, 0)),
                  pl.BlockSpec((tm, D), lambda i: (i, 0)),
                  pl.BlockSpec((1, 1, mod3.shape[2]), lambda i: (i // per_b, 0, 0)),
                  pl.BlockSpec((K, D), lambda i: (0, 0)),
                  pl.BlockSpec((K, D), lambda i: (0, 0))],
        out_specs=pl.BlockSpec((tm, D), lambda i: (i, 0)),
        out_shape=jax.ShapeDtypeStruct((N, D), F32),
        compiler_params=_params(("arbitrary",)),
        name="outproj",
    )(m2, a2, x2, mod3, w_top, w_bot)


def _ffn_kernel(x_ref, mod_ref, g2_ref, gf_ref, wg_ref, wu_ref, wd_ref, o_ref, h_scr, *, D):
    j = pl.program_id(1)

    @pl.when(j == 0)
    def _():
        x = x_ref[...]
        y = x * lax.rsqrt(jnp.mean(x * x, axis=-1, keepdims=True) + EPS) * g2_ref[...]
        sh = mod_ref[0, :, 3 * D:4 * D]
        sc = mod_ref[0, :, 4 * D:5 * D]
        h_scr[...] = (y * (1.0 + sc) + sh).astype(BF16)

    hb = h_scr[...]
    gate = jnp.dot(hb, wg_ref[...], preferred_element_type=F32)
    up = jnp.dot(hb, wu_ref[...], preferred_element_type=F32)
    act = (gate * _sigmoid(gate) * up).astype(BF16)
    part = jnp.dot(act, wd_ref[...], preferred_element_type=F32)

    @pl.when(j == 0)
    def _():
        o_ref[...] = part

    @pl.when(j > 0)
    def _():
        o_ref[...] += part

    @pl.when(j == pl.num_programs(1) - 1)
    def _():
        gt = mod_ref[0, :, 5 * D:6 * D]
        y = x_ref[...] + gt * o_ref[...]
        o_ref[...] = y * lax.rsqrt(jnp.mean(y * y, axis=-1, keepdims=True) + EPS) * gf_ref[...]


def _ffn(x2, mod3, g2, gf, w_gate, w_up, w_down, S, tm=512, tf=512):
    N, D = x2.shape
    FF = w_gate.shape[1]
    per_b = S // tm
    return pl.pallas_call(
        functools.partial(_ffn_kernel, D=D),
        grid=(N // tm, FF // tf),
        in_specs=[pl.BlockSpec((tm, D), lambda i, j: (i, 0)),
                  pl.BlockSpec((1, 1, mod3.shape[2]), lambda i, j: (i // per_b, 0, 0)),
                  pl.BlockSpec((1, D), lambda i, j: (0, 0)),
                  pl.BlockSpec((1, D), lambda i, j: (0, 0)),
                  pl.BlockSpec((D, tf), lambda i, j: (0, j)),
                  pl.BlockSpec((D, tf), lambda i, j: (0, j)),
                  pl.BlockSpec((tf, D), lambda i, j: (j, 0))],
        out_specs=pl.BlockSpec((tm, D), lambda i, j: (i, 0)),
        out_shape=jax.ShapeDtypeStruct((N, D), F32),
        scratch_shapes=[pltpu.VMEM((tm, D), BF16)],
        compiler_params=_params(("arbitrary", "arbitrary")),
        name="ffn",
    )(x2, mod3, g2, gf, w_gate, w_up, w_down)


def kernel(x, c, positions, w_ada, b_ada, g_norm1, g_norm2, w_in, b_gates, w_conv, b_conv, g_mlstm_out,
           sink, g_attn_out, w_out, w_gate, w_up, w_down, g_final):
    B, S, D = x.shape
    depth = w_ada.shape[0]
    H = MLSTM_HEADS
    MW = D // 2
    NG = 4 * H
    L = MLSTM_CHUNK
    nc = S // L
    g0 = 4 * MW
    x2 = x.reshape(B * S, D)
    pos2 = positions.reshape(B * S, 1)
    for l in range(depth):
        mod3 = _ada(c, w_ada[l], b_ada[l]).reshape(B, 1, 6 * D)
        wl = w_in[l]
        w_main = jnp.concatenate([wl[:, :2 * MW], wl[:, 3 * MW:g0], wl[:, g0 + NG:]], axis=1).astype(BF16)
        w_vt = wl[:, 2 * MW:3 * MW].T.astype(BF16)
        w_gates_t = wl[:, g0:g0 + NG].T.astype(BF16)
        proj, vt3, gates_t = _inproj(x2, mod3, g_norm1[l].reshape(1, D), pos2, w_main, w_vt, w_gates_t, B, S)
        proj3 = proj.reshape(B, S, proj.shape[1])
        gates5 = gates_t.reshape(4, H, B, nc, L).transpose(2, 1, 0, 3, 4)
        gbias = b_gates[l].reshape(4, H).T.reshape(H, 4, 1, 1)
        m_out = _mlstm(proj3, vt3, gates5, gbias, w_conv[l], b_conv[l].reshape(1, 2 * MW),
                       g_mlstm_out[l].reshape(1, MW))
        a_out = _attn(proj3, sink[l], g_attn_out[l].reshape(1, MW), q_col0=3 * MW)
        w_o = w_out[l].astype(BF16)
        x2 = _outproj(m_out.reshape(B * S, MW), a_out.reshape(B * S, MW), x2, mod3, w_o[:MW], w_o[MW:], S)
        last = l == depth - 1
        assert last, "final norm is fused into the last layer's FFN"
        x2 = _ffn(x2, mod3, g_norm2[l].reshape(1, D), g_final.reshape(1, D),
                  w_gate[l].astype(BF16), w_up[l].astype(BF16), w_down[l].astype(BF16), S)
    return x2.reshape(B, S, D)
```

```python
import functools

import jax
import jax.numpy as jnp
from jax import lax
from jax.experimental import pallas as pl
from jax.experimental.pallas import tpu as pltpu

F32 = jnp.float32
BF16 = jnp.bfloat16

EPS = 1e-6
NEG_INF = -1e30
ROPE_THETA = 10000.0

MLSTM_HEADS = 4
MLSTM_CHUNK = 128
ATTN_HEAD_DIM = 128
ATTN_KV_HEADS = 2
WINDOW = 128
ATTN_BLOCK = 128

LANES = 128
BF16_ROWS = 16
VMEM_LIMIT = 56 * 1024 * 1024

_NT = (((1,), (1,)), ((), ()))


def _sigmoid(x):
    return 1.0 / (1.0 + jnp.exp(-x))


def _params(sem, vmem=VMEM_LIMIT):
    return pltpu.CompilerParams(dimension_semantics=sem, vmem_limit_bytes=vmem)


def _ada_kernel(c_ref, w_ref, b_ref, o_ref):
    c = c_ref[...]
    s = c * _sigmoid(c)
    o_ref[...] = jnp.dot(s, w_ref[...], preferred_element_type=F32,
                         precision=lax.Precision.HIGHEST) + b_ref[...]


def _ada(c, w_ada, b_ada, tn=1024):
    B, D = c.shape
    N = w_ada.shape[1]
    return pl.pallas_call(
        _ada_kernel,
        grid=(N // tn,),
        in_specs=[pl.BlockSpec((B, D), lambda j: (0, 0)),
                  pl.BlockSpec((D, tn), lambda j: (0, j)),
                  pl.BlockSpec((1, tn), lambda j: (0, j))],
        out_specs=pl.BlockSpec((B, tn), lambda j: (0, j)),
        out_shape=jax.ShapeDtypeStruct((B, N), F32),
        compiler_params=_params(("arbitrary",)),
        name="ada",
    )(c, w_ada, b_ada.reshape(1, N))


def _rope(t, cos2, sin2):
    return t * cos2 + pltpu.roll(t, ATTN_HEAD_DIM // 2, 1) * sin2


def _inproj_kernel(x_ref, mod_ref, g_ref, pos_ref, inv_ref, sgn_ref, w_ref, wt_ref,
                   o_ref, t_ref, gt_ref, h_scr, cos_scr, sin_scr, cost_scr, sint_scr,
                   *, D, n_main, rope_j0, n_t, ak_rows, gate_rows):
    j = pl.program_id(1)
    hd = ATTN_HEAD_DIM

    @pl.when(j == 0)
    def _():
        x = x_ref[...]
        y = x * lax.rsqrt(jnp.mean(x * x, axis=-1, keepdims=True) + EPS) * g_ref[...]
        sh = mod_ref[0, :, 0:D]
        sc = mod_ref[0, :, D:2 * D]
        h_scr[...] = (y * (1.0 + sc) + sh).astype(BF16)
        ang = pos_ref[...].astype(F32) * inv_ref[...]
        cos2 = jnp.cos(ang)
        sin2 = jnp.sin(ang) * sgn_ref[...]
        cos_scr[...] = cos2
        sin_scr[...] = sin2
        cost_scr[...] = cos2.T
        sint_scr[...] = sin2.T

    @pl.when(j < rope_j0)
    def _():
        o_ref[...] = jnp.dot(h_scr[...], w_ref[...], preferred_element_type=F32).astype(BF16)

    @pl.when(jnp.logical_and(j >= rope_j0, j < n_main))
    def _():
        acc = jnp.dot(h_scr[...], w_ref[...], preferred_element_type=F32)
        cos2 = cos_scr[...]
        sin2 = sin_scr[...]
        parts = [_rope(acc[:, hh * hd:(hh + 1) * hd], cos2, sin2) for hh in range(acc.shape[1] // hd)]
        o_ref[...] = jnp.concatenate(parts, axis=1).astype(BF16)

    tt = t_ref.shape[1]
    for ti in range(n_t):
        @pl.when(j == n_main + ti)
        def _(ti=ti):
            acc = lax.dot_general(wt_ref[...], h_scr[...], _NT, preferred_element_type=F32)
            lo, hi = ti * tt, (ti + 1) * tt
            pieces = []
            cur = lo
            for r0 in range(ak_rows[0], ak_rows[1], hd):
                if r0 < lo or r0 + hd > hi:
                    assert r0 + hd <= lo or r0 >= hi, "a k head straddles two transposed tiles"
                    continue
                if r0 > cur:
                    pieces.append(acc[cur - lo:r0 - lo, :])
                t = acc[r0 - lo:r0 - lo + hd, :]
                rot = jnp.concatenate([t[hd // 2:, :], t[:hd // 2, :]], axis=0)
                pieces.append(t * cost_scr[...] + rot * sint_scr[...])
                cur = r0 + hd
            if cur < hi:
                pieces.append(acc[cur - lo:, :])
            t_ref[0] = jnp.concatenate(pieces, axis=0).astype(BF16) if len(pieces) > 1 else acc.astype(BF16)
            if lo <= gate_rows[0] and gate_rows[1] <= hi:
                gt_ref[...] = acc[gate_rows[0] - lo:gate_rows[1] - lo, :]


def _inproj(x2, mod3, g1, pos2, w_main, w_t, B, S, NG, tm=1024, tn=512, n_t=2):
    N, D = x2.shape
    P = w_main.shape[1]
    TR = w_t.shape[0]
    hd = ATTN_HEAD_DIM
    half = hd // 2
    inv = ROPE_THETA ** (-jnp.arange(half, dtype=F32) * 2.0 / hd)
    inv2 = jnp.concatenate([inv, inv]).reshape(1, hd)
    sgn = jnp.concatenate([-jnp.ones((half,), F32), jnp.ones((half,), F32)]).reshape(1, hd)
    MW = D // 2
    KW = ATTN_KV_HEADS * hd
    aq0 = 3 * MW
    ak_rows = (MW + KW, MW + 2 * KW)
    gate_rows = (MW + 2 * KW, MW + 2 * KW + NG)
    tt = TR // n_t
    assert aq0 % tn == 0 and P == aq0 + MW and P % tn == 0 and TR % n_t == 0 and tt % BF16_ROWS == 0
    assert gate_rows[1] <= TR
    n_main = P // tn
    kern = functools.partial(_inproj_kernel, D=D, n_main=n_main, rope_j0=aq0 // tn, n_t=n_t,
                             ak_rows=ak_rows, gate_rows=gate_rows)
    per_b = S // tm
    main_j = lambda j: jnp.minimum(j, n_main - 1)
    t_j = lambda j: jnp.maximum(j - n_main, 0)
    return pl.pallas_call(
        kern,
        grid=(N // tm, n_main + n_t),
        in_specs=[pl.BlockSpec((tm, D), lambda i, j: (i, 0)),
                  pl.BlockSpec((1, 1, mod3.shape[2]), lambda i, j: (i // per_b, 0, 0)),
                  pl.BlockSpec((1, D), lambda i, j: (0, 0)),
                  pl.BlockSpec((tm, 1), lambda i, j: (i, 0)),
                  pl.BlockSpec((1, hd), lambda i, j: (0, 0)),
                  pl.BlockSpec((1, hd), lambda i, j: (0, 0)),
                  pl.BlockSpec((D, tn), lambda i, j: (0, main_j(j))),
                  pl.BlockSpec((tt, D), lambda i, j: (t_j(j), 0))],
        out_specs=[pl.BlockSpec((tm, tn), lambda i, j: (i, main_j(j))),
                   pl.BlockSpec((1, tt, tm), lambda i, j: (i // per_b, t_j(j), i % per_b)),
                   pl.BlockSpec((NG, tm), lambda i, j: (0, i))],
        out_shape=[jax.ShapeDtypeStruct((N, P), BF16),
                   jax.ShapeDtypeStruct((B, TR, S), BF16),
                   jax.ShapeDtypeStruct((NG, N), F32)],
        scratch_shapes=[pltpu.VMEM((tm, D), BF16),
                        pltpu.VMEM((tm, hd), F32),
                        pltpu.VMEM((tm, hd), F32),
                        pltpu.VMEM((hd, tm), F32),
                        pltpu.VMEM((hd, tm), F32)],
        compiler_params=_params(("arbitrary", "arbitrary")),
        name="inproj",
    )(x2, mod3, g1, pos2, inv2, sgn, w_main, w_t)


_R, _MT, _A, _LIM, _E, _SC = range(6)
_NVEC = 6


def _log_sigmoid(x):
    return jnp.minimum(x, 0.0) - jnp.log1p(jnp.exp(-jnp.abs(x)))


def _chunk_scan(x, reverse, op, ident):
    L = x.shape[1]
    lane = lax.broadcasted_iota(jnp.int32, x.shape, 1)
    sh = 1
    while sh < L:
        if reverse:
            x = op(x, jnp.where(lane < L - sh, pltpu.roll(x, L - sh, 1), ident))
        else:
            x = op(x, jnp.where(lane >= sh, pltpu.roll(x, sh, 1), ident))
        sh *= 2
    return x


def _gate_vectors(li, lf_pre, reverse):
    nc, L = li.shape
    lf = _log_sigmoid(lf_pre)
    b = _chunk_scan(lf, reverse, jnp.add, 0.0)
    r = li - b
    rcm = _chunk_scan(r, reverse, jnp.maximum, NEG_INF)
    end = slice(0, 1) if reverse else slice(L - 1, L)
    bl = jnp.broadcast_to(b[:, end], (nc, L))
    rmax = jnp.broadcast_to(rcm[:, end], (nc, L))
    row = lax.broadcasted_iota(jnp.int32, (nc, L), 0)
    m = jnp.zeros((1, L), F32)
    m_all = jnp.zeros((nc, L), F32)
    order = range(nc - 1, -1, -1) if reverse else range(nc)
    for j in order:
        m_all = jnp.where(row == j, m, m_all)
        m = bl[j:j + 1, :] + jnp.maximum(m, rmax[j:j + 1, :])
    mt = jnp.maximum(m_all, rcm)
    m_end = jnp.maximum(m_all, rmax)
    return r, mt, jnp.exp(m_all - mt), jnp.exp(-(b + mt)), jnp.exp(r - m_end), jnp.exp(m_all - m_end)


def _mlstm_kernel(q_ref, k_ref, vt_ref, mo_ref, gr_ref, gb_ref, wq_ref, wk_ref, bq_ref, bk_ref, go_ref,
                  o_ref, pad_scr, qs_scr, big_scr, vt_scr, vec_scr, ns_scr, ct_scr, n_scr,
                  *, S, d, L, CW):
    nc = S // L
    half_w = CW // 2
    PADR = 8
    K0 = 0
    N0 = L
    C0 = L + BF16_ROWS

    g = gr_ref[0, 0] + gb_ref[0]
    for dr in range(2):
        vecs = _gate_vectors(g[2 * dr], g[2 * dr + 1], reverse=(dr == 1))
        for idx, val in enumerate(vecs):
            vec_scr[dr * _NVEC + idx] = val

    zero_rows = jnp.zeros((PADR, d), F32)

    def conv_silu(src_ref, w_ref, b_ref, store, scale):
        pad_scr[0:PADR, :] = zero_rows
        pad_scr[PADR + S:2 * PADR + S, :] = zero_rows

        def fill(j, c):
            r0 = pl.multiple_of(j * L, L)
            pad_scr[pl.ds(PADR + r0, L), :] = src_ref[0, pl.ds(r0, L), :].astype(F32)
            return c
        lax.fori_loop(0, nc, fill, 0)
        w = w_ref[...]
        bias = b_ref[...]

        def body(j, c):
            r0 = pl.multiple_of(j * L, L)
            win = pad_scr[pl.ds(r0, L + 2 * PADR), :]
            y = bias
            for i in range(CW):
                o = PADR - half_w + i
                y = y + win[o:o + L, :] * w[i:i + 1, :]
            y = y * _sigmoid(y)
            if scale != 1.0:
                y = y * scale
            store(j, r0, y.astype(BF16))
            return c
        lax.fori_loop(0, nc, body, 0)

    def store_q(j, r0, y):
        qs_scr[pl.ds(r0, L), :] = y

    def store_k(j, r0, y):
        big_scr[j, K0:K0 + L, :] = y

    conv_silu(q_ref, wq_ref, bq_ref, store_q, 1.0)
    conv_silu(k_ref, wk_ref, bk_ref, store_k, d ** -0.5)

    for j in range(nc):
        vt_scr[j] = vt_ref[0, :, j * L:(j + 1) * L]

    ns_scr[...] = jnp.zeros(ns_scr.shape, F32)
    ct_scr[...] = jnp.zeros(ct_scr.shape, F32)
    n_scr[...] = jnp.zeros(n_scr.shape, F32)

    def state_step(dr, jc):
        base = dr * _NVEC
        e_row = vec_scr[base + _E, pl.ds(jc, 1), :]
        sc = vec_scr[base + _SC, pl.ds(jc, 1), :]
        sc2 = jnp.concatenate([sc] * (d // L), axis=1)
        lhs = jnp.concatenate([vt_scr[jc].astype(F32) * e_row, jnp.broadcast_to(e_row, (BF16_ROWS, L))], axis=0)
        upd = jnp.dot(lhs.astype(BF16), big_scr[jc, K0:K0 + L, :], preferred_element_type=F32)
        ct = ct_scr[dr]
        n = n_scr[dr]
        big_scr[jc, C0 + dr * d:C0 + (dr + 1) * d, :] = ct.astype(BF16)
        ns_scr[jc, dr:dr + 1, :] = n[0:1, :]
        ct_scr[dr] = sc2 * ct + upd[0:d, :]
        n_scr[dr] = sc2 * n + upd[d:d + 8, :]

    def phase1(j, c):
        state_step(0, j)
        state_step(1, nc - 1 - j)
        return c
    lax.fori_loop(0, nc, phase1, 0, unroll=2)

    def put_n(j, c):
        big_scr[j, N0:N0 + BF16_ROWS, :] = ns_scr[j].astype(BF16)
        return c
    lax.fori_loop(0, nc, put_n, 0)

    rr = lax.broadcasted_iota(jnp.int32, (L, L), 0)
    cc = lax.broadcasted_iota(jnp.int32, (L, L), 1)
    eye = rr == cc
    valid = (rr <= cc, rr >= cc)
    gout = go_ref[...]

    def phase2(j, c):
        r0 = pl.multiple_of(j * L, L)
        qj = qs_scr[pl.ds(r0, L), :]
        p = lax.dot_general(big_scr[j], qj, _NT, preferred_element_type=F32)
        st = p[K0:K0 + L, :]
        qkw = []
        inv_den = []
        a_rows = []
        for dr in range(2):
            base = dr * _NVEC
            r_row = vec_scr[base + _R, pl.ds(j, 1), :]
            mt_row = vec_scr[base + _MT, pl.ds(j, 1), :]
            a_row = vec_scr[base + _A, pl.ds(j, 1), :]
            lim_row = vec_scr[base + _LIM, pl.ds(j, 1), :]
            r_col = jnp.sum(jnp.where(eye, r_row, 0.0), axis=1, keepdims=True)
            w = jnp.exp(jnp.where(valid[dr], r_col - mt_row, NEG_INF))
            qkw_d = st * w
            den = a_row * p[N0 + dr:N0 + dr + 1, :] + jnp.sum(qkw_d, axis=0, keepdims=True)
            inv_den.append(1.0 / jnp.maximum(jnp.abs(den), lim_row))
            a_rows.append(a_row)
            qkw.append(qkw_d.astype(BF16))
        intra = jnp.dot(vt_scr[j], jnp.concatenate(qkw, axis=1), preferred_element_type=F32)
        hs = None
        for dr in range(2):
            inter = p[C0 + dr * d:C0 + (dr + 1) * d, :]
            hd = (a_rows[dr] * inter + intra[:, dr * L:(dr + 1) * L]) * inv_den[dr]
            hs = hd if hs is None else hs + hd
        mu = jnp.mean(hs, axis=0, keepdims=True)
        hc = hs - mu
        hn = (hc * lax.rsqrt(jnp.mean(hc * hc, axis=0, keepdims=True) + EPS)).T
        mo = mo_ref[0, pl.ds(r0, L), :].astype(F32)
        o_ref[0, pl.ds(r0, L), :] = (_sigmoid(mo) * hn * gout).astype(BF16)
        return c
    lax.fori_loop(0, nc, phase2, 0, unroll=2)


def _mlstm(proj3, vt3, gates5, gbias, w_conv, b_conv, g_out):
    B, S, _ = proj3.shape
    H = MLSTM_HEADS
    d = g_out.shape[1] // H
    L = MLSTM_CHUNK
    nc = S // L
    CW = w_conv.shape[0]
    assert L == LANES and d == 2 * L and CW // 2 <= 8
    kern = functools.partial(_mlstm_kernel, S=S, d=d, L=L, CW=CW)
    col = lambda off: (lambda b, h: (b, 0, off + h))
    return pl.pallas_call(
        kern,
        grid=(B, H),
        in_specs=[pl.BlockSpec((1, S, d), col(0)),
                  pl.BlockSpec((1, S, d), col(H)),
                  pl.BlockSpec((1, d, S), lambda b, h: (b, h, 0)),
                  pl.BlockSpec((1, S, d), col(2 * H)),
                  pl.BlockSpec((1, 1, 4, nc, L), lambda b, h: (b, h, 0, 0, 0)),
                  pl.BlockSpec((1, 4, 1, 1), lambda b, h: (h, 0, 0, 0)),
                  pl.BlockSpec((CW, d), lambda b, h: (0, h)),
                  pl.BlockSpec((CW, d), lambda b, h: (0, H + h)),
                  pl.BlockSpec((1, d), lambda b, h: (0, h)),
                  pl.BlockSpec((1, d), lambda b, h: (0, H + h)),
                  pl.BlockSpec((1, d), lambda b, h: (0, h))],
        out_specs=pl.BlockSpec((1, S, d), lambda b, h: (b, 0, h)),
        out_shape=jax.ShapeDtypeStruct((B, S, H * d), BF16),
        scratch_shapes=[pltpu.VMEM((S + 16, d), F32),
                        pltpu.VMEM((S, d), BF16),
                        pltpu.VMEM((nc, L + BF16_ROWS + 2 * d, d), BF16),
                        pltpu.VMEM((nc, d, L), BF16),
                        pltpu.VMEM((2 * _NVEC, nc, L), F32),
                        pltpu.VMEM((nc, BF16_ROWS, d), F32),
                        pltpu.VMEM((2, d, d), F32),
                        pltpu.VMEM((2, 8, d), F32)],
        compiler_params=_params(("arbitrary", "arbitrary")),
        name="mlstm",
    )(proj3, proj3, vt3, proj3, gates5, gbias, w_conv, w_conv, b_conv, b_conv, g_out)


LOG2E = 1.4426950408889634


def _attn_kernel(sink_ref, q_ref, ktp_ref, ktc_ref, ktn_ref, vtp_ref, vtc_ref, vtn_ref, g_ref, o_ref, *, H, G):
    n = pl.program_id(1)
    nb = pl.num_programs(1)
    blk = ATTN_BLOCK
    hd = ATTN_HEAD_DIM
    R = H // G
    c1 = (hd ** -0.5) * LOG2E
    rr = lax.broadcasted_iota(jnp.int32, (blk, blk), 0)
    cc = lax.broadcasted_iota(jnp.int32, (blk, blk), 1)
    m_prev = jnp.logical_and(cc >= rr, n > 0)
    m_next = jnp.logical_and(cc <= rr, n < nb - 1)
    q = q_ref[0]
    ones = jnp.ones((hd, 3 * blk), BF16)
    outs = []
    for g in range(G):
        rs = slice(g * hd, (g + 1) * hd)
        kt = jnp.concatenate([ktp_ref[0][rs, :], ktc_ref[0][rs, :], ktn_ref[0][rs, :]], axis=1)
        vt = jnp.concatenate([vtp_ref[0][rs, :], vtc_ref[0][rs, :], vtn_ref[0][rs, :]], axis=1)
        vt_aug = jnp.concatenate([vt, ones], axis=0)
        qg = jnp.concatenate([q[:, (g * R + r) * hd:(g * R + r + 1) * hd] for r in range(R)], axis=0)
        s = jnp.dot(qg, kt, preferred_element_type=F32)
        ps = []
        extras = []
        for r in range(R):
            sr = s[r * blk:(r + 1) * blk, :] * c1
            t0 = jnp.where(m_prev, sr[:, 0:blk], NEG_INF)
            t1 = sr[:, blk:2 * blk]
            t2 = jnp.where(m_next, sr[:, 2 * blk:3 * blk], NEG_INF)
            sink2 = sink_ref[g * R + r] * LOG2E
            m = jnp.maximum(jnp.max(jnp.maximum(jnp.maximum(t0, t1), t2), axis=1, keepdims=True), sink2)
            ps.append(jnp.concatenate([jnp.exp2(t0 - m), jnp.exp2(t1 - m), jnp.exp2(t2 - m)], axis=1).astype(BF16))
            extras.append(jnp.exp2(sink2 - m))
        oa = lax.dot_general(jnp.concatenate(ps, axis=0), vt_aug, _NT, preferred_element_type=F32)
        for r in range(R):
            o_r = oa[r * blk:(r + 1) * blk, :]
            outs.append(o_r[:, 0:hd] / (o_r[:, hd:2 * hd] + extras[r]))
    oall = jnp.concatenate(outs, axis=1)
    y = oall * lax.rsqrt(jnp.mean(oall * oall, axis=-1, keepdims=True) + EPS) * g_ref[...]
    o_ref[0] = y.astype(BF16)


def _attn(proj3, t3, sink, g_attn, q_col0, vt_row0, kt_row0):
    B, S, _ = proj3.shape
    blk = ATTN_BLOCK
    nb = S // blk
    H = sink.shape[0]
    G = ATTN_KV_HEADS
    AW = H * ATTN_HEAD_DIM
    KW = G * ATTN_HEAD_DIM
    assert WINDOW == blk and q_col0 % AW == 0 and vt_row0 % KW == 0 and kt_row0 % KW == 0
    qb, vb, kb = q_col0 // AW, vt_row0 // KW, kt_row0 // KW
    kern = functools.partial(_attn_kernel, H=H, G=G)
    prev = lambda rb: (lambda b, n: (b, rb, jnp.maximum(n - 1, 0)))
    cur = lambda rb: (lambda b, n: (b, rb, n))
    nxt = lambda rb: (lambda b, n: (b, rb, jnp.minimum(n + 1, nb - 1)))
    return pl.pallas_call(
        kern,
        grid=(B, nb),
        in_specs=[pl.BlockSpec(memory_space=pltpu.SMEM),
                  pl.BlockSpec((1, blk, AW), lambda b, n: (b, n, qb)),
                  pl.BlockSpec((1, KW, blk), prev(kb)),
                  pl.BlockSpec((1, KW, blk), cur(kb)),
                  pl.BlockSpec((1, KW, blk), nxt(kb)),
                  pl.BlockSpec((1, KW, blk), prev(vb)),
                  pl.BlockSpec((1, KW, blk), cur(vb)),
                  pl.BlockSpec((1, KW, blk), nxt(vb)),
                  pl.BlockSpec((1, AW), lambda b, n: (0, 0))],
        out_specs=pl.BlockSpec((1, blk, AW), lambda b, n: (b, n, 0)),
        out_shape=jax.ShapeDtypeStruct((B, S, AW), BF16),
        compiler_params=_params(("arbitrary", "arbitrary")),
        name="attn",
    )(sink, proj3, t3, t3, t3, t3, t3, t3, g_attn)


def _outproj_kernel(m_ref, a_ref, x_ref, mod_ref, wt_ref, wb_ref, o_ref, *, D):
    acc = jnp.dot(m_ref[...], wt_ref[...], preferred_element_type=F32)
    acc = acc + jnp.dot(a_ref[...], wb_ref[...], preferred_element_type=F32)
    gt = mod_ref[0, :, 2 * D:3 * D]
    o_ref[...] = x_ref[...] + gt * acc


def _outproj(m2, a2, x2, mod3, w_top, w_bot, S, tm=512):
    N, D = x2.shape
    K = m2.shape[1]
    per_b = S // tm
    return pl.pallas_call(
        functools.partial(_outproj_kernel, D=D),
        grid=(N // tm,),
        in_specs=[pl.BlockSpec((tm, K), lambda i: (i, 0)),
                  pl.BlockSpec((tm, K), lambda i: (i, 0)),
                  pl.BlockSpec((tm, D), lambda i: (i, 0)),
                  pl.BlockSpec((1, 1, mod3.shape[2]), lambda i: (i // per_b, 0, 0)),
                  pl.BlockSpec((K, D), lambda i: (0, 0)),
                  pl.BlockSpec((K, D), lambda i: (0, 0))],
        out_specs=pl.BlockSpec((tm, D), lambda i: (i, 0)),
        out_shape=jax.ShapeDtypeStruct((N, D), F32),
        compiler_params=_params(("arbitrary",)),
        name="outproj",
    )(m2, a2, x2, mod3, w_top, w_bot)


def _ffn_up_kernel(x_ref, mod_ref, g2_ref, wg_ref, wu_ref, o_ref, h_scr, *, D):
    @pl.when(pl.program_id(1) == 0)
    def _():
        x = x_ref[...]
        y = x * lax.rsqrt(jnp.mean(x * x, axis=-1, keepdims=True) + EPS) * g2_ref[...]
        sh = mod_ref[0, :, 3 * D:4 * D]
        sc = mod_ref[0, :, 4 * D:5 * D]
        h_scr[...] = (y * (1.0 + sc) + sh).astype(BF16)

    hb = h_scr[...]
    gate = jnp.dot(hb, wg_ref[...], preferred_element_type=F32)
    up = jnp.dot(hb, wu_ref[...], preferred_element_type=F32)
    o_ref[...] = (gate * _sigmoid(gate) * up).astype(BF16)


def _ffn_up(x2, mod3, g2, w_gate, w_up, S, tm=1024, tf=512):
    N, D = x2.shape
    FF = w_gate.shape[1]
    per_b = S // tm
    return pl.pallas_call(
        functools.partial(_ffn_up_kernel, D=D),
        grid=(N // tm, FF // tf),
        in_specs=[pl.BlockSpec((tm, D), lambda i, j: (i, 0)),
                  pl.BlockSpec((1, 1, mod3.shape[2]), lambda i, j: (i // per_b, 0, 0)),
                  pl.BlockSpec((1, D), lambda i, j: (0, 0)),
                  pl.BlockSpec((D, tf), lambda i, j: (0, j)),
                  pl.BlockSpec((D, tf), lambda i, j: (0, j))],
        out_specs=pl.BlockSpec((tm, tf), lambda i, j: (i, j)),
        out_shape=jax.ShapeDtypeStruct((N, FF), BF16),
        scratch_shapes=[pltpu.VMEM((tm, D), BF16)],
        compiler_params=_params(("arbitrary", "arbitrary")),
        name="ffn_up",
    )(x2, mod3, g2, w_gate, w_up)


def _ffn_down_kernel(u_ref, x_ref, mod_ref, gf_ref, wd_ref, o_ref, *, D):
    j = pl.program_id(1)

    @pl.when(j == 0)
    def _():
        o_ref[...] = jnp.zeros(o_ref.shape, F32)

    o_ref[...] += jnp.dot(u_ref[...], wd_ref[...], preferred_element_type=F32)

    @pl.when(j == pl.num_programs(1) - 1)
    def _():
        gt = mod_ref[0, :, 5 * D:6 * D]
        y = x_ref[...] + gt * o_ref[...]
        o_ref[...] = y * lax.rsqrt(jnp.mean(y * y, axis=-1, keepdims=True) + EPS) * gf_ref[...]


def _ffn_down(u2, x2, mod3, gf, w_down, S, tm=1024, tk=512):
    N, D = x2.shape
    FF = u2.shape[1]
    per_b = S // tm
    return pl.pallas_call(
        functools.partial(_ffn_down_kernel, D=D),
        grid=(N // tm, FF // tk),
        in_specs=[pl.BlockSpec((tm, tk), lambda i, j: (i, j)),
                  pl.BlockSpec((tm, D), lambda i, j: (i, 0)),
                  pl.BlockSpec((1, 1, mod3.shape[2]), lambda i, j: (i // per_b, 0, 0)),
                  pl.BlockSpec((1, D), lambda i, j: (0, 0)),
                  pl.BlockSpec((tk, D), lambda i, j: (j, 0))],
        out_specs=pl.BlockSpec((tm, D), lambda i, j: (i, 0)),
        out_shape=jax.ShapeDtypeStruct((N, D), F32),
        compiler_params=_params(("arbitrary", "arbitrary")),
        name="ffn_down",
    )(u2, x2, mod3, gf, w_down)


def kernel(x, c, positions, w_ada, b_ada, g_norm1, g_norm2, w_in, b_gates, w_conv, b_conv, g_mlstm_out,
           sink, g_attn_out, w_out, w_gate, w_up, w_down, g_final):
    B, S, D = x.shape
    depth = w_ada.shape[0]
    H = MLSTM_HEADS
    MW = D // 2
    KW = ATTN_KV_HEADS * ATTN_HEAD_DIM
    NG = 4 * H
    L = MLSTM_CHUNK
    nc = S // L
    g0 = 4 * MW
    x2 = x.reshape(B * S, D)
    pos2 = positions.reshape(B * S, 1)
    for l in range(depth):
        mod3 = _ada(c, w_ada[l], b_ada[l]).reshape(B, 1, 6 * D)
        wl = w_in[l]
        a0 = g0 + NG
        w_main = jnp.concatenate([wl[:, :2 * MW], wl[:, 3 * MW:g0], wl[:, a0:a0 + MW]], axis=1).astype(BF16)
        t_rows = MW + 2 * KW + NG
        t_pad = -t_rows % (2 * BF16_ROWS)
        w_t = jnp.concatenate([wl[:, 2 * MW:3 * MW], wl[:, a0 + MW + KW:], wl[:, a0 + MW:a0 + MW + KW],
                               wl[:, g0:a0], jnp.zeros((D, t_pad), wl.dtype)], axis=1).T.astype(BF16)
        proj, t3, gates_t = _inproj(x2, mod3, g_norm1[l].reshape(1, D), pos2, w_main, w_t, B, S, NG)
        proj3 = proj.reshape(B, S, proj.shape[1])
        gates5 = gates_t.reshape(4, H, B, nc, L).transpose(2, 1, 0, 3, 4)
        gbias = b_gates[l].reshape(4, H).T.reshape(H, 4, 1, 1)
        m_out = _mlstm(proj3, t3, gates5, gbias, w_conv[l], b_conv[l].reshape(1, 2 * MW),
                       g_mlstm_out[l].reshape(1, MW))
        a_out = _attn(proj3, t3, sink[l], g_attn_out[l].reshape(1, MW), q_col0=3 * MW,
                      vt_row0=MW, kt_row0=MW + KW)
        w_o = w_out[l].astype(BF16)
        x2 = _outproj(m_out.reshape(B * S, MW), a_out.reshape(B * S, MW), x2, mod3, w_o[:MW], w_o[MW:], S)
        assert l == depth - 1, "final norm is fused into the last layer's FFN"
        u2 = _ffn_up(x2, mod3, g_norm2[l].reshape(1, D), w_gate[l].astype(BF16), w_up[l].astype(BF16), S)
        x2 = _ffn_down(u2, x2, mod3, g_final.reshape(1, D), w_down[l].astype(BF16), S)
    return x2.reshape(B, S, D)
```

```python
import functools

import jax
import jax.numpy as jnp
from jax import lax
from jax.experimental import pallas as pl
from jax.experimental.pallas import tpu as pltpu

F32 = jnp.float32
BF16 = jnp.bfloat16

EPS = 1e-6
NEG_INF = -1e30
ROPE_THETA = 10000.0

MLSTM_HEADS = 4
MLSTM_CHUNK = 256
ATTN_HEAD_DIM = 128
ATTN_KV_HEADS = 2
WINDOW = 128
ATTN_BLOCK = 128

LANES = 128
BF16_ROWS = 16
VMEM_LIMIT = 56 * 1024 * 1024

_NT = (((1,), (1,)), ((), ()))


def _sigmoid(x):
    return 1.0 / (1.0 + jnp.exp(-x))


def _params(sem, vmem=VMEM_LIMIT):
    return pltpu.CompilerParams(dimension_semantics=sem, vmem_limit_bytes=vmem)


def _ada_kernel(c_ref, w_ref, b_ref, o_ref):
    c = c_ref[...]
    s = c * _sigmoid(c)
    o_ref[...] = jnp.dot(s, w_ref[...], preferred_element_type=F32,
                         precision=lax.Precision.HIGHEST) + b_ref[...]


def _ada(c, w_ada, b_ada, tn=1024):
    B, D = c.shape
    N = w_ada.shape[1]
    return pl.pallas_call(
        _ada_kernel,
        grid=(N // tn,),
        in_specs=[pl.BlockSpec((B, D), lambda j: (0, 0)),
                  pl.BlockSpec((D, tn), lambda j: (0, j)),
                  pl.BlockSpec((1, tn), lambda j: (0, j))],
        out_specs=pl.BlockSpec((B, tn), lambda j: (0, j)),
        out_shape=jax.ShapeDtypeStruct((B, N), F32),
        compiler_params=_params(("arbitrary",)),
        name="ada",
    )(c, w_ada, b_ada.reshape(1, N))


def _rope(t, cos2, sin2):
    return t * cos2 + pltpu.roll(t, ATTN_HEAD_DIM // 2, 1) * sin2


def _inproj_kernel(x_ref, mod_ref, g_ref, pos_ref, inv_ref, sgn_ref, w_ref, wt_ref,
                   o_ref, t_ref, gt_ref, h_scr, cos_scr, sin_scr, cost_scr, sint_scr,
                   *, D, n_main, rope_j0, n_t, ak_rows, gate_rows):
    j = pl.program_id(1)
    hd = ATTN_HEAD_DIM

    @pl.when(j == 0)
    def _():
        x = x_ref[...]
        y = x * lax.rsqrt(jnp.mean(x * x, axis=-1, keepdims=True) + EPS) * g_ref[...]
        sh = mod_ref[0, :, 0:D]
        sc = mod_ref[0, :, D:2 * D]
        h_scr[...] = (y * (1.0 + sc) + sh).astype(BF16)
        ang = pos_ref[...].astype(F32) * inv_ref[...]
        cos2 = jnp.cos(ang)
        sin2 = jnp.sin(ang) * sgn_ref[...]
        cos_scr[...] = cos2
        sin_scr[...] = sin2
        cost_scr[...] = cos2.T
        sint_scr[...] = sin2.T

    @pl.when(j < rope_j0)
    def _():
        o_ref[...] = jnp.dot(h_scr[...], w_ref[...], preferred_element_type=F32).astype(BF16)

    @pl.when(jnp.logical_and(j >= rope_j0, j < n_main))
    def _():
        acc = jnp.dot(h_scr[...], w_ref[...], preferred_element_type=F32)
        cos2 = cos_scr[...]
        sin2 = sin_scr[...]
        parts = [_rope(acc[:, hh * hd:(hh + 1) * hd], cos2, sin2) for hh in range(acc.shape[1] // hd)]
        o_ref[...] = jnp.concatenate(parts, axis=1).astype(BF16)

    tt = t_ref.shape[1]
    for ti in range(n_t):
        @pl.when(j == n_main + ti)
        def _(ti=ti):
            acc = lax.dot_general(wt_ref[...], h_scr[...], _NT, preferred_element_type=F32)
            lo, hi = ti * tt, (ti + 1) * tt
            pieces = []
            cur = lo
            for r0 in range(ak_rows[0], ak_rows[1], hd):
                if r0 < lo or r0 + hd > hi:
                    assert r0 + hd <= lo or r0 >= hi, "a k head straddles two transposed tiles"
                    continue
                if r0 > cur:
                    pieces.append(acc[cur - lo:r0 - lo, :])
                t = acc[r0 - lo:r0 - lo + hd, :]
                rot = jnp.concatenate([t[hd // 2:, :], t[:hd // 2, :]], axis=0)
                pieces.append(t * cost_scr[...] + rot * sint_scr[...])
                cur = r0 + hd
            if cur < hi:
                pieces.append(acc[cur - lo:, :])
            t_ref[0] = jnp.concatenate(pieces, axis=0).astype(BF16) if len(pieces) > 1 else acc.astype(BF16)
            if lo <= gate_rows[0] and gate_rows[1] <= hi:
                gt_ref[...] = acc[gate_rows[0] - lo:gate_rows[1] - lo, :]


def _inproj(x2, mod3, g1, pos2, w_main, w_t, B, S, NG, tm=1024, tn=512, n_t=2):
    N, D = x2.shape
    P = w_main.shape[1]
    TR = w_t.shape[0]
    hd = ATTN_HEAD_DIM
    half = hd // 2
    inv = ROPE_THETA ** (-jnp.arange(half, dtype=F32) * 2.0 / hd)
    inv2 = jnp.concatenate([inv, inv]).reshape(1, hd)
    sgn = jnp.concatenate([-jnp.ones((half,), F32), jnp.ones((half,), F32)]).reshape(1, hd)
    MW = D // 2
    KW = ATTN_KV_HEADS * hd
    aq0 = 3 * MW
    ak_rows = (MW + KW, MW + 2 * KW)
    gate_rows = (MW + 2 * KW, MW + 2 * KW + NG)
    tt = TR // n_t
    assert aq0 % tn == 0 and P == aq0 + MW and P % tn == 0 and TR % n_t == 0 and tt % BF16_ROWS == 0
    assert gate_rows[1] <= TR
    n_main = P // tn
    kern = functools.partial(_inproj_kernel, D=D, n_main=n_main, rope_j0=aq0 // tn, n_t=n_t,
                             ak_rows=ak_rows, gate_rows=gate_rows)
    per_b = S // tm
    main_j = lambda j: jnp.minimum(j, n_main - 1)
    t_j = lambda j: jnp.maximum(j - n_main, 0)
    return pl.pallas_call(
        kern,
        grid=(N // tm, n_main + n_t),
        in_specs=[pl.BlockSpec((tm, D), lambda i, j: (i, 0)),
                  pl.BlockSpec((1, 1, mod3.shape[2]), lambda i, j: (i // per_b, 0, 0)),
                  pl.BlockSpec((1, D), lambda i, j: (0, 0)),
                  pl.BlockSpec((tm, 1), lambda i, j: (i, 0)),
                  pl.BlockSpec((1, hd), lambda i, j: (0, 0)),
                  pl.BlockSpec((1, hd), lambda i, j: (0, 0)),
                  pl.BlockSpec((D, tn), lambda i, j: (0, main_j(j))),
                  pl.BlockSpec((tt, D), lambda i, j: (t_j(j), 0))],
        out_specs=[pl.BlockSpec((tm, tn), lambda i, j: (i, main_j(j))),
                   pl.BlockSpec((1, tt, tm), lambda i, j: (i // per_b, t_j(j), i % per_b)),
                   pl.BlockSpec((NG, tm), lambda i, j: (0, i))],
        out_shape=[jax.ShapeDtypeStruct((N, P), BF16),
                   jax.ShapeDtypeStruct((B, TR, S), BF16),
                   jax.ShapeDtypeStruct((NG, N), F32)],
        scratch_shapes=[pltpu.VMEM((tm, D), BF16),
                        pltpu.VMEM((tm, hd), F32),
                        pltpu.VMEM((tm, hd), F32),
                        pltpu.VMEM((hd, tm), F32),
                        pltpu.VMEM((hd, tm), F32)],
        compiler_params=_params(("arbitrary", "arbitrary")),
        name="inproj",
    )(x2, mod3, g1, pos2, inv2, sgn, w_main, w_t)


_R, _MT, _A, _LIM, _E, _SC = range(6)
_NVEC = 6


def _log_sigmoid(x):
    return jnp.minimum(x, 0.0) - jnp.log1p(jnp.exp(-jnp.abs(x)))


def _lane_scan(x, reverse, op, ident):
    L = x.shape[1]
    lane = lax.broadcasted_iota(jnp.int32, x.shape, 1)
    sh = 1
    while sh < L:
        if reverse:
            x = op(x, jnp.where(lane < L - sh, pltpu.roll(x, L - sh, 1), ident))
        else:
            x = op(x, jnp.where(lane >= sh, pltpu.roll(x, sh, 1), ident))
        sh *= 2
    return x


def _chunk_scan(x, reverse, op, ident):
    n = x.shape[1] // LANES
    parts = [_lane_scan(x[:, i * LANES:(i + 1) * LANES], reverse, op, ident) for i in range(n)]
    order = range(n - 2, -1, -1) if reverse else range(1, n)
    for i in order:
        prev = parts[i + 1][:, 0:1] if reverse else parts[i - 1][:, LANES - 1:LANES]
        parts[i] = op(parts[i], prev)
    return jnp.concatenate(parts, axis=1)


def _gate_vectors(li, lf_pre, reverse, nc):
    rows, Lc = li.shape
    hb = rows // nc
    lf = _log_sigmoid(lf_pre)
    b = _chunk_scan(lf, reverse, jnp.add, 0.0)
    r = li - b
    rcm = _chunk_scan(r, reverse, jnp.maximum, NEG_INF)
    end = slice(0, 1) if reverse else slice(Lc - 1, Lc)
    bl = jnp.broadcast_to(b[:, end], (rows, Lc))
    rmax = jnp.broadcast_to(rcm[:, end], (rows, Lc))
    m = jnp.zeros((hb, Lc), F32)
    m_parts = [None] * nc
    for j in (range(nc - 1, -1, -1) if reverse else range(nc)):
        m_parts[j] = m
        sl = slice(j * hb, (j + 1) * hb)
        m = bl[sl, :] + jnp.maximum(m, rmax[sl, :])
    m_all = jnp.concatenate(m_parts, axis=0)
    mt = jnp.maximum(m_all, rcm)
    m_end = jnp.maximum(m_all, rmax)
    return r, mt, jnp.exp(m_all - mt), jnp.exp(-(b + mt)), jnp.exp(r - m_end), jnp.exp(m_all - m_end)


def _gatevec_kernel(g_ref, gb_ref, o_ref, *, nc):
    for dr in range(2):
        li = g_ref[2 * dr] + gb_ref[2 * dr]
        lf_pre = g_ref[2 * dr + 1] + gb_ref[2 * dr + 1]
        for idx, val in enumerate(_gate_vectors(li, lf_pre, dr == 1, nc)):
            o_ref[dr * _NVEC + idx] = val


def _gatevec(gates, gbias, nc):
    _, rows, Lc = gates.shape
    return pl.pallas_call(
        functools.partial(_gatevec_kernel, nc=nc),
        out_shape=jax.ShapeDtypeStruct((2 * _NVEC, rows, Lc), F32),
        compiler_params=pltpu.CompilerParams(vmem_limit_bytes=VMEM_LIMIT),
        name="gatevec",
    )(gates, gbias)


CONV_ROWS = 128


def _mlstm_kernel(q_ref, k_ref, vt_ref, mo_ref, vec_ref, wq_ref, wk_ref, bq_ref, bk_ref, go_ref,
                  o_ref, pad_scr, qs_scr, big_scr, vt_scr, ns_scr, ct_scr, n_scr, *, S, d, Lc, CW):
    nc = S // Lc
    half_w = CW // 2
    PADR = 8
    CR = CONV_ROWS
    N0 = Lc
    C0 = Lc + BF16_ROWS

    zero_rows = jnp.zeros((PADR, d), F32)

    def conv_silu(src_ref, w_ref, b_ref, store, scale):
        pad_scr[0:PADR, :] = zero_rows
        pad_scr[PADR + S:2 * PADR + S, :] = zero_rows

        def fill(j, c):
            r0 = pl.multiple_of(j * CR, CR)
            pad_scr[pl.ds(PADR + r0, CR), :] = src_ref[0, pl.ds(r0, CR), :].astype(F32)
            return c
        lax.fori_loop(0, S // CR, fill, 0)
        w = w_ref[...]
        bias = b_ref[...]

        def body(j, c):
            r0 = pl.multiple_of(j * CR, CR)
            win = pad_scr[pl.ds(r0, CR + 2 * PADR), :]
            y = bias
            for i in range(CW):
                o = PADR - half_w + i
                y = y + win[o:o + CR, :] * w[i:i + 1, :]
            y = y * _sigmoid(y)
            if scale != 1.0:
                y = y * scale
            store(j, r0, y.astype(BF16))
            return c
        lax.fori_loop(0, S // CR, body, 0)

    def store_q(j, r0, y):
        qs_scr[pl.ds(r0, CR), :] = y

    per = Lc // CR

    def store_k(j, r0, y):
        big_scr[j // per, pl.ds(pl.multiple_of((j % per) * CR, CR), CR), :] = y

    conv_silu(q_ref, wq_ref, bq_ref, store_q, 1.0)
    conv_silu(k_ref, wk_ref, bk_ref, store_k, d ** -0.5)

    for j in range(nc):
        vt_scr[j] = vt_ref[0, :, j * Lc:(j + 1) * Lc]

    ns_scr[...] = jnp.zeros(ns_scr.shape, F32)
    ct_scr[...] = jnp.zeros(ct_scr.shape, F32)
    n_scr[...] = jnp.zeros(n_scr.shape, F32)

    def state_step(dr, jc):
        base = dr * _NVEC
        e_row = vec_ref[0, 0, base + _E, pl.ds(jc, 1), :]
        sc = vec_ref[0, 0, base + _SC, pl.ds(jc, 1), :]
        lhs = jnp.concatenate([vt_scr[jc].astype(F32) * e_row, jnp.broadcast_to(e_row, (BF16_ROWS, Lc))], axis=0)
        upd = jnp.dot(lhs.astype(BF16), big_scr[jc, 0:Lc, :], preferred_element_type=F32)
        ct = ct_scr[dr]
        n = n_scr[dr]
        big_scr[jc, C0 + dr * d:C0 + (dr + 1) * d, :] = ct.astype(BF16)
        ns_scr[jc, dr:dr + 1, :] = n[0:1, :]
        ct_scr[dr] = sc * ct + upd[0:d, :]
        n_scr[dr] = sc * n + upd[d:d + 8, :]

    def phase1(j, c):
        state_step(0, j)
        state_step(1, nc - 1 - j)
        return c
    lax.fori_loop(0, nc, phase1, 0)

    def put_n(j, c):
        big_scr[j, N0:N0 + BF16_ROWS, :] = ns_scr[j].astype(BF16)
        return c
    lax.fori_loop(0, nc, put_n, 0)

    rr = lax.broadcasted_iota(jnp.int32, (Lc, Lc), 0)
    cc = lax.broadcasted_iota(jnp.int32, (Lc, Lc), 1)
    r1 = lax.broadcasted_iota(jnp.int32, (LANES, LANES), 0)
    c1 = lax.broadcasted_iota(jnp.int32, (LANES, LANES), 1)
    eye = r1 == c1
    gout = go_ref[...]

    def to_col(row):
        return jnp.concatenate(
            [jnp.sum(jnp.where(eye, row[:, i * LANES:(i + 1) * LANES], 0.0), axis=1, keepdims=True)
             for i in range(Lc // LANES)], axis=0)

    def phase2(j, c):
        r0 = pl.multiple_of(j * Lc, Lc)
        qj = qs_scr[pl.ds(r0, Lc), :]
        hrows = C0 + d
        p = jnp.concatenate(
            [lax.dot_general(big_scr[j, 0:hrows, :], qj, _NT, preferred_element_type=F32),
             lax.dot_general(big_scr[j, hrows:hrows + d, :], qj, _NT, preferred_element_type=F32)], axis=0)
        st = p[0:Lc, :]
        qkw = []
        inv_den = []
        a_rows = []
        for dr in range(2):
            base = dr * _NVEC
            r_row = vec_ref[0, 0, base + _R, pl.ds(j, 1), :]
            mt_row = vec_ref[0, 0, base + _MT, pl.ds(j, 1), :]
            a_row = vec_ref[0, 0, base + _A, pl.ds(j, 1), :]
            lim_row = vec_ref[0, 0, base + _LIM, pl.ds(j, 1), :]
            valid = (rr <= cc) if dr == 0 else (rr >= cc)
            w = jnp.exp(jnp.where(valid, to_col(r_row) - mt_row, NEG_INF))
            qkw_d = st * w
            den = a_row * p[N0 + dr:N0 + dr + 1, :] + jnp.sum(qkw_d, axis=0, keepdims=True)
            inv_den.append(1.0 / jnp.maximum(jnp.abs(den), lim_row))
            a_rows.append(a_row)
            qkw.append(qkw_d.astype(BF16))
        intra = jnp.dot(vt_scr[j], jnp.concatenate(qkw, axis=1), preferred_element_type=F32)
        hs = None
        for dr in range(2):
            inter = p[C0 + dr * d:C0 + (dr + 1) * d, :]
            hd = (a_rows[dr] * inter + intra[:, dr * Lc:(dr + 1) * Lc]) * inv_den[dr]
            hs = hd if hs is None else hs + hd
        mu = jnp.mean(hs, axis=0, keepdims=True)
        hc = hs - mu
        hn = (hc * lax.rsqrt(jnp.mean(hc * hc, axis=0, keepdims=True) + EPS)).T
        mo = mo_ref[0, pl.ds(r0, Lc), :].astype(F32)
        o_ref[0, pl.ds(r0, Lc), :] = (_sigmoid(mo) * hn * gout).astype(BF16)
        return c
    lax.fori_loop(0, nc, phase2, 0)


def _mlstm(proj3, vt3, vec5, w_conv, b_conv, g_out):
    B, S, _ = proj3.shape
    H = MLSTM_HEADS
    d = g_out.shape[1] // H
    Lc = MLSTM_CHUNK
    nc = S // Lc
    CW = w_conv.shape[0]
    assert Lc % LANES == 0 and Lc % CONV_ROWS == 0 and d == Lc and CW // 2 <= 8
    kern = functools.partial(_mlstm_kernel, S=S, d=d, Lc=Lc, CW=CW)
    col = lambda off: (lambda b, h: (b, 0, off + h))
    return pl.pallas_call(
        kern,
        grid=(B, H),
        in_specs=[pl.BlockSpec((1, S, d), col(0)),
                  pl.BlockSpec((1, S, d), col(H)),
                  pl.BlockSpec((1, d, S), lambda b, h: (b, h, 0)),
                  pl.BlockSpec((1, S, d), col(2 * H)),
                  pl.BlockSpec((1, 1, 2 * _NVEC, nc, Lc), lambda b, h: (b, h, 0, 0, 0)),
                  pl.BlockSpec((CW, d), lambda b, h: (0, h)),
                  pl.BlockSpec((CW, d), lambda b, h: (0, H + h)),
                  pl.BlockSpec((1, d), lambda b, h: (0, h)),
                  pl.BlockSpec((1, d), lambda b, h: (0, H + h)),
                  pl.BlockSpec((1, d), lambda b, h: (0, h))],
        out_specs=pl.BlockSpec((1, S, d), lambda b, h: (b, 0, h)),
        out_shape=jax.ShapeDtypeStruct((B, S, H * d), BF16),
        scratch_shapes=[pltpu.VMEM((S + 16, d), F32),
                        pltpu.VMEM((S, d), BF16),
                        pltpu.VMEM((nc, Lc + BF16_ROWS + 2 * d, d), BF16),
                        pltpu.VMEM((nc, d, Lc), BF16),
                        pltpu.VMEM((nc, BF16_ROWS, d), F32),
                        pltpu.VMEM((2, d, d), F32),
                        pltpu.VMEM((2, 8, d), F32)],
        compiler_params=_params(("arbitrary", "arbitrary")),
        name="mlstm",
    )(proj3, proj3, vt3, proj3, vec5, w_conv, w_conv, b_conv, b_conv, g_out)


LOG2E = 1.4426950408889634


def _attn_kernel(sink_ref, q_ref, ktp_ref, ktc_ref, ktn_ref, vtp_ref, vtc_ref, vtn_ref, g_ref, o_ref, *, H, G):
    n = pl.program_id(1)
    nb = pl.num_programs(1)
    blk = ATTN_BLOCK
    hd = ATTN_HEAD_DIM
    R = H // G
    c1 = (hd ** -0.5) * LOG2E
    rr = lax.broadcasted_iota(jnp.int32, (blk, blk), 0)
    cc = lax.broadcasted_iota(jnp.int32, (blk, blk), 1)
    m_prev = jnp.logical_and(cc >= rr, n > 0)
    m_next = jnp.logical_and(cc <= rr, n < nb - 1)
    q = q_ref[0]
    ones = jnp.ones((hd, 3 * blk), BF16)
    outs = []
    for g in range(G):
        rs = slice(g * hd, (g + 1) * hd)
        kt = jnp.concatenate([ktp_ref[0][rs, :], ktc_ref[0][rs, :], ktn_ref[0][rs, :]], axis=1)
        vt = jnp.concatenate([vtp_ref[0][rs, :], vtc_ref[0][rs, :], vtn_ref[0][rs, :]], axis=1)
        vt_aug = jnp.concatenate([vt, ones], axis=0)
        qg = jnp.concatenate([q[:, (g * R + r) * hd:(g * R + r + 1) * hd] for r in range(R)], axis=0)
        s = jnp.dot(qg, kt, preferred_element_type=F32)
        ps = []
        extras = []
        for r in range(R):
            sr = s[r * blk:(r + 1) * blk, :] * c1
            t0 = jnp.where(m_prev, sr[:, 0:blk], NEG_INF)
            t1 = sr[:, blk:2 * blk]
            t2 = jnp.where(m_next, sr[:, 2 * blk:3 * blk], NEG_INF)
            sink2 = sink_ref[g * R + r] * LOG2E
            m = jnp.maximum(jnp.max(jnp.maximum(jnp.maximum(t0, t1), t2), axis=1, keepdims=True), sink2)
            ps.append(jnp.concatenate([jnp.exp2(t0 - m), jnp.exp2(t1 - m), jnp.exp2(t2 - m)], axis=1).astype(BF16))
            extras.append(jnp.exp2(sink2 - m))
        oa = lax.dot_general(jnp.concatenate(ps, axis=0), vt_aug, _NT, preferred_element_type=F32)
        for r in range(R):
            o_r = oa[r * blk:(r + 1) * blk, :]
            outs.append(o_r[:, 0:hd] / (o_r[:, hd:2 * hd] + extras[r]))
    oall = jnp.concatenate(outs, axis=1)
    y = oall * lax.rsqrt(jnp.mean(oall * oall, axis=-1, keepdims=True) + EPS) * g_ref[...]
    o_ref[0] = y.astype(BF16)


def _attn(proj3, t3, sink, g_attn, q_col0, vt_row0, kt_row0):
    B, S, _ = proj3.shape
    blk = ATTN_BLOCK
    nb = S // blk
    H = sink.shape[0]
    G = ATTN_KV_HEADS
    AW = H * ATTN_HEAD_DIM
    KW = G * ATTN_HEAD_DIM
    assert WINDOW == blk and q_col0 % AW == 0 and vt_row0 % KW == 0 and kt_row0 % KW == 0
    qb, vb, kb = q_col0 // AW, vt_row0 // KW, kt_row0 // KW
    kern = functools.partial(_attn_kernel, H=H, G=G)
    prev = lambda rb: (lambda b, n: (b, rb, jnp.maximum(n - 1, 0)))
    cur = lambda rb: (lambda b, n: (b, rb, n))
    nxt = lambda rb: (lambda b, n: (b, rb, jnp.minimum(n + 1, nb - 1)))
    return pl.pallas_call(
        kern,
        grid=(B, nb),
        in_specs=[pl.BlockSpec(memory_space=pltpu.SMEM),
                  pl.BlockSpec((1, blk, AW), lambda b, n: (b, n, qb)),
                  pl.BlockSpec((1, KW, blk), prev(kb)),
                  pl.BlockSpec((1, KW, blk), cur(kb)),
                  pl.BlockSpec((1, KW, blk), nxt(kb)),
                  pl.BlockSpec((1, KW, blk), prev(vb)),
                  pl.BlockSpec((1, KW, blk), cur(vb)),
                  pl.BlockSpec((1, KW, blk), nxt(vb)),
                  pl.BlockSpec((1, AW), lambda b, n: (0, 0))],
        out_specs=pl.BlockSpec((1, blk, AW), lambda b, n: (b, n, 0)),
        out_shape=jax.ShapeDtypeStruct((B, S, AW), BF16),
        compiler_params=_params(("arbitrary", "arbitrary")),
        name="attn",
    )(sink, proj3, t3, t3, t3, t3, t3, t3, g_attn)


def _outproj_kernel(m_ref, a_ref, x_ref, mod_ref, wt_ref, wb_ref, o_ref, *, D):
    acc = jnp.dot(m_ref[...], wt_ref[...], preferred_element_type=F32)
    acc = acc + jnp.dot(a_ref[...], wb_ref[...], preferred_element_type=F32)
    gt = mod_ref[0, :, 2 * D:3 * D]
    o_ref[...] = x_ref[...] + gt * acc


def _outproj(m2, a2, x2, mod3, w_top, w_bot, S, tm=512):
    N, D = x2.shape
    K = m2.shape[1]
    per_b = S // tm
    return pl.pallas_call(
        functools.partial(_outproj_kernel, D=D),
        grid=(N // tm,),
        in_specs=[pl.BlockSpec((tm, K), lambda i: (i, 0)),
                  pl.BlockSpec((tm, K), lambda i: (i, 0)),
                  pl.BlockSpec((tm, D), lambda i: (i, 0)),
                  pl.BlockSpec((1, 1, mod3.shape[2]), lambda i: (i // per_b, 0, 0)),
                  pl.BlockSpec((K, D), lambda i: (0, 0)),
                  pl.BlockSpec((K, D), lambda i: (0, 0))],
        out_specs=pl.BlockSpec((tm, D), lambda i: (i, 0)),
        out_shape=jax.ShapeDtypeStruct((N, D), F32),
        compiler_params=_params(("arbitrary",)),
        name="outproj",
    )(m2, a2, x2, mod3, w_top, w_bot)


def _ffn_up_kernel(x_ref, mod_ref, g2_ref, wg_ref, wu_ref, o_ref, h_scr, *, D):
    @pl.when(pl.program_id(1) == 0)
    def _():
        x = x_ref[...]
        y = x * lax.rsqrt(jnp.mean(x * x, axis=-1, keepdims=True) + EPS) * g2_ref[...]
        sh = mod_ref[0, :, 3 * D:4 * D]
        sc = mod_ref[0, :, 4 * D:5 * D]
        h_scr[...] = (y * (1.0 + sc) + sh).astype(BF16)

    hb = h_scr[...]
    gate = jnp.dot(hb, wg_ref[...], preferred_element_type=F32)
    up = jnp.dot(hb, wu_ref[...], preferred_element_type=F32)
    o_ref[...] = (gate * _sigmoid(gate) * up).astype(BF16)


def _ffn_up(x2, mod3, g2, w_gate, w_up, S, tm=1024, tf=512):
    N, D = x2.shape
    FF = w_gate.shape[1]
    per_b = S // tm
    return pl.pallas_call(
        functools.partial(_ffn_up_kernel, D=D),
        grid=(N // tm, FF // tf),
        in_specs=[pl.BlockSpec((tm, D), lambda i, j: (i, 0)),
                  pl.BlockSpec((1, 1, mod3.shape[2]), lambda i, j: (i // per_b, 0, 0)),
                  pl.BlockSpec((1, D), lambda i, j: (0, 0)),
                  pl.BlockSpec((D, tf), lambda i, j: (0, j)),
                  pl.BlockSpec((D, tf), lambda i, j: (0, j))],
        out_specs=pl.BlockSpec((tm, tf), lambda i, j: (i, j)),
        out_shape=jax.ShapeDtypeStruct((N, FF), BF16),
        scratch_shapes=[pltpu.VMEM((tm, D), BF16)],
        compiler_params=_params(("arbitrary", "arbitrary")),
        name="ffn_up",
    )(x2, mod3, g2, w_gate, w_up)


def _ffn_down_kernel(u_ref, x_ref, mod_ref, gf_ref, wd_ref, o_ref, *, D):
    j = pl.program_id(1)

    @pl.when(j == 0)
    def _():
        o_ref[...] = jnp.zeros(o_ref.shape, F32)

    o_ref[...] += jnp.dot(u_ref[...], wd_ref[...], preferred_element_type=F32)

    @pl.when(j == pl.num_programs(1) - 1)
    def _():
        gt = mod_ref[0, :, 5 * D:6 * D]
        y = x_ref[...] + gt * o_ref[...]
        o_ref[...] = y * lax.rsqrt(jnp.mean(y * y, axis=-1, keepdims=True) + EPS) * gf_ref[...]


def _ffn_down(u2, x2, mod3, gf, w_down, S, tm=1024, tk=512):
    N, D = x2.shape
    FF = u2.shape[1]
    per_b = S // tm
    return pl.pallas_call(
        functools.partial(_ffn_down_kernel, D=D),
        grid=(N // tm, FF // tk),
        in_specs=[pl.BlockSpec((tm, tk), lambda i, j: (i, j)),
                  pl.BlockSpec((tm, D), lambda i, j: (i, 0)),
                  pl.BlockSpec((1, 1, mod3.shape[2]), lambda i, j: (i // per_b, 0, 0)),
                  pl.BlockSpec((1, D), lambda i, j: (0, 0)),
                  pl.BlockSpec((tk, D), lambda i, j: (j, 0))],
        out_specs=pl.BlockSpec((tm, D), lambda i, j: (i, 0)),
        out_shape=jax.ShapeDtypeStruct((N, D), F32),
        compiler_params=_params(("arbitrary", "arbitrary")),
        name="ffn_down",
    )(u2, x2, mod3, gf, w_down)


def kernel(x, c, positions, w_ada, b_ada, g_norm1, g_norm2, w_in, b_gates, w_conv, b_conv, g_mlstm_out,
           sink, g_attn_out, w_out, w_gate, w_up, w_down, g_final):
    B, S, D = x.shape
    depth = w_ada.shape[0]
    H = MLSTM_HEADS
    MW = D // 2
    KW = ATTN_KV_HEADS * ATTN_HEAD_DIM
    NG = 4 * H
    L = MLSTM_CHUNK
    nc = S // L
    g0 = 4 * MW
    x2 = x.reshape(B * S, D)
    pos2 = positions.reshape(B * S, 1)
    for l in range(depth):
        mod3 = _ada(c, w_ada[l], b_ada[l]).reshape(B, 1, 6 * D)
        wl = w_in[l]
        a0 = g0 + NG
        w_main = jnp.concatenate([wl[:, :2 * MW], wl[:, 3 * MW:g0], wl[:, a0:a0 + MW]], axis=1).astype(BF16)
        t_rows = MW + 2 * KW + NG
        t_pad = -t_rows % (2 * BF16_ROWS)
        w_t = jnp.concatenate([wl[:, 2 * MW:3 * MW], wl[:, a0 + MW + KW:], wl[:, a0 + MW:a0 + MW + KW],
                               wl[:, g0:a0], jnp.zeros((D, t_pad), wl.dtype)], axis=1).T.astype(BF16)
        proj, t3, gates_t = _inproj(x2, mod3, g_norm1[l].reshape(1, D), pos2, w_main, w_t, B, S, NG)
        proj3 = proj.reshape(B, S, proj.shape[1])
        gates = gates_t.reshape(4, H, B, nc, L).transpose(0, 3, 1, 2, 4).reshape(4, nc * H * B, L)
        gbias = jnp.broadcast_to(b_gates[l].reshape(4, 1, H, 1), (4, nc, H, B)).reshape(4, nc * H * B, 1)
        vec = _gatevec(gates, gbias, nc)
        vec5 = vec.reshape(2 * _NVEC, nc, H, B, L).transpose(3, 2, 0, 1, 4)
        m_out = _mlstm(proj3, t3, vec5, w_conv[l], b_conv[l].reshape(1, 2 * MW), g_mlstm_out[l].reshape(1, MW))
        a_out = _attn(proj3, t3, sink[l], g_attn_out[l].reshape(1, MW), q_col0=3 * MW,
                      vt_row0=MW, kt_row0=MW + KW)
        w_o = w_out[l].astype(BF16)
        x2 = _outproj(m_out.reshape(B * S, MW), a_out.reshape(B * S, MW), x2, mod3, w_o[:MW], w_o[MW:], S)
        assert l == depth - 1, "final norm is fused into the last layer's FFN"
        u2 = _ffn_up(x2, mod3, g_norm2[l].reshape(1, D), w_gate[l].astype(BF16), w_up[l].astype(BF16), S)
        x2 = _ffn_down(u2, x2, mod3, g_final.reshape(1, D), w_down[l].astype(BF16), S)
    return x2.reshape(B, S, D)
```

```python
import functools

import jax
import jax.numpy as jnp
from jax import lax
from jax.experimental import pallas as pl
from jax.experimental.pallas import tpu as pltpu

F32 = jnp.float32
BF16 = jnp.bfloat16

EPS = 1e-6
NEG_INF = -1e30
ROPE_THETA = 10000.0

MLSTM_HEADS = 4
MLSTM_CHUNK = 256
ATTN_HEAD_DIM = 128
ATTN_KV_HEADS = 2
WINDOW = 128
ATTN_BLOCK = 128

LANES = 128
BF16_ROWS = 16
VMEM_LIMIT = 56 * 1024 * 1024

_NT = (((1,), (1,)), ((), ()))


def _sigmoid(x):
    return 1.0 / (1.0 + jnp.exp(-x))


def _params(sem, vmem=VMEM_LIMIT):
    return pltpu.CompilerParams(dimension_semantics=sem, vmem_limit_bytes=vmem)


def _split_bf16(v):
    hi = v.astype(BF16)
    return hi, (v - hi.astype(F32)).astype(BF16)


def _ada_kernel(c_ref, w_ref, b_ref, o_ref):
    c = c_ref[...]
    s_hi, s_lo = _split_bf16(c * _sigmoid(c))
    w_hi, w_lo = _split_bf16(w_ref[...])
    dot = functools.partial(jnp.dot, preferred_element_type=F32)
    o_ref[...] = dot(s_hi, w_hi) + (dot(s_lo, w_hi) + dot(s_hi, w_lo)) + b_ref[...]


def _ada(c, w_ada, b_ada, tn=1024):
    B, D = c.shape
    N = w_ada.shape[1]
    return pl.pallas_call(
        _ada_kernel,
        grid=(N // tn,),
        in_specs=[pl.BlockSpec((B, D), lambda j: (0, 0)),
                  pl.BlockSpec((D, tn), lambda j: (0, j)),
                  pl.BlockSpec((1, tn), lambda j: (0, j))],
        out_specs=pl.BlockSpec((B, tn), lambda j: (0, j)),
        out_shape=jax.ShapeDtypeStruct((B, N), F32),
        compiler_params=_params(("arbitrary",)),
        name="ada",
    )(c, w_ada, b_ada.reshape(1, N))


def _rope(t, cos2, sin2):
    return t * cos2 + pltpu.roll(t, ATTN_HEAD_DIM // 2, 1) * sin2


def _inproj_kernel(x_ref, mod_ref, g_ref, pos_ref, inv_ref, sgn_ref, w_ref, wt_ref,
                   o_ref, t_ref, gt_ref, h_scr, cos_scr, sin_scr, cost_scr, sint_scr,
                   *, D, n_main, rope_j0, n_t, ak_rows, gate_rows):
    j = pl.program_id(1)
    hd = ATTN_HEAD_DIM

    @pl.when(j == 0)
    def _():
        x = x_ref[...]
        sh = mod_ref[0, :, 0:D]
        gsc = g_ref[...] * (1.0 + mod_ref[0, :, D:2 * D])
        h_scr[...] = (x * lax.rsqrt(jnp.mean(x * x, axis=-1, keepdims=True) + EPS) * gsc + sh).astype(BF16)
        ang = pos_ref[...].astype(F32) * inv_ref[...]
        cos2 = jnp.cos(ang)
        sin2 = jnp.sin(ang) * sgn_ref[...]
        cos_scr[...] = cos2
        sin_scr[...] = sin2
        cost_scr[...] = cos2.T
        sint_scr[...] = sin2.T

    @pl.when(j < rope_j0)
    def _():
        o_ref[...] = jnp.dot(h_scr[...], w_ref[...], preferred_element_type=F32).astype(BF16)

    @pl.when(jnp.logical_and(j >= rope_j0, j < n_main))
    def _():
        acc = jnp.dot(h_scr[...], w_ref[...], preferred_element_type=F32)
        cos2 = cos_scr[...]
        sin2 = sin_scr[...]
        parts = [_rope(acc[:, hh * hd:(hh + 1) * hd], cos2, sin2) for hh in range(acc.shape[1] // hd)]
        o_ref[...] = jnp.concatenate(parts, axis=1).astype(BF16)

    tt = t_ref.shape[1]
    for ti in range(n_t):
        @pl.when(j == n_main + ti)
        def _(ti=ti):
            acc = lax.dot_general(wt_ref[...], h_scr[...], _NT, preferred_element_type=F32)
            lo, hi = ti * tt, (ti + 1) * tt
            pieces = []
            cur = lo
            for r0 in range(ak_rows[0], ak_rows[1], hd):
                if r0 < lo or r0 + hd > hi:
                    assert r0 + hd <= lo or r0 >= hi, "a k head straddles two transposed tiles"
                    continue
                if r0 > cur:
                    pieces.append(acc[cur - lo:r0 - lo, :])
                t = acc[r0 - lo:r0 - lo + hd, :]
                rot = jnp.concatenate([t[hd // 2:, :], t[:hd // 2, :]], axis=0)
                pieces.append(t * cost_scr[...] + rot * sint_scr[...])
                cur = r0 + hd
            if cur < hi:
                pieces.append(acc[cur - lo:, :])
            t_ref[0] = jnp.concatenate(pieces, axis=0).astype(BF16) if len(pieces) > 1 else acc.astype(BF16)
            if lo <= gate_rows[0] and gate_rows[1] <= hi:
                gt_ref[...] = acc[gate_rows[0] - lo:gate_rows[1] - lo, :]


def _inproj(x2, mod3, g1, pos2, w_main, w_t, B, S, NG, tm=1024, tn=512, n_t=2):
    N, D = x2.shape
    P = w_main.shape[1]
    TR = w_t.shape[0]
    hd = ATTN_HEAD_DIM
    half = hd // 2
    inv = ROPE_THETA ** (-jnp.arange(half, dtype=F32) * 2.0 / hd)
    inv2 = jnp.concatenate([inv, inv]).reshape(1, hd)
    sgn = jnp.concatenate([-jnp.ones((half,), F32), jnp.ones((half,), F32)]).reshape(1, hd)
    MW = D // 2
    KW = ATTN_KV_HEADS * hd
    aq0 = 3 * MW
    ak_rows = (MW + KW, MW + 2 * KW)
    gate_rows = (MW + 2 * KW, MW + 2 * KW + NG)
    tt = TR // n_t
    assert aq0 % tn == 0 and P == aq0 + MW and P % tn == 0 and TR % n_t == 0 and tt % BF16_ROWS == 0
    assert gate_rows[1] <= TR
    n_main = P // tn
    kern = functools.partial(_inproj_kernel, D=D, n_main=n_main, rope_j0=aq0 // tn, n_t=n_t,
                             ak_rows=ak_rows, gate_rows=gate_rows)
    per_b = S // tm
    main_j = lambda j: jnp.minimum(j, n_main - 1)
    t_j = lambda j: jnp.maximum(j - n_main, 0)
    return pl.pallas_call(
        kern,
        grid=(N // tm, n_main + n_t),
        in_specs=[pl.BlockSpec((tm, D), lambda i, j: (i, 0)),
                  pl.BlockSpec((1, 1, mod3.shape[2]), lambda i, j: (i // per_b, 0, 0)),
                  pl.BlockSpec((1, D), lambda i, j: (0, 0)),
                  pl.BlockSpec((tm, 1), lambda i, j: (i, 0)),
                  pl.BlockSpec((1, hd), lambda i, j: (0, 0)),
                  pl.BlockSpec((1, hd), lambda i, j: (0, 0)),
                  pl.BlockSpec((D, tn), lambda i, j: (0, main_j(j))),
                  pl.BlockSpec((tt, D), lambda i, j: (t_j(j), 0))],
        out_specs=[pl.BlockSpec((tm, tn), lambda i, j: (i, main_j(j))),
                   pl.BlockSpec((1, tt, tm), lambda i, j: (i // per_b, t_j(j), i % per_b)),
                   pl.BlockSpec((NG, tm), lambda i, j: (0, i))],
        out_shape=[jax.ShapeDtypeStruct((N, P), BF16),
                   jax.ShapeDtypeStruct((B, TR, S), BF16),
                   jax.ShapeDtypeStruct((NG, N), F32)],
        scratch_shapes=[pltpu.VMEM((tm, D), BF16),
                        pltpu.VMEM((tm, hd), F32),
                        pltpu.VMEM((tm, hd), F32),
                        pltpu.VMEM((hd, tm), F32),
                        pltpu.VMEM((hd, tm), F32)],
        compiler_params=_params(("arbitrary", "arbitrary")),
        name="inproj",
    )(x2, mod3, g1, pos2, inv2, sgn, w_main, w_t)


_R, _MT, _A, _LIM, _E, _SC = range(6)
_NVEC = 6


def _log_sigmoid(x):
    return jnp.minimum(x, 0.0) - jnp.log1p(jnp.exp(-jnp.abs(x)))


def _lane_scan(x, reverse, op, ident):
    L = x.shape[1]
    lane = lax.broadcasted_iota(jnp.int32, x.shape, 1)
    sh = 1
    while sh < L:
        if reverse:
            x = op(x, jnp.where(lane < L - sh, pltpu.roll(x, L - sh, 1), ident))
        else:
            x = op(x, jnp.where(lane >= sh, pltpu.roll(x, sh, 1), ident))
        sh *= 2
    return x


def _chunk_scan(x, reverse, op, ident):
    n = x.shape[1] // LANES
    parts = [_lane_scan(x[:, i * LANES:(i + 1) * LANES], reverse, op, ident) for i in range(n)]
    order = range(n - 2, -1, -1) if reverse else range(1, n)
    for i in order:
        prev = parts[i + 1][:, 0:1] if reverse else parts[i - 1][:, LANES - 1:LANES]
        parts[i] = op(parts[i], prev)
    return jnp.concatenate(parts, axis=1)


def _gate_vectors(li, lf_pre, reverse, nc):
    rows, Lc = li.shape
    hb = rows // nc
    lf = _log_sigmoid(lf_pre)
    b = _chunk_scan(lf, reverse, jnp.add, 0.0)
    r = li - b
    rcm = _chunk_scan(r, reverse, jnp.maximum, NEG_INF)
    end = slice(0, 1) if reverse else slice(Lc - 1, Lc)
    bl = jnp.broadcast_to(b[:, end], (rows, Lc))
    rmax = jnp.broadcast_to(rcm[:, end], (rows, Lc))
    m = jnp.zeros((hb, Lc), F32)
    m_parts = [None] * nc
    for j in (range(nc - 1, -1, -1) if reverse else range(nc)):
        m_parts[j] = m
        sl = slice(j * hb, (j + 1) * hb)
        m = bl[sl, :] + jnp.maximum(m, rmax[sl, :])
    m_all = jnp.concatenate(m_parts, axis=0)
    mt = jnp.maximum(m_all, rcm)
    m_end = jnp.maximum(m_all, rmax)
    return r, mt, jnp.exp(m_all - mt), jnp.exp(-(b + mt)), jnp.exp(r - m_end), jnp.exp(m_all - m_end)


def _gatevec_kernel(g_ref, gb_ref, o_ref, *, nc):
    for dr in range(2):
        li = g_ref[2 * dr] + gb_ref[2 * dr]
        lf_pre = g_ref[2 * dr + 1] + gb_ref[2 * dr + 1]
        for idx, val in enumerate(_gate_vectors(li, lf_pre, dr == 1, nc)):
            o_ref[dr * _NVEC + idx] = val


def _gatevec(gates, gbias, nc):
    _, rows, Lc = gates.shape
    return pl.pallas_call(
        functools.partial(_gatevec_kernel, nc=nc),
        out_shape=jax.ShapeDtypeStruct((2 * _NVEC, rows, Lc), F32),
        compiler_params=pltpu.CompilerParams(vmem_limit_bytes=VMEM_LIMIT),
        name="gatevec",
    )(gates, gbias)


CONV_ROWS = 128


def _mlstm_kernel(q_ref, k_ref, vt_ref, mo_ref, vec_ref, wq_ref, wk_ref, bq_ref, bk_ref, go_ref,
                  o_ref, pad_scr, qs_scr, big_scr, vt_scr, ns_scr, ct_scr, n_scr, *, S, d, Lc, CW):
    nc = S // Lc
    half_w = CW // 2
    PADR = 8
    CR = CONV_ROWS
    N0 = Lc
    C0 = Lc + BF16_ROWS

    zero_rows = jnp.zeros((PADR, d), F32)

    def conv_silu(src_ref, w_ref, b_ref, store, scale):
        pad_scr[0:PADR, :] = zero_rows
        pad_scr[PADR + S:2 * PADR + S, :] = zero_rows

        def fill(j, c):
            r0 = pl.multiple_of(j * CR, CR)
            pad_scr[pl.ds(PADR + r0, CR), :] = src_ref[0, pl.ds(r0, CR), :].astype(F32)
            return c
        lax.fori_loop(0, S // CR, fill, 0)
        w = w_ref[...]
        bias = b_ref[...]

        def body(j, c):
            r0 = pl.multiple_of(j * CR, CR)
            win = pad_scr[pl.ds(r0, CR + 2 * PADR), :]
            y = bias
            for i in range(CW):
                o = PADR - half_w + i
                y = y + win[o:o + CR, :] * w[i:i + 1, :]
            y = y * _sigmoid(y)
            if scale != 1.0:
                y = y * scale
            store(j, r0, y.astype(BF16))
            return c
        lax.fori_loop(0, S // CR, body, 0)

    def store_q(j, r0, y):
        qs_scr[pl.ds(r0, CR), :] = y

    per = Lc // CR

    def store_k(j, r0, y):
        big_scr[j // per, pl.ds(pl.multiple_of((j % per) * CR, CR), CR), :] = y

    conv_silu(q_ref, wq_ref, bq_ref, store_q, 1.0)
    conv_silu(k_ref, wk_ref, bk_ref, store_k, d ** -0.5)

    for j in range(nc):
        vt_scr[j] = vt_ref[0, :, j * Lc:(j + 1) * Lc]

    ns_scr[...] = jnp.zeros(ns_scr.shape, F32)
    ct_scr[...] = jnp.zeros(ct_scr.shape, F32)
    n_scr[...] = jnp.zeros(n_scr.shape, F32)

    def state_step(dr, jc):
        base = dr * _NVEC
        e_row = vec_ref[0, 0, base + _E, pl.ds(jc, 1), :]
        sc = vec_ref[0, 0, base + _SC, pl.ds(jc, 1), :]
        lhs = jnp.concatenate([vt_scr[jc].astype(F32) * e_row, jnp.broadcast_to(e_row, (BF16_ROWS, Lc))], axis=0)
        upd = jnp.dot(lhs.astype(BF16), big_scr[jc, 0:Lc, :], preferred_element_type=F32)
        ct = ct_scr[dr]
        n = n_scr[dr]
        big_scr[jc, C0 + dr * d:C0 + (dr + 1) * d, :] = ct.astype(BF16)
        ns_scr[jc, dr:dr + 1, :] = n[0:1, :]
        ct_scr[dr] = sc * ct + upd[0:d, :]
        n_scr[dr] = sc * n + upd[d:d + 8, :]

    def phase1(j, c):
        state_step(0, j)
        state_step(1, nc - 1 - j)
        return c
    lax.fori_loop(0, nc, phase1, 0)

    def put_n(j, c):
        big_scr[j, N0:N0 + BF16_ROWS, :] = ns_scr[j].astype(BF16)
        return c
    lax.fori_loop(0, nc, put_n, 0)

    rr = lax.broadcasted_iota(jnp.int32, (Lc, Lc), 0)
    cc = lax.broadcasted_iota(jnp.int32, (Lc, Lc), 1)
    r1 = lax.broadcasted_iota(jnp.int32, (LANES, LANES), 0)
    c1 = lax.broadcasted_iota(jnp.int32, (LANES, LANES), 1)
    eye = r1 == c1
    gout = go_ref[...]

    def to_col(row):
        return jnp.concatenate(
            [jnp.sum(jnp.where(eye, row[:, i * LANES:(i + 1) * LANES], 0.0), axis=1, keepdims=True)
             for i in range(Lc // LANES)], axis=0)

    def phase2(j, c):
        r0 = pl.multiple_of(j * Lc, Lc)
        qj = qs_scr[pl.ds(r0, Lc), :]
        hrows = C0 + d
        p = jnp.concatenate(
            [lax.dot_general(big_scr[j, 0:hrows, :], qj, _NT, preferred_element_type=F32),
             lax.dot_general(big_scr[j, hrows:hrows + d, :], qj, _NT, preferred_element_type=F32)], axis=0)
        st = p[0:Lc, :]
        qkw = []
        inv_den = []
        a_rows = []
        for dr in range(2):
            base = dr * _NVEC
            r_row = vec_ref[0, 0, base + _R, pl.ds(j, 1), :]
            mt_row = vec_ref[0, 0, base + _MT, pl.ds(j, 1), :]
            a_row = vec_ref[0, 0, base + _A, pl.ds(j, 1), :]
            lim_row = vec_ref[0, 0, base + _LIM, pl.ds(j, 1), :]
            valid = (rr <= cc) if dr == 0 else (rr >= cc)
            w = jnp.exp(jnp.where(valid, to_col(r_row) - mt_row, NEG_INF))
            qkw_d = st * w
            den = a_row * p[N0 + dr:N0 + dr + 1, :] + jnp.sum(qkw_d, axis=0, keepdims=True)
            inv_den.append(1.0 / jnp.maximum(jnp.abs(den), lim_row))
            a_rows.append(a_row)
            qkw.append(qkw_d.astype(BF16))
        intra = jnp.dot(vt_scr[j], jnp.concatenate(qkw, axis=1), preferred_element_type=F32)
        hs = None
        for dr in range(2):
            inter = p[C0 + dr * d:C0 + (dr + 1) * d, :]
            hd = (a_rows[dr] * inter + intra[:, dr * Lc:(dr + 1) * Lc]) * inv_den[dr]
            hs = hd if hs is None else hs + hd
        mu = jnp.mean(hs, axis=0, keepdims=True)
        hc = hs - mu
        hn = (hc * lax.rsqrt(jnp.mean(hc * hc, axis=0, keepdims=True) + EPS)).T
        mo = mo_ref[0, pl.ds(r0, Lc), :].astype(F32)
        o_ref[0, pl.ds(r0, Lc), :] = (_sigmoid(mo) * hn * gout).astype(BF16)
        return c
    lax.fori_loop(0, nc, phase2, 0)


def _mlstm(proj3, vt3, vec5, w_conv, b_conv, g_out):
    B, S, _ = proj3.shape
    H = MLSTM_HEADS
    d = g_out.shape[1] // H
    Lc = MLSTM_CHUNK
    nc = S // Lc
    CW = w_conv.shape[0]
    assert Lc % LANES == 0 and Lc % CONV_ROWS == 0 and d == Lc and CW // 2 <= 8
    kern = functools.partial(_mlstm_kernel, S=S, d=d, Lc=Lc, CW=CW)
    col = lambda off: (lambda b, h: (b, 0, off + h))
    return pl.pallas_call(
        kern,
        grid=(B, H),
        in_specs=[pl.BlockSpec((1, S, d), col(0)),
                  pl.BlockSpec((1, S, d), col(H)),
                  pl.BlockSpec((1, d, S), lambda b, h: (b, h, 0)),
                  pl.BlockSpec((1, S, d), col(2 * H)),
                  pl.BlockSpec((1, 1, 2 * _NVEC, nc, Lc), lambda b, h: (b, h, 0, 0, 0)),
                  pl.BlockSpec((CW, d), lambda b, h: (0, h)),
                  pl.BlockSpec((CW, d), lambda b, h: (0, H + h)),
                  pl.BlockSpec((1, d), lambda b, h: (0, h)),
                  pl.BlockSpec((1, d), lambda b, h: (0, H + h)),
                  pl.BlockSpec((1, d), lambda b, h: (0, h))],
        out_specs=pl.BlockSpec((1, S, d), lambda b, h: (b, 0, h)),
        out_shape=jax.ShapeDtypeStruct((B, S, H * d), BF16),
        scratch_shapes=[pltpu.VMEM((S + 16, d), F32),
                        pltpu.VMEM((S, d), BF16),
                        pltpu.VMEM((nc, Lc + BF16_ROWS + 2 * d, d), BF16),
                        pltpu.VMEM((nc, d, Lc), BF16),
                        pltpu.VMEM((nc, BF16_ROWS, d), F32),
                        pltpu.VMEM((2, d, d), F32),
                        pltpu.VMEM((2, 8, d), F32)],
        compiler_params=_params(("arbitrary", "arbitrary")),
        name="mlstm",
    )(proj3, proj3, vt3, proj3, vec5, w_conv, w_conv, b_conv, b_conv, g_out)


LOG2E = 1.4426950408889634


def _attn_kernel(sink_ref, q_ref, ktp_ref, ktc_ref, ktn_ref, vtp_ref, vtc_ref, vtn_ref, g_ref, o_ref, *, H, G):
    n = pl.program_id(1)
    nb = pl.num_programs(1)
    blk = ATTN_BLOCK
    hd = ATTN_HEAD_DIM
    R = H // G
    c1 = (hd ** -0.5) * LOG2E
    rr = lax.broadcasted_iota(jnp.int32, (blk, blk), 0)
    cc = lax.broadcasted_iota(jnp.int32, (blk, blk), 1)
    m_prev = jnp.logical_and(cc >= rr, n > 0)
    m_next = jnp.logical_and(cc <= rr, n < nb - 1)
    q = q_ref[0]
    ones = jnp.ones((hd, 3 * blk), BF16)
    outs = []
    for g in range(G):
        rs = slice(g * hd, (g + 1) * hd)
        kt = jnp.concatenate([ktp_ref[0][rs, :], ktc_ref[0][rs, :], ktn_ref[0][rs, :]], axis=1)
        vt = jnp.concatenate([vtp_ref[0][rs, :], vtc_ref[0][rs, :], vtn_ref[0][rs, :]], axis=1)
        vt_aug = jnp.concatenate([vt, ones], axis=0)
        qg = jnp.concatenate([q[:, (g * R + r) * hd:(g * R + r + 1) * hd] for r in range(R)], axis=0)
        s = jnp.dot(qg, kt, preferred_element_type=F32)
        ps = []
        extras = []
        for r in range(R):
            sr = s[r * blk:(r + 1) * blk, :] * c1
            t0 = jnp.where(m_prev, sr[:, 0:blk], NEG_INF)
            t1 = sr[:, blk:2 * blk]
            t2 = jnp.where(m_next, sr[:, 2 * blk:3 * blk], NEG_INF)
            sink2 = sink_ref[g * R + r] * LOG2E
            m = jnp.maximum(jnp.max(jnp.maximum(jnp.maximum(t0, t1), t2), axis=1, keepdims=True), sink2)
            ps.append(jnp.concatenate([jnp.exp2(t0 - m), jnp.exp2(t1 - m), jnp.exp2(t2 - m)], axis=1).astype(BF16))
            extras.append(jnp.exp2(sink2 - m))
        oa = lax.dot_general(jnp.concatenate(ps, axis=0), vt_aug, _NT, preferred_element_type=F32)
        for r in range(R):
            o_r = oa[r * blk:(r + 1) * blk, :]
            outs.append(o_r[:, 0:hd] / (o_r[:, hd:2 * hd] + extras[r]))
    oall = jnp.concatenate(outs, axis=1)
    y = oall * lax.rsqrt(jnp.mean(oall * oall, axis=-1, keepdims=True) + EPS) * g_ref[...]
    o_ref[0] = y.astype(BF16)


def _attn(proj3, t3, sink, g_attn, q_col0, vt_row0, kt_row0):
    B, S, _ = proj3.shape
    blk = ATTN_BLOCK
    nb = S // blk
    H = sink.shape[0]
    G = ATTN_KV_HEADS
    AW = H * ATTN_HEAD_DIM
    KW = G * ATTN_HEAD_DIM
    assert WINDOW == blk and q_col0 % AW == 0 and vt_row0 % KW == 0 and kt_row0 % KW == 0
    qb, vb, kb = q_col0 // AW, vt_row0 // KW, kt_row0 // KW
    kern = functools.partial(_attn_kernel, H=H, G=G)
    prev = lambda rb: (lambda b, n: (b, rb, jnp.maximum(n - 1, 0)))
    cur = lambda rb: (lambda b, n: (b, rb, n))
    nxt = lambda rb: (lambda b, n: (b, rb, jnp.minimum(n + 1, nb - 1)))
    return pl.pallas_call(
        kern,
        grid=(B, nb),
        in_specs=[pl.BlockSpec(memory_space=pltpu.SMEM),
                  pl.BlockSpec((1, blk, AW), lambda b, n: (b, n, qb)),
                  pl.BlockSpec((1, KW, blk), prev(kb)),
                  pl.BlockSpec((1, KW, blk), cur(kb)),
                  pl.BlockSpec((1, KW, blk), nxt(kb)),
                  pl.BlockSpec((1, KW, blk), prev(vb)),
                  pl.BlockSpec((1, KW, blk), cur(vb)),
                  pl.BlockSpec((1, KW, blk), nxt(vb)),
                  pl.BlockSpec((1, AW), lambda b, n: (0, 0))],
        out_specs=pl.BlockSpec((1, blk, AW), lambda b, n: (b, n, 0)),
        out_shape=jax.ShapeDtypeStruct((B, S, AW), BF16),
        compiler_params=_params(("arbitrary", "arbitrary")),
        name="attn",
    )(sink, proj3, t3, t3, t3, t3, t3, t3, g_attn)


def _outproj_kernel(m_ref, a_ref, x_ref, mod_ref, g2_ref, wt_ref, wb_ref, o_ref, h_ref, *, D):
    acc = jnp.dot(m_ref[...], wt_ref[...], preferred_element_type=F32)
    acc = acc + jnp.dot(a_ref[...], wb_ref[...], preferred_element_type=F32)
    gt = mod_ref[0, :, 2 * D:3 * D]
    x1 = x_ref[...] + gt * acc
    o_ref[...] = x1
    sh = mod_ref[0, :, 3 * D:4 * D]
    gsc = g2_ref[...] * (1.0 + mod_ref[0, :, 4 * D:5 * D])
    h_ref[...] = (x1 * lax.rsqrt(jnp.mean(x1 * x1, axis=-1, keepdims=True) + EPS) * gsc + sh).astype(BF16)


def _outproj(m2, a2, x2, mod3, g2, w_top, w_bot, S, tm=512):
    N, D = x2.shape
    K = m2.shape[1]
    per_b = S // tm
    return pl.pallas_call(
        functools.partial(_outproj_kernel, D=D),
        grid=(N // tm,),
        in_specs=[pl.BlockSpec((tm, K), lambda i: (i, 0)),
                  pl.BlockSpec((tm, K), lambda i: (i, 0)),
                  pl.BlockSpec((tm, D), lambda i: (i, 0)),
                  pl.BlockSpec((1, 1, mod3.shape[2]), lambda i: (i // per_b, 0, 0)),
                  pl.BlockSpec((1, D), lambda i: (0, 0)),
                  pl.BlockSpec((K, D), lambda i: (0, 0)),
                  pl.BlockSpec((K, D), lambda i: (0, 0))],
        out_specs=[pl.BlockSpec((tm, D), lambda i: (i, 0)),
                   pl.BlockSpec((tm, D), lambda i: (i, 0))],
        out_shape=[jax.ShapeDtypeStruct((N, D), F32),
                   jax.ShapeDtypeStruct((N, D), BF16)],
        compiler_params=_params(("arbitrary",)),
        name="outproj",
    )(m2, a2, x2, mod3, g2, w_top, w_bot)


def _ffn_up_kernel(h_ref, wg_ref, wu_ref, o_ref):
    hb = h_ref[...]
    gate = jnp.dot(hb, wg_ref[...].astype(BF16), preferred_element_type=F32)
    up = jnp.dot(hb, wu_ref[...].astype(BF16), preferred_element_type=F32)
    o_ref[...] = (gate * _sigmoid(gate) * up).astype(BF16)


def _ffn_up(h2, w_gate, w_up, tm=2048, tf=256):
    N, D = h2.shape
    FF = w_gate.shape[1]
    return pl.pallas_call(
        _ffn_up_kernel,
        grid=(N // tm, FF // tf),
        in_specs=[pl.BlockSpec((tm, D), lambda i, j: (i, 0)),
                  pl.BlockSpec((D, tf), lambda i, j: (0, j)),
                  pl.BlockSpec((D, tf), lambda i, j: (0, j))],
        out_specs=pl.BlockSpec((tm, tf), lambda i, j: (i, j)),
        out_shape=jax.ShapeDtypeStruct((N, FF), BF16),
        compiler_params=_params(("arbitrary", "arbitrary")),
        name="ffn_up",
    )(h2, w_gate, w_up)


def _ffn_down_kernel(u_ref, x_ref, mod_ref, gf_ref, wd_ref, o_ref, *, D):
    j = pl.program_id(1)

    @pl.when(j == 0)
    def _():
        o_ref[...] = jnp.zeros(o_ref.shape, F32)

    o_ref[...] += jnp.dot(u_ref[...], wd_ref[...], preferred_element_type=F32)

    @pl.when(j == pl.num_programs(1) - 1)
    def _():
        gt = mod_ref[0, :, 5 * D:6 * D]
        y = x_ref[...] + gt * o_ref[...]
        o_ref[...] = y * lax.rsqrt(jnp.mean(y * y, axis=-1, keepdims=True) + EPS) * gf_ref[...]


def _ffn_down(u2, x2, mod3, gf, w_down, S, tm=1024, tk=512):
    N, D = x2.shape
    FF = u2.shape[1]
    per_b = S // tm
    return pl.pallas_call(
        functools.partial(_ffn_down_kernel, D=D),
        grid=(N // tm, FF // tk),
        in_specs=[pl.BlockSpec((tm, tk), lambda i, j: (i, j)),
                  pl.BlockSpec((tm, D), lambda i, j: (i, 0)),
                  pl.BlockSpec((1, 1, mod3.shape[2]), lambda i, j: (i // per_b, 0, 0)),
                  pl.BlockSpec((1, D), lambda i, j: (0, 0)),
                  pl.BlockSpec((tk, D), lambda i, j: (j, 0))],
        out_specs=pl.BlockSpec((tm, D), lambda i, j: (i, 0)),
        out_shape=jax.ShapeDtypeStruct((N, D), F32),
        compiler_params=_params(("arbitrary", "arbitrary")),
        name="ffn_down",
    )(u2, x2, mod3, gf, w_down)


def kernel(x, c, positions, w_ada, b_ada, g_norm1, g_norm2, w_in, b_gates, w_conv, b_conv, g_mlstm_out,
           sink, g_attn_out, w_out, w_gate, w_up, w_down, g_final):
    B, S, D = x.shape
    depth = w_ada.shape[0]
    H = MLSTM_HEADS
    MW = D // 2
    KW = ATTN_KV_HEADS * ATTN_HEAD_DIM
    NG = 4 * H
    L = MLSTM_CHUNK
    nc = S // L
    g0 = 4 * MW
    x2 = x.reshape(B * S, D)
    pos2 = positions.reshape(B * S, 1)
    for l in range(depth):
        mod3 = _ada(c, w_ada[l], b_ada[l]).reshape(B, 1, 6 * D)
        wl = w_in[l].astype(BF16)
        a0 = g0 + NG
        w_main = jnp.concatenate([wl[:, :2 * MW], wl[:, 3 * MW:g0], wl[:, a0:a0 + MW]], axis=1)
        t_rows = MW + 2 * KW + NG
        t_pad = -t_rows % (2 * BF16_ROWS)
        w_t = jnp.concatenate([wl[:, 2 * MW:3 * MW], wl[:, a0 + MW + KW:], wl[:, a0 + MW:a0 + MW + KW],
                               wl[:, g0:a0], jnp.zeros((D, t_pad), BF16)], axis=1).T
        proj, t3, gates_t = _inproj(x2, mod3, g_norm1[l].reshape(1, D), pos2, w_main, w_t, B, S, NG)
        proj3 = proj.reshape(B, S, proj.shape[1])
        gates = gates_t.reshape(4, H, B, nc, L).transpose(0, 3, 1, 2, 4).reshape(4, nc * H * B, L)
        gbias = jnp.broadcast_to(b_gates[l].reshape(4, 1, H, 1), (4, nc, H, B)).reshape(4, nc * H * B, 1)
        vec = _gatevec(gates, gbias, nc)
        vec5 = vec.reshape(2 * _NVEC, nc, H, B, L).transpose(3, 2, 0, 1, 4)
        m_out = _mlstm(proj3, t3, vec5, w_conv[l], b_conv[l].reshape(1, 2 * MW), g_mlstm_out[l].reshape(1, MW))
        a_out = _attn(proj3, t3, sink[l], g_attn_out[l].reshape(1, MW), q_col0=3 * MW,
                      vt_row0=MW, kt_row0=MW + KW)
        w_o = w_out[l].astype(BF16)
        x2, h2 = _outproj(m_out.reshape(B * S, MW), a_out.reshape(B * S, MW), x2, mod3,
                          g_norm2[l].reshape(1, D), w_o[:MW], w_o[MW:], S)
        assert l == depth - 1, "final norm is fused into the last layer's FFN"
        u2 = _ffn_up(h2, w_gate[l], w_up[l])
        x2 = _ffn_down(u2, x2, mod3, g_final.reshape(1, D), w_down[l].astype(BF16), S)
    return x2.reshape(B, S, D)
```

```python
import functools

import jax
import jax.numpy as jnp
from jax import lax
from jax.experimental import pallas as pl
from jax.experimental.pallas import tpu as pltpu

F32 = jnp.float32
BF16 = jnp.bfloat16

EPS = 1e-6
NEG_INF = -1e30
ROPE_THETA = 10000.0

MLSTM_HEADS = 4
MLSTM_CHUNK = 256
ATTN_HEAD_DIM = 128
ATTN_KV_HEADS = 2
WINDOW = 128
ATTN_BLOCK = 128

LANES = 128
BF16_ROWS = 16
VMEM_LIMIT = 56 * 1024 * 1024

_NT = (((1,), (1,)), ((), ()))


def _sigmoid(x):
    return 1.0 / (1.0 + jnp.exp(-x))


def _params(sem, vmem=VMEM_LIMIT):
    return pltpu.CompilerParams(dimension_semantics=sem, vmem_limit_bytes=vmem)


def _split_bf16(v):
    hi = v.astype(BF16)
    return hi, (v - hi.astype(F32)).astype(BF16)


def _ada_kernel(c_ref, w_ref, b_ref, o_ref):
    c = c_ref[...]
    s_hi, s_lo = _split_bf16(c * _sigmoid(c))
    w_hi, w_lo = _split_bf16(w_ref[...])
    dot = functools.partial(jnp.dot, preferred_element_type=F32)
    o_ref[...] = dot(s_hi, w_hi) + (dot(s_lo, w_hi) + dot(s_hi, w_lo)) + b_ref[...]


def _ada(c, w_ada, b_ada, tn=1024):
    B, D = c.shape
    N = w_ada.shape[1]
    return pl.pallas_call(
        _ada_kernel,
        grid=(N // tn,),
        in_specs=[pl.BlockSpec((B, D), lambda j: (0, 0)),
                  pl.BlockSpec((D, tn), lambda j: (0, j)),
                  pl.BlockSpec((1, tn), lambda j: (0, j))],
        out_specs=pl.BlockSpec((B, tn), lambda j: (0, j)),
        out_shape=jax.ShapeDtypeStruct((B, N), F32),
        compiler_params=_params(("arbitrary",)),
        name="ada",
    )(c, w_ada, b_ada.reshape(1, N))


def _rope(t, cos2, sin2):
    return t * cos2 + pltpu.roll(t, ATTN_HEAD_DIM // 2, 1) * sin2


PREP_SLABS = 8
PREP_STEP0 = 2


def _inproj_kernel(x_hbm, mod_ref, g_ref, pos_ref, inv_ref, sgn_ref, w_ref, wt_ref,
                   o_ref, t_ref, gt_ref, xbuf, h_scr, cos_scr, sin_scr, cost_scr, sint_scr, sem,
                   *, D, tm, n_tiles, n_main, rope_j0, n_t, ak_rows, gate_rows):
    r = pl.program_id(0)
    j = pl.program_id(1)
    hd = ATTN_HEAD_DIM
    rows = tm // PREP_SLABS
    ps = r % 2
    cs = 1 - ps
    tile_p = jnp.minimum(r, n_tiles - 1)

    def x_copy():
        return pltpu.make_async_copy(x_hbm.at[pl.ds(pl.multiple_of(tile_p * tm, tm), tm), :], xbuf, sem)

    @pl.when(j == 0)
    def _():
        x_copy().start()

    @pl.when(j == PREP_STEP0 - 1)
    def _():
        x_copy().wait()

    def prep_slab():
        k = j - PREP_STEP0
        r0 = pl.multiple_of(k * rows, rows)
        x = xbuf[pl.ds(r0, rows), :]
        sh = mod_ref[0, :, 0:D]
        gsc = g_ref[...] * (1.0 + mod_ref[0, :, D:2 * D])
        h_scr[ps, pl.ds(r0, rows), :] = (
            x * lax.rsqrt(jnp.mean(x * x, axis=-1, keepdims=True) + EPS) * gsc + sh).astype(BF16)
        ang = pos_ref[pl.ds(r0, rows), :].astype(F32) * inv_ref[...]
        cos2 = jnp.cos(ang)
        sin2 = jnp.sin(ang) * sgn_ref[...]
        cos_scr[ps, pl.ds(r0, rows), :] = cos2
        sin_scr[ps, pl.ds(r0, rows), :] = sin2
        cost_scr[ps, k] = cos2.T
        sint_scr[ps, k] = sin2.T

    def plain():
        o_ref[...] = jnp.dot(h_scr[cs], w_ref[...], preferred_element_type=F32).astype(BF16)

    def roped():
        acc = jnp.dot(h_scr[cs], w_ref[...], preferred_element_type=F32)
        cos2 = cos_scr[cs]
        sin2 = sin_scr[cs]
        parts = [_rope(acc[:, hh * hd:(hh + 1) * hd], cos2, sin2) for hh in range(acc.shape[1] // hd)]
        o_ref[...] = jnp.concatenate(parts, axis=1).astype(BF16)

    tt = t_ref.shape[1]

    def transposed(ti):
        acc = lax.dot_general(wt_ref[...], h_scr[cs], _NT, preferred_element_type=F32)
        lo, hi = ti * tt, (ti + 1) * tt
        pieces = []
        cur = lo
        for a0 in range(ak_rows[0], ak_rows[1], hd):
            if a0 < lo or a0 + hd > hi:
                assert a0 + hd <= lo or a0 >= hi, "a k head straddles two transposed tiles"
                continue
            if a0 > cur:
                pieces.append(acc[cur - lo:a0 - lo, :])
            t = acc[a0 - lo:a0 - lo + hd, :]
            rot = jnp.concatenate([t[hd // 2:, :], t[:hd // 2, :]], axis=0)
            cost = jnp.concatenate([cost_scr[cs, k] for k in range(PREP_SLABS)], axis=1)
            sint = jnp.concatenate([sint_scr[cs, k] for k in range(PREP_SLABS)], axis=1)
            pieces.append(t * cost + rot * sint)
            cur = a0 + hd
        if cur < hi:
            pieces.append(acc[cur - lo:, :])
        t_ref[0] = jnp.concatenate(pieces, axis=0).astype(BF16) if len(pieces) > 1 else acc.astype(BF16)
        if lo <= gate_rows[0] and gate_rows[1] <= hi:
            gt_ref[...] = acc[gate_rows[0] - lo:gate_rows[1] - lo, :]

    mult = r > 0
    both = jnp.logical_and

    @pl.when(both(mult, j < PREP_STEP0))
    def _():
        plain()

    @pl.when(both(mult, both(j >= PREP_STEP0, j < rope_j0)))
    def _():
        plain()
        prep_slab()

    @pl.when(both(mult, both(j >= rope_j0, j < n_main)))
    def _():
        roped()
        prep_slab()

    for ti in range(n_t):
        @pl.when(both(mult, j == n_main + ti))
        def _(ti=ti):
            transposed(ti)
            prep_slab()

    @pl.when(both(jnp.logical_not(mult), j >= PREP_STEP0))
    def _():
        prep_slab()


def _inproj(x2, mod3, g1, pos2, w_main, w_t, B, S, NG, tm=1024, tn=512, n_t=2):
    N, D = x2.shape
    P = w_main.shape[1]
    TR = w_t.shape[0]
    hd = ATTN_HEAD_DIM
    half = hd // 2
    inv = ROPE_THETA ** (-jnp.arange(half, dtype=F32) * 2.0 / hd)
    inv2 = jnp.concatenate([inv, inv]).reshape(1, hd)
    sgn = jnp.concatenate([-jnp.ones((half,), F32), jnp.ones((half,), F32)]).reshape(1, hd)
    MW = D // 2
    KW = ATTN_KV_HEADS * hd
    aq0 = 3 * MW
    ak_rows = (MW + KW, MW + 2 * KW)
    gate_rows = (MW + 2 * KW, MW + 2 * KW + NG)
    tt = TR // n_t
    n_tiles = N // tm
    n_main = P // tn
    n_j = n_main + n_t
    rope_j0 = aq0 // tn
    assert aq0 % tn == 0 and P == aq0 + MW and P % tn == 0 and TR % n_t == 0 and tt % BF16_ROWS == 0
    assert gate_rows[1] <= TR and tm // PREP_SLABS == hd and n_j == PREP_STEP0 + PREP_SLABS
    assert PREP_STEP0 <= rope_j0
    kern = functools.partial(_inproj_kernel, D=D, tm=tm, n_tiles=n_tiles, n_main=n_main, rope_j0=rope_j0,
                             n_t=n_t, ak_rows=ak_rows, gate_rows=gate_rows)
    per_b = S // tm
    tile_p = lambda r: jnp.minimum(r, n_tiles - 1)
    tile_c = lambda r: jnp.maximum(r - 1, 0)
    live = lambda r: jnp.minimum(r, 1)
    main_j = lambda r, j: jnp.minimum(j, n_main - 1) * live(r)
    t_j = lambda r, j: jnp.maximum(j - n_main, 0) * live(r)
    return pl.pallas_call(
        kern,
        grid=(n_tiles + 1, n_j),
        in_specs=[pl.BlockSpec(memory_space=pl.ANY),
                  pl.BlockSpec((1, 1, mod3.shape[2]), lambda r, j: (tile_p(r) // per_b, 0, 0)),
                  pl.BlockSpec((1, D), lambda r, j: (0, 0)),
                  pl.BlockSpec((tm, 1), lambda r, j: (tile_p(r), 0)),
                  pl.BlockSpec((1, hd), lambda r, j: (0, 0)),
                  pl.BlockSpec((1, hd), lambda r, j: (0, 0)),
                  pl.BlockSpec((D, tn), lambda r, j: (0, jnp.minimum(j, n_main - 1))),
                  pl.BlockSpec((tt, D), lambda r, j: (jnp.maximum(j - n_main, 0), 0))],
        out_specs=[pl.BlockSpec((tm, tn), lambda r, j: (tile_c(r), main_j(r, j))),
                   pl.BlockSpec((1, tt, tm), lambda r, j: (tile_c(r) // per_b, t_j(r, j), tile_c(r) % per_b)),
                   pl.BlockSpec((NG, tm), lambda r, j: (0, tile_c(r)))],
        out_shape=[jax.ShapeDtypeStruct((N, P), BF16),
                   jax.ShapeDtypeStruct((B, TR, S), BF16),
                   jax.ShapeDtypeStruct((NG, N), F32)],
        scratch_shapes=[pltpu.VMEM((tm, D), F32),
                        pltpu.VMEM((2, tm, D), BF16),
                        pltpu.VMEM((2, tm, hd), F32),
                        pltpu.VMEM((2, tm, hd), F32),
                        pltpu.VMEM((2, PREP_SLABS, hd, hd), F32),
                        pltpu.VMEM((2, PREP_SLABS, hd, hd), F32),
                        pltpu.SemaphoreType.DMA(())],
        compiler_params=_params(("arbitrary", "arbitrary")),
        name="inproj",
    )(x2, mod3, g1, pos2, inv2, sgn, w_main, w_t)


_R, _MT, _A, _LIM, _E, _SC = range(6)
_NVEC = 6


def _log_sigmoid(x):
    return jnp.minimum(x, 0.0) - jnp.log1p(jnp.exp(-jnp.abs(x)))


def _lane_scan(x, reverse, op, ident):
    L = x.shape[1]
    lane = lax.broadcasted_iota(jnp.int32, x.shape, 1)
    sh = 1
    while sh < L:
        if reverse:
            x = op(x, jnp.where(lane < L - sh, pltpu.roll(x, L - sh, 1), ident))
        else:
            x = op(x, jnp.where(lane >= sh, pltpu.roll(x, sh, 1), ident))
        sh *= 2
    return x


def _chunk_scan(x, reverse, op, ident):
    n = x.shape[1] // LANES
    parts = [_lane_scan(x[:, i * LANES:(i + 1) * LANES], reverse, op, ident) for i in range(n)]
    order = range(n - 2, -1, -1) if reverse else range(1, n)
    for i in order:
        prev = parts[i + 1][:, 0:1] if reverse else parts[i - 1][:, LANES - 1:LANES]
        parts[i] = op(parts[i], prev)
    return jnp.concatenate(parts, axis=1)


def _gate_vectors(li, lf_pre, reverse, nc):
    rows, Lc = li.shape
    hb = rows // nc
    lf = _log_sigmoid(lf_pre)
    b = _chunk_scan(lf, reverse, jnp.add, 0.0)
    r = li - b
    rcm = _chunk_scan(r, reverse, jnp.maximum, NEG_INF)
    end = slice(0, 1) if reverse else slice(Lc - 1, Lc)
    bl = jnp.broadcast_to(b[:, end], (rows, Lc))
    rmax = jnp.broadcast_to(rcm[:, end], (rows, Lc))
    m = jnp.zeros((hb, Lc), F32)
    m_parts = [None] * nc
    for j in (range(nc - 1, -1, -1) if reverse else range(nc)):
        m_parts[j] = m
        sl = slice(j * hb, (j + 1) * hb)
        m = bl[sl, :] + jnp.maximum(m, rmax[sl, :])
    m_all = jnp.concatenate(m_parts, axis=0)
    mt = jnp.maximum(m_all, rcm)
    m_end = jnp.maximum(m_all, rmax)
    return r, mt, jnp.exp(m_all - mt), jnp.exp(-(b + mt)), jnp.exp(r - m_end), jnp.exp(m_all - m_end)


def _gatevec_kernel(g_ref, gb_ref, o_ref, *, nc):
    for dr in range(2):
        li = g_ref[2 * dr] + gb_ref[2 * dr]
        lf_pre = g_ref[2 * dr + 1] + gb_ref[2 * dr + 1]
        for idx, val in enumerate(_gate_vectors(li, lf_pre, dr == 1, nc)):
            o_ref[dr * _NVEC + idx] = val


def _gatevec(gates, gbias, nc):
    _, rows, Lc = gates.shape
    return pl.pallas_call(
        functools.partial(_gatevec_kernel, nc=nc),
        out_shape=jax.ShapeDtypeStruct((2 * _NVEC, rows, Lc), F32),
        compiler_params=pltpu.CompilerParams(vmem_limit_bytes=VMEM_LIMIT),
        name="gatevec",
    )(gates, gbias)


CONV_ROWS = 128


def _mlstm_kernel(q_ref, k_ref, vt_ref, mo_ref, vec_ref, wq_ref, wk_ref, bq_ref, bk_ref, go_ref,
                  o_ref, pad_scr, qs_scr, big_scr, vt_scr, ns_scr, ct_scr, n_scr, *, S, d, Lc, CW):
    nc = S // Lc
    half_w = CW // 2
    PADR = 8
    CR = CONV_ROWS
    N0 = Lc
    C0 = Lc + BF16_ROWS

    zero_rows = jnp.zeros((PADR, d), F32)

    def conv_silu(src_ref, w_ref, b_ref, store, scale):
        pad_scr[0:PADR, :] = zero_rows
        pad_scr[PADR + S:2 * PADR + S, :] = zero_rows

        def fill(j, c):
            r0 = pl.multiple_of(j * CR, CR)
            pad_scr[pl.ds(PADR + r0, CR), :] = src_ref[0, pl.ds(r0, CR), :].astype(F32)
            return c
        lax.fori_loop(0, S // CR, fill, 0)
        w = w_ref[...]
        bias = b_ref[...]

        def body(j, c):
            r0 = pl.multiple_of(j * CR, CR)
            win = pad_scr[pl.ds(r0, CR + 2 * PADR), :]
            y = bias
            for i in range(CW):
                o = PADR - half_w + i
                y = y + win[o:o + CR, :] * w[i:i + 1, :]
            y = y * _sigmoid(y)
            if scale != 1.0:
                y = y * scale
            store(j, r0, y.astype(BF16))
            return c
        lax.fori_loop(0, S // CR, body, 0)

    def store_q(j, r0, y):
        qs_scr[pl.ds(r0, CR), :] = y

    per = Lc // CR

    def store_k(j, r0, y):
        big_scr[j // per, pl.ds(pl.multiple_of((j % per) * CR, CR), CR), :] = y

    conv_silu(q_ref, wq_ref, bq_ref, store_q, 1.0)
    conv_silu(k_ref, wk_ref, bk_ref, store_k, d ** -0.5)

    for j in range(nc):
        vt_scr[j] = vt_ref[0, :, j * Lc:(j + 1) * Lc]

    ns_scr[...] = jnp.zeros(ns_scr.shape, F32)
    ct_scr[...] = jnp.zeros(ct_scr.shape, F32)
    n_scr[...] = jnp.zeros(n_scr.shape, F32)

    def state_step(dr, jc):
        base = dr * _NVEC
        e_row = vec_ref[0, 0, base + _E, pl.ds(jc, 1), :]
        sc = vec_ref[0, 0, base + _SC, pl.ds(jc, 1), :]
        lhs = jnp.concatenate([vt_scr[jc].astype(F32) * e_row, jnp.broadcast_to(e_row, (BF16_ROWS, Lc))], axis=0)
        upd = jnp.dot(lhs.astype(BF16), big_scr[jc, 0:Lc, :], preferred_element_type=F32)
        ct = ct_scr[dr]
        n = n_scr[dr]
        big_scr[jc, C0 + dr * d:C0 + (dr + 1) * d, :] = ct.astype(BF16)
        ns_scr[jc, dr:dr + 1, :] = n[0:1, :]
        ct_scr[dr] = sc * ct + upd[0:d, :]
        n_scr[dr] = sc * n + upd[d:d + 8, :]

    def phase1(j, c):
        state_step(0, j)
        state_step(1, nc - 1 - j)
        return c
    lax.fori_loop(0, nc, phase1, 0)

    def put_n(j, c):
        big_scr[j, N0:N0 + BF16_ROWS, :] = ns_scr[j].astype(BF16)
        return c
    lax.fori_loop(0, nc, put_n, 0)

    rr = lax.broadcasted_iota(jnp.int32, (Lc, Lc), 0)
    cc = lax.broadcasted_iota(jnp.int32, (Lc, Lc), 1)
    r1 = lax.broadcasted_iota(jnp.int32, (LANES, LANES), 0)
    c1 = lax.broadcasted_iota(jnp.int32, (LANES, LANES), 1)
    eye = r1 == c1
    gout = go_ref[...]

    def to_col(row):
        return jnp.concatenate(
            [jnp.sum(jnp.where(eye, row[:, i * LANES:(i + 1) * LANES], 0.0), axis=1, keepdims=True)
             for i in range(Lc // LANES)], axis=0)

    def phase2(j, c):
        r0 = pl.multiple_of(j * Lc, Lc)
        qj = qs_scr[pl.ds(r0, Lc), :]
        hrows = C0 + d
        p = jnp.concatenate(
            [lax.dot_general(big_scr[j, 0:hrows, :], qj, _NT, preferred_element_type=F32),
             lax.dot_general(big_scr[j, hrows:hrows + d, :], qj, _NT, preferred_element_type=F32)], axis=0)
        st = p[0:Lc, :]
        qkw = []
        inv_den = []
        a_rows = []
        for dr in range(2):
            base = dr * _NVEC
            r_row = vec_ref[0, 0, base + _R, pl.ds(j, 1), :]
            mt_row = vec_ref[0, 0, base + _MT, pl.ds(j, 1), :]
            a_row = vec_ref[0, 0, base + _A, pl.ds(j, 1), :]
            lim_row = vec_ref[0, 0, base + _LIM, pl.ds(j, 1), :]
            valid = (rr <= cc) if dr == 0 else (rr >= cc)
            w = jnp.exp(jnp.where(valid, to_col(r_row) - mt_row, NEG_INF))
            qkw_d = st * w
            den = a_row * p[N0 + dr:N0 + dr + 1, :] + jnp.sum(qkw_d, axis=0, keepdims=True)
            inv_den.append(1.0 / jnp.maximum(jnp.abs(den), lim_row))
            a_rows.append(a_row)
            qkw.append(qkw_d.astype(BF16))
        intra = jnp.dot(vt_scr[j], jnp.concatenate(qkw, axis=1), preferred_element_type=F32)
        hs = None
        for dr in range(2):
            inter = p[C0 + dr * d:C0 + (dr + 1) * d, :]
            hd = (a_rows[dr] * inter + intra[:, dr * Lc:(dr + 1) * Lc]) * inv_den[dr]
            hs = hd if hs is None else hs + hd
        mu = jnp.mean(hs, axis=0, keepdims=True)
        hc = hs - mu
        hn = (hc * lax.rsqrt(jnp.mean(hc * hc, axis=0, keepdims=True) + EPS)).T
        mo = mo_ref[0, pl.ds(r0, Lc), :].astype(F32)
        o_ref[0, pl.ds(r0, Lc), :] = (_sigmoid(mo) * hn * gout).astype(BF16)
        return c
    lax.fori_loop(0, nc, phase2, 0)


def _mlstm(proj3, vt3, vec5, w_conv, b_conv, g_out):
    B, S, _ = proj3.shape
    H = MLSTM_HEADS
    d = g_out.shape[1] // H
    Lc = MLSTM_CHUNK
    nc = S // Lc
    CW = w_conv.shape[0]
    assert Lc % LANES == 0 and Lc % CONV_ROWS == 0 and d == Lc and CW // 2 <= 8
    kern = functools.partial(_mlstm_kernel, S=S, d=d, Lc=Lc, CW=CW)
    col = lambda off: (lambda b, h: (b, 0, off + h))
    return pl.pallas_call(
        kern,
        grid=(B, H),
        in_specs=[pl.BlockSpec((1, S, d), col(0)),
                  pl.BlockSpec((1, S, d), col(H)),
                  pl.BlockSpec((1, d, S), lambda b, h: (b, h, 0)),
                  pl.BlockSpec((1, S, d), col(2 * H)),
                  pl.BlockSpec((1, 1, 2 * _NVEC, nc, Lc), lambda b, h: (b, h, 0, 0, 0)),
                  pl.BlockSpec((CW, d), lambda b, h: (0, h)),
                  pl.BlockSpec((CW, d), lambda b, h: (0, H + h)),
                  pl.BlockSpec((1, d), lambda b, h: (0, h)),
                  pl.BlockSpec((1, d), lambda b, h: (0, H + h)),
                  pl.BlockSpec((1, d), lambda b, h: (0, h))],
        out_specs=pl.BlockSpec((1, S, d), lambda b, h: (b, 0, h)),
        out_shape=jax.ShapeDtypeStruct((B, S, H * d), BF16),
        scratch_shapes=[pltpu.VMEM((S + 16, d), F32),
                        pltpu.VMEM((S, d), BF16),
                        pltpu.VMEM((nc, Lc + BF16_ROWS + 2 * d, d), BF16),
                        pltpu.VMEM((nc, d, Lc), BF16),
                        pltpu.VMEM((nc, BF16_ROWS, d), F32),
                        pltpu.VMEM((2, d, d), F32),
                        pltpu.VMEM((2, 8, d), F32)],
        compiler_params=_params(("arbitrary", "arbitrary")),
        name="mlstm",
    )(proj3, proj3, vt3, proj3, vec5, w_conv, w_conv, b_conv, b_conv, g_out)


LOG2E = 1.4426950408889634


def _attn_kernel(sink_ref, q_ref, ktp_ref, ktc_ref, ktn_ref, vtp_ref, vtc_ref, vtn_ref, g_ref, o_ref, *, H, G):
    n = pl.program_id(1)
    nb = pl.num_programs(1)
    blk = ATTN_BLOCK
    hd = ATTN_HEAD_DIM
    R = H // G
    c1 = (hd ** -0.5) * LOG2E
    rr = lax.broadcasted_iota(jnp.int32, (blk, blk), 0)
    cc = lax.broadcasted_iota(jnp.int32, (blk, blk), 1)
    m_prev = jnp.logical_and(cc >= rr, n > 0)
    m_next = jnp.logical_and(cc <= rr, n < nb - 1)
    q = q_ref[0]
    ones = jnp.ones((hd, 3 * blk), BF16)
    outs = []
    for g in range(G):
        rs = slice(g * hd, (g + 1) * hd)
        kt = jnp.concatenate([ktp_ref[0][rs, :], ktc_ref[0][rs, :], ktn_ref[0][rs, :]], axis=1)
        vt = jnp.concatenate([vtp_ref[0][rs, :], vtc_ref[0][rs, :], vtn_ref[0][rs, :]], axis=1)
        vt_aug = jnp.concatenate([vt, ones], axis=0)
        qg = jnp.concatenate([q[:, (g * R + r) * hd:(g * R + r + 1) * hd] for r in range(R)], axis=0)
        s = jnp.dot(qg, kt, preferred_element_type=F32)
        ps = []
        extras = []
        for r in range(R):
            sr = s[r * blk:(r + 1) * blk, :] * c1
            t0 = jnp.where(m_prev, sr[:, 0:blk], NEG_INF)
            t1 = sr[:, blk:2 * blk]
            t2 = jnp.where(m_next, sr[:, 2 * blk:3 * blk], NEG_INF)
            sink2 = sink_ref[g * R + r] * LOG2E
            m = jnp.maximum(jnp.max(jnp.maximum(jnp.maximum(t0, t1), t2), axis=1, keepdims=True), sink2)
            ps.append(jnp.concatenate([jnp.exp2(t0 - m), jnp.exp2(t1 - m), jnp.exp2(t2 - m)], axis=1).astype(BF16))
            extras.append(jnp.exp2(sink2 - m))
        oa = lax.dot_general(jnp.concatenate(ps, axis=0), vt_aug, _NT, preferred_element_type=F32)
        for r in range(R):
            o_r = oa[r * blk:(r + 1) * blk, :]
            outs.append(o_r[:, 0:hd] / (o_r[:, hd:2 * hd] + extras[r]))
    oall = jnp.concatenate(outs, axis=1)
    y = oall * lax.rsqrt(jnp.mean(oall * oall, axis=-1, keepdims=True) + EPS) * g_ref[...]
    o_ref[0] = y.astype(BF16)


def _attn(proj3, t3, sink, g_attn, q_col0, vt_row0, kt_row0):
    B, S, _ = proj3.shape
    blk = ATTN_BLOCK
    nb = S // blk
    H = sink.shape[0]
    G = ATTN_KV_HEADS
    AW = H * ATTN_HEAD_DIM
    KW = G * ATTN_HEAD_DIM
    assert WINDOW == blk and q_col0 % AW == 0 and vt_row0 % KW == 0 and kt_row0 % KW == 0
    qb, vb, kb = q_col0 // AW, vt_row0 // KW, kt_row0 // KW
    kern = functools.partial(_attn_kernel, H=H, G=G)
    prev = lambda rb: (lambda b, n: (b, rb, jnp.maximum(n - 1, 0)))
    cur = lambda rb: (lambda b, n: (b, rb, n))
    nxt = lambda rb: (lambda b, n: (b, rb, jnp.minimum(n + 1, nb - 1)))
    return pl.pallas_call(
        kern,
        grid=(B, nb),
        in_specs=[pl.BlockSpec(memory_space=pltpu.SMEM),
                  pl.BlockSpec((1, blk, AW), lambda b, n: (b, n, qb)),
                  pl.BlockSpec((1, KW, blk), prev(kb)),
                  pl.BlockSpec((1, KW, blk), cur(kb)),
                  pl.BlockSpec((1, KW, blk), nxt(kb)),
                  pl.BlockSpec((1, KW, blk), prev(vb)),
                  pl.BlockSpec((1, KW, blk), cur(vb)),
                  pl.BlockSpec((1, KW, blk), nxt(vb)),
                  pl.BlockSpec((1, AW), lambda b, n: (0, 0))],
        out_specs=pl.BlockSpec((1, blk, AW), lambda b, n: (b, n, 0)),
        out_shape=jax.ShapeDtypeStruct((B, S, AW), BF16),
        compiler_params=_params(("arbitrary", "arbitrary")),
        name="attn",
    )(sink, proj3, t3, t3, t3, t3, t3, t3, g_attn)


def _outproj_kernel(m_ref, a_ref, x_ref, mod_ref, g2_ref, wt_ref, wb_ref, o_ref, h_ref, *, D):
    acc = jnp.dot(m_ref[...], wt_ref[...], preferred_element_type=F32)
    acc = acc + jnp.dot(a_ref[...], wb_ref[...], preferred_element_type=F32)
    gt = mod_ref[0, :, 2 * D:3 * D]
    x1 = x_ref[...] + gt * acc
    o_ref[...] = x1
    sh = mod_ref[0, :, 3 * D:4 * D]
    gsc = g2_ref[...] * (1.0 + mod_ref[0, :, 4 * D:5 * D])
    h_ref[...] = (x1 * lax.rsqrt(jnp.mean(x1 * x1, axis=-1, keepdims=True) + EPS) * gsc + sh).astype(BF16)


def _outproj(m2, a2, x2, mod3, g2, w_o, S, tm=512):
    N, D = x2.shape
    K = m2.shape[1]
    per_b = S // tm
    return pl.pallas_call(
        functools.partial(_outproj_kernel, D=D),
        grid=(N // tm,),
        in_specs=[pl.BlockSpec((tm, K), lambda i: (i, 0)),
                  pl.BlockSpec((tm, K), lambda i: (i, 0)),
                  pl.BlockSpec((tm, D), lambda i: (i, 0)),
                  pl.BlockSpec((1, 1, mod3.shape[2]), lambda i: (i // per_b, 0, 0)),
                  pl.BlockSpec((1, D), lambda i: (0, 0)),
                  pl.BlockSpec((K, D), lambda i: (0, 0)),
                  pl.BlockSpec((K, D), lambda i: (1, 0))],
        out_specs=[pl.BlockSpec((tm, D), lambda i: (i, 0)),
                   pl.BlockSpec((tm, D), lambda i: (i, 0))],
        out_shape=[jax.ShapeDtypeStruct((N, D), F32),
                   jax.ShapeDtypeStruct((N, D), BF16)],
        compiler_params=_params(("arbitrary",)),
        name="outproj",
    )(m2, a2, x2, mod3, g2, w_o, w_o)


def _ffn_up_kernel(h_ref, wg_ref, wu_ref, o_ref):
    hb = h_ref[...]
    gate = jnp.dot(hb, wg_ref[...].astype(BF16), preferred_element_type=F32)
    up = jnp.dot(hb, wu_ref[...].astype(BF16), preferred_element_type=F32)
    o_ref[...] = (gate * _sigmoid(gate) * up).astype(BF16)


def _ffn_up(h2, w_gate, w_up, tm=2048, tf=256):
    N, D = h2.shape
    FF = w_gate.shape[1]
    return pl.pallas_call(
        _ffn_up_kernel,
        grid=(N // tm, FF // tf),
        in_specs=[pl.BlockSpec((tm, D), lambda i, j: (i, 0)),
                  pl.BlockSpec((D, tf), lambda i, j: (0, j)),
                  pl.BlockSpec((D, tf), lambda i, j: (0, j))],
        out_specs=pl.BlockSpec((tm, tf), lambda i, j: (i, j)),
        out_shape=jax.ShapeDtypeStruct((N, FF), BF16),
        compiler_params=_params(("arbitrary", "arbitrary")),
        name="ffn_up",
    )(h2, w_gate, w_up)


def _ffn_down_kernel(u_ref, x_hbm, mod_ref, gf_ref, wd_ref, o_ref, xbuf, sem, *, D, tm):
    i = pl.program_id(0)
    j = pl.program_id(1)

    def x_copy():
        return pltpu.make_async_copy(x_hbm.at[pl.ds(pl.multiple_of(i * tm, tm), tm), :], xbuf, sem)

    @pl.when(j == 0)
    def _():
        x_copy().start()
        o_ref[...] = jnp.zeros(o_ref.shape, F32)

    o_ref[...] += jnp.dot(u_ref[...], wd_ref[...], preferred_element_type=F32)

    @pl.when(j == pl.num_programs(1) - 1)
    def _():
        x_copy().wait()
        gt = mod_ref[0, :, 5 * D:6 * D]
        y = xbuf[...] + gt * o_ref[...]
        o_ref[...] = y * lax.rsqrt(jnp.mean(y * y, axis=-1, keepdims=True) + EPS) * gf_ref[...]


def _ffn_down(u2, x2, mod3, gf, w_down, S, tm=1024, tk=512):
    N, D = x2.shape
    FF = u2.shape[1]
    per_b = S // tm
    return pl.pallas_call(
        functools.partial(_ffn_down_kernel, D=D, tm=tm),
        grid=(N // tm, FF // tk),
        in_specs=[pl.BlockSpec((tm, tk), lambda i, j: (i, j)),
                  pl.BlockSpec(memory_space=pl.ANY),
                  pl.BlockSpec((1, 1, mod3.shape[2]), lambda i, j: (i // per_b, 0, 0)),
                  pl.BlockSpec((1, D), lambda i, j: (0, 0)),
                  pl.BlockSpec((tk, D), lambda i, j: (j, 0))],
        out_specs=pl.BlockSpec((tm, D), lambda i, j: (i, 0)),
        out_shape=jax.ShapeDtypeStruct((N, D), F32),
        scratch_shapes=[pltpu.VMEM((tm, D), F32), pltpu.SemaphoreType.DMA(())],
        compiler_params=_params(("arbitrary", "arbitrary")),
        name="ffn_down",
    )(u2, x2, mod3, gf, w_down)


def kernel(x, c, positions, w_ada, b_ada, g_norm1, g_norm2, w_in, b_gates, w_conv, b_conv, g_mlstm_out,
           sink, g_attn_out, w_out, w_gate, w_up, w_down, g_final):
    B, S, D = x.shape
    depth = w_ada.shape[0]
    H = MLSTM_HEADS
    MW = D // 2
    KW = ATTN_KV_HEADS * ATTN_HEAD_DIM
    NG = 4 * H
    L = MLSTM_CHUNK
    nc = S // L
    g0 = 4 * MW
    x2 = x.reshape(B * S, D)
    pos2 = positions.reshape(B * S, 1)
    for l in range(depth):
        mod3 = _ada(c, w_ada[l], b_ada[l]).reshape(B, 1, 6 * D)
        wl = w_in[l].astype(BF16)
        a0 = g0 + NG
        w_main = jnp.concatenate([wl[:, :2 * MW], wl[:, 3 * MW:g0], wl[:, a0:a0 + MW]], axis=1)
        t_rows = MW + 2 * KW + NG
        t_pad = -t_rows % (2 * BF16_ROWS)
        w_t = jnp.concatenate([wl[:, 2 * MW:3 * MW], wl[:, a0 + MW + KW:], wl[:, a0 + MW:a0 + MW + KW],
                               wl[:, g0:a0], jnp.zeros((D, t_pad), BF16)], axis=1).T
        proj, t3, gates_t = _inproj(x2, mod3, g_norm1[l].reshape(1, D), pos2, w_main, w_t, B, S, NG)
        proj3 = proj.reshape(B, S, proj.shape[1])
        gates = gates_t.reshape(4, H, B, nc, L).transpose(0, 3, 1, 2, 4).reshape(4, nc * H * B, L)
        gbias = jnp.broadcast_to(b_gates[l].reshape(4, 1, H, 1), (4, nc, H, B)).reshape(4, nc * H * B, 1)
        vec = _gatevec(gates, gbias, nc)
        vec5 = vec.reshape(2 * _NVEC, nc, H, B, L).transpose(3, 2, 0, 1, 4)
        m_out = _mlstm(proj3, t3, vec5, w_conv[l], b_conv[l].reshape(1, 2 * MW), g_mlstm_out[l].reshape(1, MW))
        a_out = _attn(proj3, t3, sink[l], g_attn_out[l].reshape(1, MW), q_col0=3 * MW,
                      vt_row0=MW, kt_row0=MW + KW)
        x2, h2 = _outproj(m_out.reshape(B * S, MW), a_out.reshape(B * S, MW), x2, mod3,
                          g_norm2[l].reshape(1, D), w_out[l].astype(BF16), S)
        assert l == depth - 1, "final norm is fused into the last layer's FFN"
        u2 = _ffn_up(h2, w_gate[l], w_up[l])
        x2 = _ffn_down(u2, x2, mod3, g_final.reshape(1, D), w_down[l].astype(BF16), S)
    return x2.reshape(B, S, D)
```

```python
import functools

import jax
import jax.numpy as jnp
from jax import lax
from jax.experimental import pallas as pl
from jax.experimental.pallas import tpu as pltpu

F32 = jnp.float32
BF16 = jnp.bfloat16

EPS = 1e-6
NEG_INF = -1e30
ROPE_THETA = 10000.0

MLSTM_HEADS = 4
MLSTM_CHUNK = 256
ATTN_HEAD_DIM = 128
ATTN_KV_HEADS = 2
WINDOW = 128
ATTN_BLOCK = 128

LANES = 128
BF16_ROWS = 16
VMEM_LIMIT = 56 * 1024 * 1024

_NT = (((1,), (1,)), ((), ()))


def _sigmoid(x):
    return 1.0 / (1.0 + jnp.exp(-x))


def _params(sem, vmem=VMEM_LIMIT):
    return pltpu.CompilerParams(dimension_semantics=sem, vmem_limit_bytes=vmem)


def _split_bf16(v):
    hi = v.astype(BF16)
    return hi, (v - hi.astype(F32)).astype(BF16)


def _ada_kernel(c_ref, w_ref, b_ref, o_ref):
    c = c_ref[...]
    s_hi, s_lo = _split_bf16(c * _sigmoid(c))
    w_hi, w_lo = _split_bf16(w_ref[...])
    dot = functools.partial(jnp.dot, preferred_element_type=F32)
    o_ref[...] = dot(s_hi, w_hi) + (dot(s_lo, w_hi) + dot(s_hi, w_lo)) + b_ref[...]


def _ada(c, w_ada, b_ada, tn=1024):
    B, D = c.shape
    N = w_ada.shape[1]
    return pl.pallas_call(
        _ada_kernel,
        grid=(N // tn,),
        in_specs=[pl.BlockSpec((B, D), lambda j: (0, 0)),
                  pl.BlockSpec((D, tn), lambda j: (0, j)),
                  pl.BlockSpec((1, tn), lambda j: (0, j))],
        out_specs=pl.BlockSpec((B, tn), lambda j: (0, j)),
        out_shape=jax.ShapeDtypeStruct((B, N), F32),
        compiler_params=_params(("arbitrary",)),
        name="ada",
    )(c, w_ada, b_ada.reshape(1, N))


def _rope(t, cos2, sin2):
    return t * cos2 + pltpu.roll(t, ATTN_HEAD_DIM // 2, 1) * sin2


PREP_SLABS = 4
PREP_STEP0 = 2


def _inproj_kernel(x_hbm, mod_ref, g_ref, pos_ref, inv_ref, sgn_ref, w_ref, wt_ref,
                   o_ref, t_ref, gt_ref, xbuf, h_scr, cos_scr, sin_scr, cost_scr, sint_scr, sem,
                   *, D, tm, n_tiles, n_main, rope_j0, n_t, ak_rows, gate_rows):
    r = pl.program_id(0)
    j = pl.program_id(1)
    hd = ATTN_HEAD_DIM
    rows = tm // PREP_SLABS
    ps = r % 2
    cs = 1 - ps
    tile_p = jnp.minimum(r, n_tiles - 1)

    def x_copy():
        return pltpu.make_async_copy(x_hbm.at[pl.ds(pl.multiple_of(tile_p * tm, tm), tm), :], xbuf, sem)

    @pl.when(j == 0)
    def _():
        x_copy().start()

    @pl.when(j == PREP_STEP0 - 1)
    def _():
        x_copy().wait()

    def prep_slab():
        k = j - PREP_STEP0
        r0 = pl.multiple_of(k * rows, rows)
        x = xbuf[pl.ds(r0, rows), :]
        sh = mod_ref[0, :, 0:D]
        gsc = g_ref[...] * (1.0 + mod_ref[0, :, D:2 * D])
        h_scr[ps, pl.ds(r0, rows), :] = (
            x * lax.rsqrt(jnp.mean(x * x, axis=-1, keepdims=True) + EPS) * gsc + sh).astype(BF16)
        ang = pos_ref[pl.ds(r0, rows), :].astype(F32) * inv_ref[...]
        cos2 = jnp.cos(ang)
        sin2 = jnp.sin(ang) * sgn_ref[...]
        cos_scr[ps, pl.ds(r0, rows), :] = cos2
        sin_scr[ps, pl.ds(r0, rows), :] = sin2
        cost_scr[ps, k] = cos2.T
        sint_scr[ps, k] = sin2.T

    def plain():
        o_ref[...] = jnp.dot(h_scr[cs], w_ref[...], preferred_element_type=F32).astype(BF16)

    def roped():
        acc = jnp.dot(h_scr[cs], w_ref[...], preferred_element_type=F32)
        cos2 = cos_scr[cs]
        sin2 = sin_scr[cs]
        parts = [_rope(acc[:, hh * hd:(hh + 1) * hd], cos2, sin2) for hh in range(acc.shape[1] // hd)]
        o_ref[...] = jnp.concatenate(parts, axis=1).astype(BF16)

    tt = t_ref.shape[1]

    def transposed(ti):
        acc = lax.dot_general(wt_ref[...], h_scr[cs], _NT, preferred_element_type=F32)
        lo, hi = ti * tt, (ti + 1) * tt
        pieces = []
        cur = lo
        for a0 in range(ak_rows[0], ak_rows[1], hd):
            if a0 < lo or a0 + hd > hi:
                assert a0 + hd <= lo or a0 >= hi, "a k head straddles two transposed tiles"
                continue
            if a0 > cur:
                pieces.append(acc[cur - lo:a0 - lo, :])
            t = acc[a0 - lo:a0 - lo + hd, :]
            rot = jnp.concatenate([t[hd // 2:, :], t[:hd // 2, :]], axis=0)
            cost = jnp.concatenate([cost_scr[cs, k] for k in range(PREP_SLABS)], axis=1)
            sint = jnp.concatenate([sint_scr[cs, k] for k in range(PREP_SLABS)], axis=1)
            pieces.append(t * cost + rot * sint)
            cur = a0 + hd
        if cur < hi:
            pieces.append(acc[cur - lo:, :])
        t_ref[0] = jnp.concatenate(pieces, axis=0).astype(BF16) if len(pieces) > 1 else acc.astype(BF16)
        if lo <= gate_rows[0] and gate_rows[1] <= hi:
            gt_ref[...] = acc[gate_rows[0] - lo:gate_rows[1] - lo, :]

    mult = r > 0
    both = jnp.logical_and

    @pl.when(both(mult, j < PREP_STEP0))
    def _():
        plain()

    @pl.when(both(mult, both(j >= PREP_STEP0, j < rope_j0)))
    def _():
        plain()
        prep_slab()

    @pl.when(both(mult, both(j >= rope_j0, j < n_main)))
    def _():
        roped()
        prep_slab()

    for ti in range(n_t):
        @pl.when(both(mult, j == n_main + ti))
        def _(ti=ti):
            transposed(ti)
            prep_slab()

    @pl.when(both(jnp.logical_not(mult), j >= PREP_STEP0))
    def _():
        prep_slab()


def _inproj(x2, mod3, g1, pos2, w_main, w_t, B, S, NG, tm=1024, tn=1024, n_t=2):
    N, D = x2.shape
    P = w_main.shape[1]
    TR = w_t.shape[0]
    hd = ATTN_HEAD_DIM
    half = hd // 2
    inv = ROPE_THETA ** (-jnp.arange(half, dtype=F32) * 2.0 / hd)
    inv2 = jnp.concatenate([inv, inv]).reshape(1, hd)
    sgn = jnp.concatenate([-jnp.ones((half,), F32), jnp.ones((half,), F32)]).reshape(1, hd)
    MW = D // 2
    KW = ATTN_KV_HEADS * hd
    aq0 = 3 * MW
    ak_rows = (MW + KW, MW + 2 * KW)
    gate_rows = (MW + 2 * KW, MW + 2 * KW + NG)
    tt = TR // n_t
    n_tiles = N // tm
    n_main = P // tn
    n_j = n_main + n_t
    rope_j0 = aq0 // tn
    assert aq0 % tn == 0 and P == aq0 + MW and P % tn == 0 and TR % n_t == 0 and tt % BF16_ROWS == 0
    assert gate_rows[1] <= TR and tm % PREP_SLABS == 0 and n_j == PREP_STEP0 + PREP_SLABS
    assert PREP_STEP0 <= rope_j0
    kern = functools.partial(_inproj_kernel, D=D, tm=tm, n_tiles=n_tiles, n_main=n_main, rope_j0=rope_j0,
                             n_t=n_t, ak_rows=ak_rows, gate_rows=gate_rows)
    per_b = S // tm
    tile_p = lambda r: jnp.minimum(r, n_tiles - 1)
    tile_c = lambda r: jnp.maximum(r - 1, 0)
    live = lambda r: jnp.minimum(r, 1)
    main_j = lambda r, j: jnp.minimum(j, n_main - 1) * live(r)
    t_j = lambda r, j: jnp.maximum(j - n_main, 0) * live(r)
    return pl.pallas_call(
        kern,
        grid=(n_tiles + 1, n_j),
        in_specs=[pl.BlockSpec(memory_space=pl.ANY),
                  pl.BlockSpec((1, 1, mod3.shape[2]), lambda r, j: (tile_p(r) // per_b, 0, 0)),
                  pl.BlockSpec((1, D), lambda r, j: (0, 0)),
                  pl.BlockSpec((tm, 1), lambda r, j: (tile_p(r), 0)),
                  pl.BlockSpec((1, hd), lambda r, j: (0, 0)),
                  pl.BlockSpec((1, hd), lambda r, j: (0, 0)),
                  pl.BlockSpec((D, tn), lambda r, j: (0, jnp.minimum(j, n_main - 1))),
                  pl.BlockSpec((tt, D), lambda r, j: (jnp.maximum(j - n_main, 0), 0))],
        out_specs=[pl.BlockSpec((tm, tn), lambda r, j: (tile_c(r), main_j(r, j))),
                   pl.BlockSpec((1, tt, tm), lambda r, j: (tile_c(r) // per_b, t_j(r, j), tile_c(r) % per_b)),
                   pl.BlockSpec((NG, tm), lambda r, j: (0, tile_c(r)))],
        out_shape=[jax.ShapeDtypeStruct((N, P), BF16),
                   jax.ShapeDtypeStruct((B, TR, S), BF16),
                   jax.ShapeDtypeStruct((NG, N), F32)],
        scratch_shapes=[pltpu.VMEM((tm, D), F32),
                        pltpu.VMEM((2, tm, D), BF16),
                        pltpu.VMEM((2, tm, hd), F32),
                        pltpu.VMEM((2, tm, hd), F32),
                        pltpu.VMEM((2, PREP_SLABS, hd, tm // PREP_SLABS), F32),
                        pltpu.VMEM((2, PREP_SLABS, hd, tm // PREP_SLABS), F32),
                        pltpu.SemaphoreType.DMA(())],
        compiler_params=_params(("arbitrary", "arbitrary")),
        name="inproj",
    )(x2, mod3, g1, pos2, inv2, sgn, w_main, w_t)


_R, _MT, _A, _LIM, _E, _SC = range(6)
_NVEC = 6


def _log_sigmoid(x):
    return jnp.minimum(x, 0.0) - jnp.log1p(jnp.exp(-jnp.abs(x)))


def _lane_scan(x, reverse, op, ident):
    L = x.shape[1]
    lane = lax.broadcasted_iota(jnp.int32, x.shape, 1)
    sh = 1
    while sh < L:
        if reverse:
            x = op(x, jnp.where(lane < L - sh, pltpu.roll(x, L - sh, 1), ident))
        else:
            x = op(x, jnp.where(lane >= sh, pltpu.roll(x, sh, 1), ident))
        sh *= 2
    return x


def _chunk_scan(x, reverse, op, ident):
    n = x.shape[1] // LANES
    parts = [_lane_scan(x[:, i * LANES:(i + 1) * LANES], reverse, op, ident) for i in range(n)]
    order = range(n - 2, -1, -1) if reverse else range(1, n)
    for i in order:
        prev = parts[i + 1][:, 0:1] if reverse else parts[i - 1][:, LANES - 1:LANES]
        parts[i] = op(parts[i], prev)
    return jnp.concatenate(parts, axis=1)


def _gate_vectors(li, lf_pre, reverse, nc):
    rows, Lc = li.shape
    hb = rows // nc
    lf = _log_sigmoid(lf_pre)
    b = _chunk_scan(lf, reverse, jnp.add, 0.0)
    r = li - b
    rcm = _chunk_scan(r, reverse, jnp.maximum, NEG_INF)
    end = slice(0, 1) if reverse else slice(Lc - 1, Lc)
    bl = jnp.broadcast_to(b[:, end], (rows, Lc))
    rmax = jnp.broadcast_to(rcm[:, end], (rows, Lc))
    m = jnp.zeros((hb, Lc), F32)
    m_parts = [None] * nc
    for j in (range(nc - 1, -1, -1) if reverse else range(nc)):
        m_parts[j] = m
        sl = slice(j * hb, (j + 1) * hb)
        m = bl[sl, :] + jnp.maximum(m, rmax[sl, :])
    m_all = jnp.concatenate(m_parts, axis=0)
    mt = jnp.maximum(m_all, rcm)
    m_end = jnp.maximum(m_all, rmax)
    return r, mt, jnp.exp(m_all - mt), jnp.exp(-(b + mt)), jnp.exp(r - m_end), jnp.exp(m_all - m_end)


def _gatevec_kernel(g_ref, gb_ref, o_ref, *, nc):
    for dr in range(2):
        li = g_ref[2 * dr] + gb_ref[2 * dr]
        lf_pre = g_ref[2 * dr + 1] + gb_ref[2 * dr + 1]
        for idx, val in enumerate(_gate_vectors(li, lf_pre, dr == 1, nc)):
            o_ref[dr * _NVEC + idx] = val


def _gatevec(gates, gbias, nc):
    _, rows, Lc = gates.shape
    return pl.pallas_call(
        functools.partial(_gatevec_kernel, nc=nc),
        out_shape=jax.ShapeDtypeStruct((2 * _NVEC, rows, Lc), F32),
        compiler_params=pltpu.CompilerParams(vmem_limit_bytes=VMEM_LIMIT),
        name="gatevec",
    )(gates, gbias)


CONV_ROWS = 128


def _mlstm_kernel(q_ref, k_ref, vt_ref, mo_ref, vec_ref, wq_ref, wk_ref, bq_ref, bk_ref, go_ref,
                  o_ref, pad_scr, qs_scr, big_scr, vt_scr, ns_scr, ct_scr, n_scr, *, S, d, Lc, CW):
    nc = S // Lc
    half_w = CW // 2
    PADR = 8
    CR = CONV_ROWS
    N0 = Lc
    C0 = Lc + BF16_ROWS

    zero_rows = jnp.zeros((PADR, d), F32)

    def conv_silu(src_ref, w_ref, b_ref, store, scale):
        pad_scr[0:PADR, :] = zero_rows
        pad_scr[PADR + S:2 * PADR + S, :] = zero_rows

        def fill(j, c):
            r0 = pl.multiple_of(j * CR, CR)
            pad_scr[pl.ds(PADR + r0, CR), :] = src_ref[0, pl.ds(r0, CR), :].astype(F32)
            return c
        lax.fori_loop(0, S // CR, fill, 0)
        w = w_ref[...]
        bias = b_ref[...]

        def body(j, c):
            r0 = pl.multiple_of(j * CR, CR)
            win = pad_scr[pl.ds(r0, CR + 2 * PADR), :]
            y = bias
            for i in range(CW):
                o = PADR - half_w + i
                y = y + win[o:o + CR, :] * w[i:i + 1, :]
            y = y * _sigmoid(y)
            if scale != 1.0:
                y = y * scale
            store(j, r0, y.astype(BF16))
            return c
        lax.fori_loop(0, S // CR, body, 0)

    def store_q(j, r0, y):
        qs_scr[pl.ds(r0, CR), :] = y

    per = Lc // CR

    def store_k(j, r0, y):
        big_scr[j // per, pl.ds(pl.multiple_of((j % per) * CR, CR), CR), :] = y

    conv_silu(q_ref, wq_ref, bq_ref, store_q, 1.0)
    conv_silu(k_ref, wk_ref, bk_ref, store_k, d ** -0.5)

    for j in range(nc):
        vt_scr[j] = vt_ref[0, :, j * Lc:(j + 1) * Lc]

    ns_scr[...] = jnp.zeros(ns_scr.shape, F32)
    ct_scr[...] = jnp.zeros(ct_scr.shape, F32)
    n_scr[...] = jnp.zeros(n_scr.shape, F32)

    def state_step(dr, jc):
        base = dr * _NVEC
        e_row = vec_ref[0, 0, base + _E, pl.ds(jc, 1), :]
        sc = vec_ref[0, 0, base + _SC, pl.ds(jc, 1), :]
        lhs = jnp.concatenate([vt_scr[jc].astype(F32) * e_row, jnp.broadcast_to(e_row, (BF16_ROWS, Lc))], axis=0)
        upd = jnp.dot(lhs.astype(BF16), big_scr[jc, 0:Lc, :], preferred_element_type=F32)
        ct = ct_scr[dr]
        n = n_scr[dr]
        big_scr[jc, C0 + dr * d:C0 + (dr + 1) * d, :] = ct.astype(BF16)
        ns_scr[jc, dr:dr + 1, :] = n[0:1, :]
        ct_scr[dr] = sc * ct + upd[0:d, :]
        n_scr[dr] = sc * n + upd[d:d + 8, :]

    def phase1(j, c):
        state_step(0, j)
        state_step(1, nc - 1 - j)
        return c
    lax.fori_loop(0, nc, phase1, 0)

    def put_n(j, c):
        big_scr[j, N0:N0 + BF16_ROWS, :] = ns_scr[j].astype(BF16)
        return c
    lax.fori_loop(0, nc, put_n, 0)

    rr = lax.broadcasted_iota(jnp.int32, (Lc, Lc), 0)
    cc = lax.broadcasted_iota(jnp.int32, (Lc, Lc), 1)
    r1 = lax.broadcasted_iota(jnp.int32, (LANES, LANES), 0)
    c1 = lax.broadcasted_iota(jnp.int32, (LANES, LANES), 1)
    eye = r1 == c1
    gout = go_ref[...]

    def to_col(row):
        return jnp.concatenate(
            [jnp.sum(jnp.where(eye, row[:, i * LANES:(i + 1) * LANES], 0.0), axis=1, keepdims=True)
             for i in range(Lc // LANES)], axis=0)

    def phase2(j, c):
        r0 = pl.multiple_of(j * Lc, Lc)
        qj = qs_scr[pl.ds(r0, Lc), :]
        hrows = C0 + d
        p = jnp.concatenate(
            [lax.dot_general(big_scr[j, 0:hrows, :], qj, _NT, preferred_element_type=F32),
             lax.dot_general(big_scr[j, hrows:hrows + d, :], qj, _NT, preferred_element_type=F32)], axis=0)
        st = p[0:Lc, :]
        qkw = []
        inv_den = []
        a_rows = []
        for dr in range(2):
            base = dr * _NVEC
            r_row = vec_ref[0, 0, base + _R, pl.ds(j, 1), :]
            mt_row = vec_ref[0, 0, base + _MT, pl.ds(j, 1), :]
            a_row = vec_ref[0, 0, base + _A, pl.ds(j, 1), :]
            lim_row = vec_ref[0, 0, base + _LIM, pl.ds(j, 1), :]
            valid = (rr <= cc) if dr == 0 else (rr >= cc)
            w = jnp.exp(jnp.where(valid, to_col(r_row) - mt_row, NEG_INF))
            qkw_d = st * w
            den = a_row * p[N0 + dr:N0 + dr + 1, :] + jnp.sum(qkw_d, axis=0, keepdims=True)
            inv_den.append(1.0 / jnp.maximum(jnp.abs(den), lim_row))
            a_rows.append(a_row)
            qkw.append(qkw_d.astype(BF16))
        intra = jnp.dot(vt_scr[j], jnp.concatenate(qkw, axis=1), preferred_element_type=F32)
        hs = None
        for dr in range(2):
            inter = p[C0 + dr * d:C0 + (dr + 1) * d, :]
            hd = (a_rows[dr] * inter + intra[:, dr * Lc:(dr + 1) * Lc]) * inv_den[dr]
            hs = hd if hs is None else hs + hd
        mu = jnp.mean(hs, axis=0, keepdims=True)
        hc = hs - mu
        hn = (hc * lax.rsqrt(jnp.mean(hc * hc, axis=0, keepdims=True) + EPS)).T
        mo = mo_ref[0, pl.ds(r0, Lc), :].astype(F32)
        o_ref[0, pl.ds(r0, Lc), :] = (_sigmoid(mo) * hn * gout).astype(BF16)
        return c
    lax.fori_loop(0, nc, phase2, 0)


def _mlstm(proj3, vt3, vec5, w_conv, b_conv, g_out):
    B, S, _ = proj3.shape
    H = MLSTM_HEADS
    d = g_out.shape[1] // H
    Lc = MLSTM_CHUNK
    nc = S // Lc
    CW = w_conv.shape[0]
    assert Lc % LANES == 0 and Lc % CONV_ROWS == 0 and d == Lc and CW // 2 <= 8
    kern = functools.partial(_mlstm_kernel, S=S, d=d, Lc=Lc, CW=CW)
    col = lambda off: (lambda b, h: (b, 0, off + h))
    return pl.pallas_call(
        kern,
        grid=(B, H),
        in_specs=[pl.BlockSpec((1, S, d), col(0)),
                  pl.BlockSpec((1, S, d), col(H)),
                  pl.BlockSpec((1, d, S), lambda b, h: (b, h, 0)),
                  pl.BlockSpec((1, S, d), col(2 * H)),
                  pl.BlockSpec((1, 1, 2 * _NVEC, nc, Lc), lambda b, h: (b, h, 0, 0, 0)),
                  pl.BlockSpec((CW, d), lambda b, h: (0, h)),
                  pl.BlockSpec((CW, d), lambda b, h: (0, H + h)),
                  pl.BlockSpec((1, d), lambda b, h: (0, h)),
                  pl.BlockSpec((1, d), lambda b, h: (0, H + h)),
                  pl.BlockSpec((1, d), lambda b, h: (0, h))],
        out_specs=pl.BlockSpec((1, S, d), lambda b, h: (b, 0, h)),
        out_shape=jax.ShapeDtypeStruct((B, S, H * d), BF16),
        scratch_shapes=[pltpu.VMEM((S + 16, d), F32),
                        pltpu.VMEM((S, d), BF16),
                        pltpu.VMEM((nc, Lc + BF16_ROWS + 2 * d, d), BF16),
                        pltpu.VMEM((nc, d, Lc), BF16),
                        pltpu.VMEM((nc, BF16_ROWS, d), F32),
                        pltpu.VMEM((2, d, d), F32),
                        pltpu.VMEM((2, 8, d), F32)],
        compiler_params=_params(("arbitrary", "arbitrary")),
        name="mlstm",
    )(proj3, proj3, vt3, proj3, vec5, w_conv, w_conv, b_conv, b_conv, g_out)


LOG2E = 1.4426950408889634


def _attn_kernel(sink_ref, q_ref, ktp_ref, ktc_ref, ktn_ref, vtp_ref, vtc_ref, vtn_ref, g_ref, o_ref, *, H, G):
    n = pl.program_id(1)
    nb = pl.num_programs(1)
    blk = ATTN_BLOCK
    hd = ATTN_HEAD_DIM
    R = H // G
    c1 = (hd ** -0.5) * LOG2E
    rr = lax.broadcasted_iota(jnp.int32, (blk, blk), 0)
    cc = lax.broadcasted_iota(jnp.int32, (blk, blk), 1)
    m_prev = jnp.logical_and(cc >= rr, n > 0)
    m_next = jnp.logical_and(cc <= rr, n < nb - 1)
    q = q_ref[0]
    ones = jnp.ones((hd, 3 * blk), BF16)
    outs = []
    for g in range(G):
        rs = slice(g * hd, (g + 1) * hd)
        kt = jnp.concatenate([ktp_ref[0][rs, :], ktc_ref[0][rs, :], ktn_ref[0][rs, :]], axis=1)
        vt = jnp.concatenate([vtp_ref[0][rs, :], vtc_ref[0][rs, :], vtn_ref[0][rs, :]], axis=1)
        vt_aug = jnp.concatenate([vt, ones], axis=0)
        qg = jnp.concatenate([q[:, (g * R + r) * hd:(g * R + r + 1) * hd] for r in range(R)], axis=0)
        s = jnp.dot(qg, kt, preferred_element_type=F32)
        ps = []
        extras = []
        for r in range(R):
            sr = s[r * blk:(r + 1) * blk, :] * c1
            t0 = jnp.where(m_prev, sr[:, 0:blk], NEG_INF)
            t1 = sr[:, blk:2 * blk]
            t2 = jnp.where(m_next, sr[:, 2 * blk:3 * blk], NEG_INF)
            sink2 = sink_ref[g * R + r] * LOG2E
            m = jnp.maximum(jnp.max(jnp.maximum(jnp.maximum(t0, t1), t2), axis=1, keepdims=True), sink2)
            ps.append(jnp.concatenate([jnp.exp2(t0 - m), jnp.exp2(t1 - m), jnp.exp2(t2 - m)], axis=1).astype(BF16))
            extras.append(jnp.exp2(sink2 - m))
        oa = lax.dot_general(jnp.concatenate(ps, axis=0), vt_aug, _NT, preferred_element_type=F32)
        for r in range(R):
            o_r = oa[r * blk:(r + 1) * blk, :]
            outs.append(o_r[:, 0:hd] / (o_r[:, hd:2 * hd] + extras[r]))
    oall = jnp.concatenate(outs, axis=1)
    y = oall * lax.rsqrt(jnp.mean(oall * oall, axis=-1, keepdims=True) + EPS) * g_ref[...]
    o_ref[0] = y.astype(BF16)


def _attn(proj3, t3, sink, g_attn, q_col0, vt_row0, kt_row0):
    B, S, _ = proj3.shape
    blk = ATTN_BLOCK
    nb = S // blk
    H = sink.shape[0]
    G = ATTN_KV_HEADS
    AW = H * ATTN_HEAD_DIM
    KW = G * ATTN_HEAD_DIM
    assert WINDOW == blk and q_col0 % AW == 0 and vt_row0 % KW == 0 and kt_row0 % KW == 0
    qb, vb, kb = q_col0 // AW, vt_row0 // KW, kt_row0 // KW
    kern = functools.partial(_attn_kernel, H=H, G=G)
    prev = lambda rb: (lambda b, n: (b, rb, jnp.maximum(n - 1, 0)))
    cur = lambda rb: (lambda b, n: (b, rb, n))
    nxt = lambda rb: (lambda b, n: (b, rb, jnp.minimum(n + 1, nb - 1)))
    return pl.pallas_call(
        kern,
        grid=(B, nb),
        in_specs=[pl.BlockSpec(memory_space=pltpu.SMEM),
                  pl.BlockSpec((1, blk, AW), lambda b, n: (b, n, qb)),
                  pl.BlockSpec((1, KW, blk), prev(kb)),
                  pl.BlockSpec((1, KW, blk), cur(kb)),
                  pl.BlockSpec((1, KW, blk), nxt(kb)),
                  pl.BlockSpec((1, KW, blk), prev(vb)),
                  pl.BlockSpec((1, KW, blk), cur(vb)),
                  pl.BlockSpec((1, KW, blk), nxt(vb)),
                  pl.BlockSpec((1, AW), lambda b, n: (0, 0))],
        out_specs=pl.BlockSpec((1, blk, AW), lambda b, n: (b, n, 0)),
        out_shape=jax.ShapeDtypeStruct((B, S, AW), BF16),
        compiler_params=_params(("arbitrary", "arbitrary")),
        name="attn",
    )(sink, proj3, t3, t3, t3, t3, t3, t3, g_attn)


def _outproj_kernel(m_ref, a_ref, x_ref, mod_ref, g2_ref, wt_ref, wb_ref, o_ref, h_ref, *, D):
    acc = jnp.dot(m_ref[...], wt_ref[...], preferred_element_type=F32)
    acc = acc + jnp.dot(a_ref[...], wb_ref[...], preferred_element_type=F32)
    gt = mod_ref[0, :, 2 * D:3 * D]
    x1 = x_ref[...] + gt * acc
    o_ref[...] = x1
    sh = mod_ref[0, :, 3 * D:4 * D]
    gsc = g2_ref[...] * (1.0 + mod_ref[0, :, 4 * D:5 * D])
    h_ref[...] = (x1 * lax.rsqrt(jnp.mean(x1 * x1, axis=-1, keepdims=True) + EPS) * gsc + sh).astype(BF16)


def _outproj(m2, a2, x2, mod3, g2, w_o, S, tm=512):
    N, D = x2.shape
    K = m2.shape[1]
    per_b = S // tm
    return pl.pallas_call(
        functools.partial(_outproj_kernel, D=D),
        grid=(N // tm,),
        in_specs=[pl.BlockSpec((tm, K), lambda i: (i, 0)),
                  pl.BlockSpec((tm, K), lambda i: (i, 0)),
                  pl.BlockSpec((tm, D), lambda i: (i, 0)),
                  pl.BlockSpec((1, 1, mod3.shape[2]), lambda i: (i // per_b, 0, 0)),
                  pl.BlockSpec((1, D), lambda i: (0, 0)),
                  pl.BlockSpec((K, D), lambda i: (0, 0)),
                  pl.BlockSpec((K, D), lambda i: (1, 0))],
        out_specs=[pl.BlockSpec((tm, D), lambda i: (i, 0)),
                   pl.BlockSpec((tm, D), lambda i: (i, 0))],
        out_shape=[jax.ShapeDtypeStruct((N, D), F32),
                   jax.ShapeDtypeStruct((N, D), BF16)],
        compiler_params=_params(("arbitrary",)),
        name="outproj",
    )(m2, a2, x2, mod3, g2, w_o, w_o)


def _ffn_up_kernel(h_ref, wg_ref, wu_ref, o_ref):
    hb = h_ref[...]
    gate = jnp.dot(hb, wg_ref[...].astype(BF16), preferred_element_type=F32)
    up = jnp.dot(hb, wu_ref[...].astype(BF16), preferred_element_type=F32)
    o_ref[...] = (gate * _sigmoid(gate) * up).astype(BF16)


def _ffn_up(h2, w_gate, w_up, tm=2048, tf=512):
    N, D = h2.shape
    FF = w_gate.shape[1]
    return pl.pallas_call(
        _ffn_up_kernel,
        grid=(N // tm, FF // tf),
        in_specs=[pl.BlockSpec((tm, D), lambda i, j: (i, 0)),
                  pl.BlockSpec((D, tf), lambda i, j: (0, j)),
                  pl.BlockSpec((D, tf), lambda i, j: (0, j))],
        out_specs=pl.BlockSpec((tm, tf), lambda i, j: (i, j)),
        out_shape=jax.ShapeDtypeStruct((N, FF), BF16),
        compiler_params=_params(("arbitrary", "arbitrary")),
        name="ffn_up",
    )(h2, w_gate, w_up)


def _ffn_down_kernel(u_ref, x_hbm, mod_ref, gf_ref, wd_ref, o_ref, xbuf, sem, *, D, tm):
    i = pl.program_id(0)
    j = pl.program_id(1)

    def x_copy():
        return pltpu.make_async_copy(x_hbm.at[pl.ds(pl.multiple_of(i * tm, tm), tm), :], xbuf, sem)

    @pl.when(j == 0)
    def _():
        x_copy().start()
        o_ref[...] = jnp.zeros(o_ref.shape, F32)

    o_ref[...] += jnp.dot(u_ref[...], wd_ref[...], preferred_element_type=F32)

    @pl.when(j == pl.num_programs(1) - 1)
    def _():
        x_copy().wait()
        gt = mod_ref[0, :, 5 * D:6 * D]
        y = xbuf[...] + gt * o_ref[...]
        o_ref[...] = y * lax.rsqrt(jnp.mean(y * y, axis=-1, keepdims=True) + EPS) * gf_ref[...]


def _ffn_down(u2, x2, mod3, gf, w_down, S, tm=1024, tk=1408):
    N, D = x2.shape
    FF = u2.shape[1]
    per_b = S // tm
    return pl.pallas_call(
        functools.partial(_ffn_down_kernel, D=D, tm=tm),
        grid=(N // tm, FF // tk),
        in_specs=[pl.BlockSpec((tm, tk), lambda i, j: (i, j)),
                  pl.BlockSpec(memory_space=pl.ANY),
                  pl.BlockSpec((1, 1, mod3.shape[2]), lambda i, j: (i // per_b, 0, 0)),
                  pl.BlockSpec((1, D), lambda i, j: (0, 0)),
                  pl.BlockSpec((tk, D), lambda i, j: (j, 0))],
        out_specs=pl.BlockSpec((tm, D), lambda i, j: (i, 0)),
        out_shape=jax.ShapeDtypeStruct((N, D), F32),
        scratch_shapes=[pltpu.VMEM((tm, D), F32), pltpu.SemaphoreType.DMA(())],
        compiler_params=_params(("arbitrary", "arbitrary")),
        name="ffn_down",
    )(u2, x2, mod3, gf, w_down)


def kernel(x, c, positions, w_ada, b_ada, g_norm1, g_norm2, w_in, b_gates, w_conv, b_conv, g_mlstm_out,
           sink, g_attn_out, w_out, w_gate, w_up, w_down, g_final):
    B, S, D = x.shape
    depth = w_ada.shape[0]
    H = MLSTM_HEADS
    MW = D // 2
    KW = ATTN_KV_HEADS * ATTN_HEAD_DIM
    NG = 4 * H
    L = MLSTM_CHUNK
    nc = S // L
    g0 = 4 * MW
    x2 = x.reshape(B * S, D)
    pos2 = positions.reshape(B * S, 1)
    for l in range(depth):
        mod3 = _ada(c, w_ada[l], b_ada[l]).reshape(B, 1, 6 * D)
        wl = w_in[l].astype(BF16)
        a0 = g0 + NG
        w_main = jnp.concatenate([wl[:, :2 * MW], wl[:, 3 * MW:g0], wl[:, a0:a0 + MW]], axis=1)
        t_rows = MW + 2 * KW + NG
        t_pad = -t_rows % (2 * BF16_ROWS)
        w_t = jnp.concatenate([wl[:, 2 * MW:3 * MW], wl[:, a0 + MW + KW:], wl[:, a0 + MW:a0 + MW + KW],
                               wl[:, g0:a0], jnp.zeros((D, t_pad), BF16)], axis=1).T
        proj, t3, gates_t = _inproj(x2, mod3, g_norm1[l].reshape(1, D), pos2, w_main, w_t, B, S, NG)
        proj3 = proj.reshape(B, S, proj.shape[1])
        gates = gates_t.reshape(4, H, B, nc, L).transpose(0, 3, 1, 2, 4).reshape(4, nc * H * B, L)
        gbias = jnp.broadcast_to(b_gates[l].reshape(4, 1, H, 1), (4, nc, H, B)).reshape(4, nc * H * B, 1)
        vec = _gatevec(gates, gbias, nc)
        vec5 = vec.reshape(2 * _NVEC, nc, H, B, L).transpose(3, 2, 0, 1, 4)
        m_out = _mlstm(proj3, t3, vec5, w_conv[l], b_conv[l].reshape(1, 2 * MW), g_mlstm_out[l].reshape(1, MW))
        a_out = _attn(proj3, t3, sink[l], g_attn_out[l].reshape(1, MW), q_col0=3 * MW,
                      vt_row0=MW, kt_row0=MW + KW)
        x2, h2 = _outproj(m_out.reshape(B * S, MW), a_out.reshape(B * S, MW), x2, mod3,
                          g_norm2[l].reshape(1, D), w_out[l].astype(BF16), S)
        assert l == depth - 1, "final norm is fused into the last layer's FFN"
        u2 = _ffn_up(h2, w_gate[l], w_up[l])
        x2 = _ffn_down(u2, x2, mod3, g_final.reshape(1, D), w_down[l].astype(BF16), S)
    return x2.reshape(B, S, D)
```

```python
import functools

import jax
import jax.numpy as jnp
from jax import lax
from jax.experimental import pallas as pl
from jax.experimental.pallas import tpu as pltpu

F32 = jnp.float32
BF16 = jnp.bfloat16

EPS = 1e-6
NEG_INF = -1e30
ROPE_THETA = 10000.0

MLSTM_HEADS = 4
MLSTM_CHUNK = 256
ATTN_HEAD_DIM = 128
ATTN_KV_HEADS = 2
WINDOW = 128
ATTN_BLOCK = 128

LANES = 128
BF16_ROWS = 16
VMEM_LIMIT = 56 * 1024 * 1024

_NT = (((1,), (1,)), ((), ()))


def _sigmoid(x):
    return 1.0 / (1.0 + jnp.exp(-x))


def _params(sem, vmem=VMEM_LIMIT):
    return pltpu.CompilerParams(dimension_semantics=sem, vmem_limit_bytes=vmem)


def _split_bf16(v):
    hi = v.astype(BF16)
    return hi, (v - hi.astype(F32)).astype(BF16)


def _ada_kernel(c_ref, w_ref, b_ref, o_ref):
    c = c_ref[...]
    s_hi, s_lo = _split_bf16(c * _sigmoid(c))
    w_hi, w_lo = _split_bf16(w_ref[...])
    dot = functools.partial(jnp.dot, preferred_element_type=F32)
    o_ref[...] = dot(s_hi, w_hi) + (dot(s_lo, w_hi) + dot(s_hi, w_lo)) + b_ref[...]


def _ada(c, w_ada, b_ada, tn=1024):
    B, D = c.shape
    N = w_ada.shape[1]
    return pl.pallas_call(
        _ada_kernel,
        grid=(N // tn,),
        in_specs=[pl.BlockSpec((B, D), lambda j: (0, 0)),
                  pl.BlockSpec((D, tn), lambda j: (0, j)),
                  pl.BlockSpec((1, tn), lambda j: (0, j))],
        out_specs=pl.BlockSpec((B, tn), lambda j: (0, j)),
        out_shape=jax.ShapeDtypeStruct((B, N), F32),
        compiler_params=_params(("arbitrary",)),
        name="ada",
    )(c, w_ada, b_ada.reshape(1, N))


def _rope(t, cos2, sin2):
    return t * cos2 + pltpu.roll(t, ATTN_HEAD_DIM // 2, 1) * sin2


PREP_SLABS = 4
PREP_STEP0 = 2


def _inproj_kernel(x_hbm, mod_ref, g_ref, pos_ref, inv_ref, sgn_ref, w_ref, wt_ref,
                   o_ref, t_ref, gt_ref, xbuf, h_scr, cos_scr, sin_scr, cost_scr, sint_scr, sem,
                   *, D, tm, n_tiles, n_main, rope_j0, n_t, ak_rows, gate_rows):
    r = pl.program_id(0)
    j = pl.program_id(1)
    hd = ATTN_HEAD_DIM
    rows = tm // PREP_SLABS
    ps = r % 2
    cs = 1 - ps
    tile_p = jnp.minimum(r, n_tiles - 1)

    def x_copy():
        return pltpu.make_async_copy(x_hbm.at[pl.ds(pl.multiple_of(tile_p * tm, tm), tm), :], xbuf, sem)

    @pl.when(j == 0)
    def _():
        x_copy().start()

    @pl.when(j == PREP_STEP0 - 1)
    def _():
        x_copy().wait()

    def prep_slab():
        k = j - PREP_STEP0
        r0 = pl.multiple_of(k * rows, rows)
        x = xbuf[pl.ds(r0, rows), :]
        sh = mod_ref[0, :, 0:D]
        gsc = g_ref[...] * (1.0 + mod_ref[0, :, D:2 * D])
        h_scr[ps, pl.ds(r0, rows), :] = (
            x * lax.rsqrt(jnp.mean(x * x, axis=-1, keepdims=True) + EPS) * gsc + sh).astype(BF16)
        ang = pos_ref[pl.ds(r0, rows), :].astype(F32) * inv_ref[...]
        cos2 = jnp.cos(ang)
        sin2 = jnp.sin(ang) * sgn_ref[...]
        cos_scr[ps, pl.ds(r0, rows), :] = cos2
        sin_scr[ps, pl.ds(r0, rows), :] = sin2
        cost_scr[ps, k] = cos2.T
        sint_scr[ps, k] = sin2.T

    def plain():
        o_ref[...] = jnp.dot(h_scr[cs], w_ref[...], preferred_element_type=F32).astype(BF16)

    def roped():
        acc = jnp.dot(h_scr[cs], w_ref[...], preferred_element_type=F32)
        cos2 = cos_scr[cs]
        sin2 = sin_scr[cs]
        parts = [_rope(acc[:, hh * hd:(hh + 1) * hd], cos2, sin2) for hh in range(acc.shape[1] // hd)]
        o_ref[...] = jnp.concatenate(parts, axis=1).astype(BF16)

    tt = t_ref.shape[1]

    def transposed(ti):
        acc = lax.dot_general(wt_ref[...], h_scr[cs], _NT, preferred_element_type=F32)
        lo, hi = ti * tt, (ti + 1) * tt
        pieces = []
        cur = lo
        for a0 in range(ak_rows[0], ak_rows[1], hd):
            if a0 < lo or a0 + hd > hi:
                assert a0 + hd <= lo or a0 >= hi, "a k head straddles two transposed tiles"
                continue
            if a0 > cur:
                pieces.append(acc[cur - lo:a0 - lo, :])
            t = acc[a0 - lo:a0 - lo + hd, :]
            rot = jnp.concatenate([t[hd // 2:, :], t[:hd // 2, :]], axis=0)
            cost = jnp.concatenate([cost_scr[cs, k] for k in range(PREP_SLABS)], axis=1)
            sint = jnp.concatenate([sint_scr[cs, k] for k in range(PREP_SLABS)], axis=1)
            pieces.append(t * cost + rot * sint)
            cur = a0 + hd
        if cur < hi:
            pieces.append(acc[cur - lo:, :])
        t_ref[0] = jnp.concatenate(pieces, axis=0).astype(BF16) if len(pieces) > 1 else acc.astype(BF16)
        if lo <= gate_rows[0] and gate_rows[1] <= hi:
            gt_ref[...] = acc[gate_rows[0] - lo:gate_rows[1] - lo, :]

    mult = r > 0
    both = jnp.logical_and

    @pl.when(both(mult, j < PREP_STEP0))
    def _():
        plain()

    @pl.when(both(mult, both(j >= PREP_STEP0, j < rope_j0)))
    def _():
        plain()
        prep_slab()

    @pl.when(both(mult, both(j >= rope_j0, j < n_main)))
    def _():
        roped()
        prep_slab()

    for ti in range(n_t):
        @pl.when(both(mult, j == n_main + ti))
        def _(ti=ti):
            transposed(ti)
            prep_slab()

    @pl.when(both(jnp.logical_not(mult), j >= PREP_STEP0))
    def _():
        prep_slab()


def _inproj(x2, mod3, g1, pos2, w_main, w_t, B, S, NG, tm=1024, tn=1024, n_t=2):
    N, D = x2.shape
    P = w_main.shape[1]
    TR = w_t.shape[0]
    hd = ATTN_HEAD_DIM
    half = hd // 2
    inv = ROPE_THETA ** (-jnp.arange(half, dtype=F32) * 2.0 / hd)
    inv2 = jnp.concatenate([inv, inv]).reshape(1, hd)
    sgn = jnp.concatenate([-jnp.ones((half,), F32), jnp.ones((half,), F32)]).reshape(1, hd)
    MW = D // 2
    KW = ATTN_KV_HEADS * hd
    aq0 = 3 * MW
    ak_rows = (MW + KW, MW + 2 * KW)
    gate_rows = (MW + 2 * KW, MW + 2 * KW + NG)
    tt = TR // n_t
    n_tiles = N // tm
    n_main = P // tn
    n_j = n_main + n_t
    rope_j0 = aq0 // tn
    assert aq0 % tn == 0 and P == aq0 + MW and P % tn == 0 and TR % n_t == 0 and tt % BF16_ROWS == 0
    assert gate_rows[1] <= TR and tm % PREP_SLABS == 0 and n_j == PREP_STEP0 + PREP_SLABS
    assert PREP_STEP0 <= rope_j0
    kern = functools.partial(_inproj_kernel, D=D, tm=tm, n_tiles=n_tiles, n_main=n_main, rope_j0=rope_j0,
                             n_t=n_t, ak_rows=ak_rows, gate_rows=gate_rows)
    per_b = S // tm
    tile_p = lambda r: jnp.minimum(r, n_tiles - 1)
    tile_c = lambda r: jnp.maximum(r - 1, 0)
    live = lambda r: jnp.minimum(r, 1)
    main_j = lambda r, j: jnp.minimum(j, n_main - 1) * live(r)
    t_j = lambda r, j: jnp.maximum(j - n_main, 0) * live(r)
    return pl.pallas_call(
        kern,
        grid=(n_tiles + 1, n_j),
        in_specs=[pl.BlockSpec(memory_space=pl.ANY),
                  pl.BlockSpec((1, 1, mod3.shape[2]), lambda r, j: (tile_p(r) // per_b, 0, 0)),
                  pl.BlockSpec((1, D), lambda r, j: (0, 0)),
                  pl.BlockSpec((tm, 1), lambda r, j: (tile_p(r), 0)),
                  pl.BlockSpec((1, hd), lambda r, j: (0, 0)),
                  pl.BlockSpec((1, hd), lambda r, j: (0, 0)),
                  pl.BlockSpec((D, tn), lambda r, j: (0, jnp.minimum(j, n_main - 1))),
                  pl.BlockSpec((tt, D), lambda r, j: (jnp.maximum(j - n_main, 0), 0))],
        out_specs=[pl.BlockSpec((tm, tn), lambda r, j: (tile_c(r), main_j(r, j))),
                   pl.BlockSpec((1, tt, tm), lambda r, j: (tile_c(r) // per_b, t_j(r, j), tile_c(r) % per_b)),
                   pl.BlockSpec((NG, tm), lambda r, j: (0, tile_c(r)))],
        out_shape=[jax.ShapeDtypeStruct((N, P), BF16),
                   jax.ShapeDtypeStruct((B, TR, S), BF16),
                   jax.ShapeDtypeStruct((NG, N), F32)],
        scratch_shapes=[pltpu.VMEM((tm, D), F32),
                        pltpu.VMEM((2, tm, D), BF16),
                        pltpu.VMEM((2, tm, hd), F32),
                        pltpu.VMEM((2, tm, hd), F32),
                        pltpu.VMEM((2, PREP_SLABS, hd, tm // PREP_SLABS), F32),
                        pltpu.VMEM((2, PREP_SLABS, hd, tm // PREP_SLABS), F32),
                        pltpu.SemaphoreType.DMA(())],
        compiler_params=_params(("arbitrary", "arbitrary")),
        name="inproj",
    )(x2, mod3, g1, pos2, inv2, sgn, w_main, w_t)


_R, _MT, _A, _LIM, _E, _SC = range(6)
_NVEC = 6


def _log_sigmoid(x):
    return jnp.minimum(x, 0.0) - jnp.log1p(jnp.exp(-jnp.abs(x)))


def _lane_scan(x, reverse, op, ident):
    L = x.shape[1]
    lane = lax.broadcasted_iota(jnp.int32, x.shape, 1)
    sh = 1
    while sh < L:
        if reverse:
            x = op(x, jnp.where(lane < L - sh, pltpu.roll(x, L - sh, 1), ident))
        else:
            x = op(x, jnp.where(lane >= sh, pltpu.roll(x, sh, 1), ident))
        sh *= 2
    return x


def _chunk_scan(x, reverse, op, ident):
    n = x.shape[1] // LANES
    parts = [_lane_scan(x[:, i * LANES:(i + 1) * LANES], reverse, op, ident) for i in range(n)]
    order = range(n - 2, -1, -1) if reverse else range(1, n)
    for i in order:
        prev = parts[i + 1][:, 0:1] if reverse else parts[i - 1][:, LANES - 1:LANES]
        parts[i] = op(parts[i], prev)
    return jnp.concatenate(parts, axis=1)


def _gate_vectors(li, lf_pre, reverse, nc):
    rows, Lc = li.shape
    hb = rows // nc
    lf = _log_sigmoid(lf_pre)
    b = _chunk_scan(lf, reverse, jnp.add, 0.0)
    r = li - b
    rcm = _chunk_scan(r, reverse, jnp.maximum, NEG_INF)
    end = slice(0, 1) if reverse else slice(Lc - 1, Lc)
    bl = jnp.broadcast_to(b[:, end], (rows, Lc))
    rmax = jnp.broadcast_to(rcm[:, end], (rows, Lc))
    m = jnp.zeros((hb, Lc), F32)
    m_parts = [None] * nc
    for j in (range(nc - 1, -1, -1) if reverse else range(nc)):
        m_parts[j] = m
        sl = slice(j * hb, (j + 1) * hb)
        m = bl[sl, :] + jnp.maximum(m, rmax[sl, :])
    m_all = jnp.concatenate(m_parts, axis=0)
    mt = jnp.maximum(m_all, rcm)
    m_end = jnp.maximum(m_all, rmax)
    return r, mt, jnp.exp(m_all - mt), jnp.exp(-(b + mt)), jnp.exp(r - m_end), jnp.exp(m_all - m_end)


def _gatevec_kernel(g_ref, gb_ref, o_ref, *, nc):
    for dr in range(2):
        li = g_ref[2 * dr] + gb_ref[2 * dr]
        lf_pre = g_ref[2 * dr + 1] + gb_ref[2 * dr + 1]
        for idx, val in enumerate(_gate_vectors(li, lf_pre, dr == 1, nc)):
            o_ref[dr * _NVEC + idx] = val


def _gatevec(gates, gbias, nc):
    _, rows, Lc = gates.shape
    return pl.pallas_call(
        functools.partial(_gatevec_kernel, nc=nc),
        out_shape=jax.ShapeDtypeStruct((2 * _NVEC, rows, Lc), F32),
        compiler_params=pltpu.CompilerParams(vmem_limit_bytes=VMEM_LIMIT),
        name="gatevec",
    )(gates, gbias)


CONV_ROWS = 128


def _mlstm_kernel(q_ref, k_ref, vt_ref, mo_ref, vec_ref, wq_ref, wk_ref, bq_ref, bk_ref, go_ref,
                  o_ref, pad_scr, qs_scr, big_scr, vt_scr, ns_scr, ct_scr, n_scr, p0_scr, p1_scr, *, S, d, Lc, CW):
    nc = S // Lc
    half_w = CW // 2
    PADR = 8
    CR = CONV_ROWS
    N0 = Lc
    C0 = Lc + BF16_ROWS

    zero_rows = jnp.zeros((PADR, d), F32)

    def conv_silu(src_ref, w_ref, b_ref, store, scale):
        pad_scr[0:PADR, :] = zero_rows
        pad_scr[PADR + S:2 * PADR + S, :] = zero_rows

        def fill(j, c):
            r0 = pl.multiple_of(j * CR, CR)
            pad_scr[pl.ds(PADR + r0, CR), :] = src_ref[0, pl.ds(r0, CR), :].astype(F32)
            return c
        lax.fori_loop(0, S // CR, fill, 0)
        w = w_ref[...]
        bias = b_ref[...]

        def body(j, c):
            r0 = pl.multiple_of(j * CR, CR)
            win = pad_scr[pl.ds(r0, CR + 2 * PADR), :]
            y = bias
            for i in range(CW):
                o = PADR - half_w + i
                y = y + win[o:o + CR, :] * w[i:i + 1, :]
            y = y * _sigmoid(y)
            if scale != 1.0:
                y = y * scale
            store(j, r0, y.astype(BF16))
            return c
        lax.fori_loop(0, S // CR, body, 0)

    def store_q(j, r0, y):
        qs_scr[pl.ds(r0, CR), :] = y

    per = Lc // CR

    def store_k(j, r0, y):
        big_scr[j // per, pl.ds(pl.multiple_of((j % per) * CR, CR), CR), :] = y

    conv_silu(q_ref, wq_ref, bq_ref, store_q, 1.0)
    conv_silu(k_ref, wk_ref, bk_ref, store_k, d ** -0.5)

    for j in range(nc):
        vt_scr[j] = vt_ref[0, :, j * Lc:(j + 1) * Lc]

    ns_scr[...] = jnp.zeros(ns_scr.shape, F32)
    ct_scr[...] = jnp.zeros(ct_scr.shape, F32)
    n_scr[...] = jnp.zeros(n_scr.shape, F32)

    def state_step(dr, jc):
        base = dr * _NVEC
        e_row = vec_ref[0, 0, base + _E, pl.ds(jc, 1), :]
        sc = vec_ref[0, 0, base + _SC, pl.ds(jc, 1), :]
        lhs = jnp.concatenate([vt_scr[jc].astype(F32) * e_row, jnp.broadcast_to(e_row, (BF16_ROWS, Lc))], axis=0)
        upd = jnp.dot(lhs.astype(BF16), big_scr[jc, 0:Lc, :], preferred_element_type=F32)
        ct = ct_scr[dr]
        n = n_scr[dr]
        big_scr[jc, C0 + dr * d:C0 + (dr + 1) * d, :] = ct.astype(BF16)
        ns_scr[jc, dr:dr + 1, :] = n[0:1, :]
        ct_scr[dr] = sc * ct + upd[0:d, :]
        n_scr[dr] = sc * n + upd[d:d + 8, :]

    for j in range(nc):
        state_step(0, j)
        state_step(1, nc - 1 - j)
    for j in range(nc):
        big_scr[j, N0:N0 + BF16_ROWS, :] = ns_scr[j].astype(BF16)

    rr = lax.broadcasted_iota(jnp.int32, (Lc, Lc), 0)
    cc = lax.broadcasted_iota(jnp.int32, (Lc, Lc), 1)
    r1 = lax.broadcasted_iota(jnp.int32, (LANES, LANES), 0)
    c1 = lax.broadcasted_iota(jnp.int32, (LANES, LANES), 1)
    eye = r1 == c1
    gout = go_ref[...]

    def to_col(row):
        return jnp.concatenate(
            [jnp.sum(jnp.where(eye, row[:, i * LANES:(i + 1) * LANES], 0.0), axis=1, keepdims=True)
             for i in range(Lc // LANES)], axis=0)

    hrows = C0 + d

    def head(j, p_scr):
        qj = qs_scr[pl.ds(pl.multiple_of(j * Lc, Lc), Lc), :]
        p_scr[0:hrows, :] = lax.dot_general(big_scr[j, 0:hrows, :], qj, _NT, preferred_element_type=F32)
        p_scr[hrows:hrows + d, :] = lax.dot_general(big_scr[j, hrows:hrows + d, :], qj, _NT,
                                                    preferred_element_type=F32)

    def tail(j, p_scr):
        r0 = pl.multiple_of(j * Lc, Lc)
        st = p_scr[0:Lc, :]
        qkw = []
        inv_den = []
        a_rows = []
        for dr in range(2):
            base = dr * _NVEC
            r_row = vec_ref[0, 0, base + _R, pl.ds(j, 1), :]
            mt_row = vec_ref[0, 0, base + _MT, pl.ds(j, 1), :]
            a_row = vec_ref[0, 0, base + _A, pl.ds(j, 1), :]
            lim_row = vec_ref[0, 0, base + _LIM, pl.ds(j, 1), :]
            valid = (rr <= cc) if dr == 0 else (rr >= cc)
            w = jnp.exp(jnp.where(valid, to_col(r_row) - mt_row, NEG_INF))
            qkw_d = st * w
            den = a_row * p_scr[N0 + dr:N0 + dr + 1, :] + jnp.sum(qkw_d, axis=0, keepdims=True)
            inv_den.append(1.0 / jnp.maximum(jnp.abs(den), lim_row))
            a_rows.append(a_row)
            qkw.append(qkw_d.astype(BF16))
        intra = jnp.dot(vt_scr[j], jnp.concatenate(qkw, axis=1), preferred_element_type=F32)
        hs = None
        for dr in range(2):
            inter = p_scr[C0 + dr * d:C0 + (dr + 1) * d, :]
            hd = (a_rows[dr] * inter + intra[:, dr * Lc:(dr + 1) * Lc]) * inv_den[dr]
            hs = hd if hs is None else hs + hd
        mu = jnp.mean(hs, axis=0, keepdims=True)
        hc = hs - mu
        hn = (hc * lax.rsqrt(jnp.mean(hc * hc, axis=0, keepdims=True) + EPS)).T
        mo = mo_ref[0, pl.ds(r0, Lc), :].astype(F32)
        o_ref[0, pl.ds(r0, Lc), :] = (_sigmoid(mo) * hn * gout).astype(BF16)

    bufs = (p0_scr, p1_scr)
    head(0, bufs[0])
    for j in range(nc):
        if j + 1 < nc:
            head(j + 1, bufs[(j + 1) % 2])
        tail(j, bufs[j % 2])


def _mlstm(proj3, vt3, vec5, w_conv, b_conv, g_out):
    B, S, _ = proj3.shape
    H = MLSTM_HEADS
    d = g_out.shape[1] // H
    Lc = MLSTM_CHUNK
    nc = S // Lc
    CW = w_conv.shape[0]
    assert Lc % LANES == 0 and Lc % CONV_ROWS == 0 and d == Lc and CW // 2 <= 8 and nc % 2 == 0 and nc >= 4
    kern = functools.partial(_mlstm_kernel, S=S, d=d, Lc=Lc, CW=CW)
    col = lambda off: (lambda b, h: (b, 0, off + h))
    return pl.pallas_call(
        kern,
        grid=(B, H),
        in_specs=[pl.BlockSpec((1, S, d), col(0)),
                  pl.BlockSpec((1, S, d), col(H)),
                  pl.BlockSpec((1, d, S), lambda b, h: (b, h, 0)),
                  pl.BlockSpec((1, S, d), col(2 * H)),
                  pl.BlockSpec((1, 1, 2 * _NVEC, nc, Lc), lambda b, h: (b, h, 0, 0, 0)),
                  pl.BlockSpec((CW, d), lambda b, h: (0, h)),
                  pl.BlockSpec((CW, d), lambda b, h: (0, H + h)),
                  pl.BlockSpec((1, d), lambda b, h: (0, h)),
                  pl.BlockSpec((1, d), lambda b, h: (0, H + h)),
                  pl.BlockSpec((1, d), lambda b, h: (0, h))],
        out_specs=pl.BlockSpec((1, S, d), lambda b, h: (b, 0, h)),
        out_shape=jax.ShapeDtypeStruct((B, S, H * d), BF16),
        scratch_shapes=[pltpu.VMEM((S + 16, d), F32),
                        pltpu.VMEM((S, d), BF16),
                        pltpu.VMEM((nc, Lc + BF16_ROWS + 2 * d, d), BF16),
                        pltpu.VMEM((nc, d, Lc), BF16),
                        pltpu.VMEM((nc, BF16_ROWS, d), F32),
                        pltpu.VMEM((2, d, d), F32),
                        pltpu.VMEM((2, 8, d), F32),
                        pltpu.VMEM((Lc + BF16_ROWS + 2 * d, Lc), F32),
                        pltpu.VMEM((Lc + BF16_ROWS + 2 * d, Lc), F32)],
        compiler_params=_params(("arbitrary", "arbitrary")),
        name="mlstm",
    )(proj3, proj3, vt3, proj3, vec5, w_conv, w_conv, b_conv, b_conv, g_out)


LOG2E = 1.4426950408889634


def _attn_kernel(sink_ref, q_ref, ktp_ref, ktc_ref, ktn_ref, vtp_ref, vtc_ref, vtn_ref, g_ref, o_ref, *, H, G):
    n = pl.program_id(1)
    nb = pl.num_programs(1)
    blk = ATTN_BLOCK
    hd = ATTN_HEAD_DIM
    R = H // G
    c1 = (hd ** -0.5) * LOG2E
    rr = lax.broadcasted_iota(jnp.int32, (blk, blk), 0)
    cc = lax.broadcasted_iota(jnp.int32, (blk, blk), 1)
    m_prev = jnp.logical_and(cc >= rr, n > 0)
    m_next = jnp.logical_and(cc <= rr, n < nb - 1)
    q = q_ref[0]
    ones = jnp.ones((hd, 3 * blk), BF16)
    outs = []
    for g in range(G):
        rs = slice(g * hd, (g + 1) * hd)
        kt = jnp.concatenate([ktp_ref[0][rs, :], ktc_ref[0][rs, :], ktn_ref[0][rs, :]], axis=1)
        vt = jnp.concatenate([vtp_ref[0][rs, :], vtc_ref[0][rs, :], vtn_ref[0][rs, :]], axis=1)
        vt_aug = jnp.concatenate([vt, ones], axis=0)
        qg = jnp.concatenate([q[:, (g * R + r) * hd:(g * R + r + 1) * hd] for r in range(R)], axis=0)
        s = jnp.dot(qg, kt, preferred_element_type=F32)
        ps = []
        extras = []
        for r in range(R):
            sr = s[r * blk:(r + 1) * blk, :] * c1
            t0 = jnp.where(m_prev, sr[:, 0:blk], NEG_INF)
            t1 = sr[:, blk:2 * blk]
            t2 = jnp.where(m_next, sr[:, 2 * blk:3 * blk], NEG_INF)
            sink2 = sink_ref[g * R + r] * LOG2E
            m = jnp.maximum(jnp.max(jnp.maximum(jnp.maximum(t0, t1), t2), axis=1, keepdims=True), sink2)
            ps.append(jnp.concatenate([jnp.exp2(t0 - m), jnp.exp2(t1 - m), jnp.exp2(t2 - m)], axis=1).astype(BF16))
            extras.append(jnp.exp2(sink2 - m))
        oa = lax.dot_general(jnp.concatenate(ps, axis=0), vt_aug, _NT, preferred_element_type=F32)
        for r in range(R):
            o_r = oa[r * blk:(r + 1) * blk, :]
            outs.append(o_r[:, 0:hd] / (o_r[:, hd:2 * hd] + extras[r]))
    oall = jnp.concatenate(outs, axis=1)
    y = oall * lax.rsqrt(jnp.mean(oall * oall, axis=-1, keepdims=True) + EPS) * g_ref[...]
    o_ref[0] = y.astype(BF16)


def _attn(proj3, t3, sink, g_attn, q_col0, vt_row0, kt_row0):
    B, S, _ = proj3.shape
    blk = ATTN_BLOCK
    nb = S // blk
    H = sink.shape[0]
    G = ATTN_KV_HEADS
    AW = H * ATTN_HEAD_DIM
    KW = G * ATTN_HEAD_DIM
    assert WINDOW == blk and q_col0 % AW == 0 and vt_row0 % KW == 0 and kt_row0 % KW == 0
    qb, vb, kb = q_col0 // AW, vt_row0 // KW, kt_row0 // KW
    kern = functools.partial(_attn_kernel, H=H, G=G)
    prev = lambda rb: (lambda b, n: (b, rb, jnp.maximum(n - 1, 0)))
    cur = lambda rb: (lambda b, n: (b, rb, n))
    nxt = lambda rb: (lambda b, n: (b, rb, jnp.minimum(n + 1, nb - 1)))
    return pl.pallas_call(
        kern,
        grid=(B, nb),
        in_specs=[pl.BlockSpec(memory_space=pltpu.SMEM),
                  pl.BlockSpec((1, blk, AW), lambda b, n: (b, n, qb)),
                  pl.BlockSpec((1, KW, blk), prev(kb)),
                  pl.BlockSpec((1, KW, blk), cur(kb)),
                  pl.BlockSpec((1, KW, blk), nxt(kb)),
                  pl.BlockSpec((1, KW, blk), prev(vb)),
                  pl.BlockSpec((1, KW, blk), cur(vb)),
                  pl.BlockSpec((1, KW, blk), nxt(vb)),
                  pl.BlockSpec((1, AW), lambda b, n: (0, 0))],
        out_specs=pl.BlockSpec((1, blk, AW), lambda b, n: (b, n, 0)),
        out_shape=jax.ShapeDtypeStruct((B, S, AW), BF16),
        compiler_params=_params(("arbitrary", "arbitrary")),
        name="attn",
    )(sink, proj3, t3, t3, t3, t3, t3, t3, g_attn)


def _outproj_kernel(m_ref, a_ref, x_ref, mod_ref, g2_ref, wt_ref, wb_ref, o_ref, h_ref, *, D):
    acc = jnp.dot(m_ref[...], wt_ref[...], preferred_element_type=F32)
    acc = acc + jnp.dot(a_ref[...], wb_ref[...], preferred_element_type=F32)
    gt = mod_ref[0, :, 2 * D:3 * D]
    x1 = x_ref[...] + gt * acc
    o_ref[...] = x1
    sh = mod_ref[0, :, 3 * D:4 * D]
    gsc = g2_ref[...] * (1.0 + mod_ref[0, :, 4 * D:5 * D])
    h_ref[...] = (x1 * lax.rsqrt(jnp.mean(x1 * x1, axis=-1, keepdims=True) + EPS) * gsc + sh).astype(BF16)


def _outproj(m2, a2, x2, mod3, g2, w_o, S, tm=512):
    N, D = x2.shape
    K = m2.shape[1]
    per_b = S // tm
    return pl.pallas_call(
        functools.partial(_outproj_kernel, D=D),
        grid=(N // tm,),
        in_specs=[pl.BlockSpec((tm, K), lambda i: (i, 0)),
                  pl.BlockSpec((tm, K), lambda i: (i, 0)),
                  pl.BlockSpec((tm, D), lambda i: (i, 0)),
                  pl.BlockSpec((1, 1, mod3.shape[2]), lambda i: (i // per_b, 0, 0)),
                  pl.BlockSpec((1, D), lambda i: (0, 0)),
                  pl.BlockSpec((K, D), lambda i: (0, 0)),
                  pl.BlockSpec((K, D), lambda i: (1, 0))],
        out_specs=[pl.BlockSpec((tm, D), lambda i: (i, 0)),
                   pl.BlockSpec((tm, D), lambda i: (i, 0))],
        out_shape=[jax.ShapeDtypeStruct((N, D), F32),
                   jax.ShapeDtypeStruct((N, D), BF16)],
        compiler_params=_params(("arbitrary",)),
        name="outproj",
    )(m2, a2, x2, mod3, g2, w_o, w_o)


def _ffn_up_kernel(h_ref, wg_ref, wu_ref, o_ref):
    hb = h_ref[...]
    gate = jnp.dot(hb, wg_ref[...].astype(BF16), preferred_element_type=F32)
    up = jnp.dot(hb, wu_ref[...].astype(BF16), preferred_element_type=F32)
    o_ref[...] = (gate * _sigmoid(gate) * up).astype(BF16)


def _ffn_up(h2, w_gate, w_up, tm=2048, tf=512):
    N, D = h2.shape
    FF = w_gate.shape[1]
    return pl.pallas_call(
        _ffn_up_kernel,
        grid=(N // tm, FF // tf),
        in_specs=[pl.BlockSpec((tm, D), lambda i, j: (i, 0)),
                  pl.BlockSpec((D, tf), lambda i, j: (0, j)),
                  pl.BlockSpec((D, tf), lambda i, j: (0, j))],
        out_specs=pl.BlockSpec((tm, tf), lambda i, j: (i, j)),
        out_shape=jax.ShapeDtypeStruct((N, FF), BF16),
        compiler_params=_params(("arbitrary", "arbitrary")),
        name="ffn_up",
    )(h2, w_gate, w_up)


def _ffn_down_kernel(u_ref, x_hbm, mod_ref, gf_ref, wd_ref, o_ref, xbuf, sem, *, D, tm):
    i = pl.program_id(0)
    j = pl.program_id(1)

    def x_copy():
        return pltpu.make_async_copy(x_hbm.at[pl.ds(pl.multiple_of(i * tm, tm), tm), :], xbuf, sem)

    @pl.when(j == 0)
    def _():
        x_copy().start()
        o_ref[...] = jnp.zeros(o_ref.shape, F32)

    o_ref[...] += jnp.dot(u_ref[...], wd_ref[...], preferred_element_type=F32)

    @pl.when(j == pl.num_programs(1) - 1)
    def _():
        x_copy().wait()
        gt = mod_ref[0, :, 5 * D:6 * D]
        y = xbuf[...] + gt * o_ref[...]
        o_ref[...] = y * lax.rsqrt(jnp.mean(y * y, axis=-1, keepdims=True) + EPS) * gf_ref[...]


def _ffn_down(u2, x2, mod3, gf, w_down, S, tm=1024, tk=1408):
    N, D = x2.shape
    FF = u2.shape[1]
    per_b = S // tm
    return pl.pallas_call(
        functools.partial(_ffn_down_kernel, D=D, tm=tm),
        grid=(N // tm, FF // tk),
        in_specs=[pl.BlockSpec((tm, tk), lambda i, j: (i, j)),
                  pl.BlockSpec(memory_space=pl.ANY),
                  pl.BlockSpec((1, 1, mod3.shape[2]), lambda i, j: (i // per_b, 0, 0)),
                  pl.BlockSpec((1, D), lambda i, j: (0, 0)),
                  pl.BlockSpec((tk, D), lambda i, j: (j, 0))],
        out_specs=pl.BlockSpec((tm, D), lambda i, j: (i, 0)),
        out_shape=jax.ShapeDtypeStruct((N, D), F32),
        scratch_shapes=[pltpu.VMEM((tm, D), F32), pltpu.SemaphoreType.DMA(())],
        compiler_params=_params(("arbitrary", "arbitrary")),
        name="ffn_down",
    )(u2, x2, mod3, gf, w_down)


def kernel(x, c, positions, w_ada, b_ada, g_norm1, g_norm2, w_in, b_gates, w_conv, b_conv, g_mlstm_out,
           sink, g_attn_out, w_out, w_gate, w_up, w_down, g_final):
    B, S, D = x.shape
    depth = w_ada.shape[0]
    H = MLSTM_HEADS
    MW = D // 2
    KW = ATTN_KV_HEADS * ATTN_HEAD_DIM
    NG = 4 * H
    L = MLSTM_CHUNK
    nc = S // L
    g0 = 4 * MW
    x2 = x.reshape(B * S, D)
    pos2 = positions.reshape(B * S, 1)
    for l in range(depth):
        mod3 = _ada(c, w_ada[l], b_ada[l]).reshape(B, 1, 6 * D)
        wl = w_in[l].astype(BF16)
        a0 = g0 + NG
        w_main = jnp.concatenate([wl[:, :2 * MW], wl[:, 3 * MW:g0], wl[:, a0:a0 + MW]], axis=1)
        t_rows = MW + 2 * KW + NG
        t_pad = -t_rows % (2 * BF16_ROWS)
        w_t = jnp.concatenate([wl[:, 2 * MW:3 * MW], wl[:, a0 + MW + KW:], wl[:, a0 + MW:a0 + MW + KW],
                               wl[:, g0:a0], jnp.zeros((D, t_pad), BF16)], axis=1).T
        proj, t3, gates_t = _inproj(x2, mod3, g_norm1[l].reshape(1, D), pos2, w_main, w_t, B, S, NG)
        proj3 = proj.reshape(B, S, proj.shape[1])
        gates = gates_t.reshape(4, H, B, nc, L).transpose(0, 3, 1, 2, 4).reshape(4, nc * H * B, L)
        gbias = jnp.broadcast_to(b_gates[l].reshape(4, 1, H, 1), (4, nc, H, B)).reshape(4, nc * H * B, 1)
        vec = _gatevec(gates, gbias, nc)
        vec5 = vec.reshape(2 * _NVEC, nc, H, B, L).transpose(3, 2, 0, 1, 4)
        m_out = _mlstm(proj3, t3, vec5, w_conv[l], b_conv[l].reshape(1, 2 * MW), g_mlstm_out[l].reshape(1, MW))
        a_out = _attn(proj3, t3, sink[l], g_attn_out[l].reshape(1, MW), q_col0=3 * MW,
                      vt_row0=MW, kt_row0=MW + KW)
        x2, h2 = _outproj(m_out.reshape(B * S, MW), a_out.reshape(B * S, MW), x2, mod3,
                          g_norm2[l].reshape(1, D), w_out[l].astype(BF16), S)
        assert l == depth - 1, "final norm is fused into the last layer's FFN"
        u2 = _ffn_up(h2, w_gate[l], w_up[l])
        x2 = _ffn_down(u2, x2, mod3, g_final.reshape(1, D), w_down[l].astype(BF16), S)
    return x2.reshape(B, S, D)
```

```python
import functools

import jax
import jax.numpy as jnp
from jax import lax
from jax.experimental import pallas as pl
from jax.experimental.pallas import tpu as pltpu

F32 = jnp.float32
BF16 = jnp.bfloat16

EPS = 1e-6
NEG_INF = -1e30
ROPE_THETA = 10000.0

MLSTM_HEADS = 4
MLSTM_CHUNK = 256
ATTN_HEAD_DIM = 128
ATTN_KV_HEADS = 2
WINDOW = 128
ATTN_BLOCK = 128

LANES = 128
BF16_ROWS = 16
VMEM_LIMIT = 56 * 1024 * 1024

_NT = (((1,), (1,)), ((), ()))


def _sigmoid(x):
    return 1.0 / (1.0 + jnp.exp(-x))


def _params(sem, vmem=VMEM_LIMIT):
    return pltpu.CompilerParams(dimension_semantics=sem, vmem_limit_bytes=vmem)


def _split_bf16(v):
    hi = v.astype(BF16)
    return hi, (v - hi.astype(F32)).astype(BF16)


def _ada_kernel(c_ref, w_ref, b_ref, o_ref):
    c = c_ref[...]
    s_hi, s_lo = _split_bf16(c * _sigmoid(c))
    w_hi, w_lo = _split_bf16(w_ref[...])
    dot = functools.partial(jnp.dot, preferred_element_type=F32)
    o_ref[...] = dot(s_hi, w_hi) + (dot(s_lo, w_hi) + dot(s_hi, w_lo)) + b_ref[...]


def _ada(c, w_ada, b_ada, tn=1024):
    B, D = c.shape
    N = w_ada.shape[1]
    return pl.pallas_call(
        _ada_kernel,
        grid=(N // tn,),
        in_specs=[pl.BlockSpec((B, D), lambda j: (0, 0)),
                  pl.BlockSpec((D, tn), lambda j: (0, j)),
                  pl.BlockSpec((1, tn), lambda j: (0, j))],
        out_specs=pl.BlockSpec((B, tn), lambda j: (0, j)),
        out_shape=jax.ShapeDtypeStruct((B, N), F32),
        compiler_params=_params(("arbitrary",)),
        name="ada",
    )(c, w_ada, b_ada.reshape(1, N))


def _rope(t, cos2, sin2):
    return t * cos2 + pltpu.roll(t, ATTN_HEAD_DIM // 2, 1) * sin2


PREP_SLABS = 4
PREP_STEP0 = 2


def _inproj_kernel(x_hbm, mod_ref, g_ref, pos_ref, inv_ref, sgn_ref, w_ref, wt_ref,
                   o_ref, t_ref, gt_ref, xbuf, h_scr, cos_scr, sin_scr, cost_scr, sint_scr, sem,
                   *, D, tm, n_tiles, n_main, rope_j0, n_t, ak_rows, gate_rows):
    r = pl.program_id(0)
    j = pl.program_id(1)
    hd = ATTN_HEAD_DIM
    rows = tm // PREP_SLABS
    ps = r % 2
    cs = 1 - ps
    tile_p = jnp.minimum(r, n_tiles - 1)

    def x_copy():
        return pltpu.make_async_copy(x_hbm.at[pl.ds(pl.multiple_of(tile_p * tm, tm), tm), :], xbuf, sem)

    @pl.when(j == 0)
    def _():
        x_copy().start()

    @pl.when(j == PREP_STEP0 - 1)
    def _():
        x_copy().wait()

    def prep_slab():
        k = j - PREP_STEP0
        r0 = pl.multiple_of(k * rows, rows)
        x = xbuf[pl.ds(r0, rows), :]
        sh = mod_ref[0, :, 0:D]
        gsc = g_ref[...] * (1.0 + mod_ref[0, :, D:2 * D])
        h_scr[ps, pl.ds(r0, rows), :] = (
            x * lax.rsqrt(jnp.mean(x * x, axis=-1, keepdims=True) + EPS) * gsc + sh).astype(BF16)
        ang = pos_ref[pl.ds(r0, rows), :].astype(F32) * inv_ref[...]
        cos2 = jnp.cos(ang)
        sin2 = jnp.sin(ang) * sgn_ref[...]
        cos_scr[ps, pl.ds(r0, rows), :] = cos2
        sin_scr[ps, pl.ds(r0, rows), :] = sin2
        cost_scr[ps, k] = cos2.T
        sint_scr[ps, k] = sin2.T

    def plain():
        o_ref[...] = jnp.dot(h_scr[cs], w_ref[...], preferred_element_type=F32).astype(BF16)

    def roped():
        acc = jnp.dot(h_scr[cs], w_ref[...], preferred_element_type=F32)
        cos2 = cos_scr[cs]
        sin2 = sin_scr[cs]
        parts = [_rope(acc[:, hh * hd:(hh + 1) * hd], cos2, sin2) for hh in range(acc.shape[1] // hd)]
        o_ref[...] = jnp.concatenate(parts, axis=1).astype(BF16)

    tt = t_ref.shape[1]

    def transposed(ti):
        acc = lax.dot_general(wt_ref[...], h_scr[cs], _NT, preferred_element_type=F32)
        lo, hi = ti * tt, (ti + 1) * tt
        pieces = []
        cur = lo
        for a0 in range(ak_rows[0], ak_rows[1], hd):
            if a0 < lo or a0 + hd > hi:
                assert a0 + hd <= lo or a0 >= hi, "a k head straddles two transposed tiles"
                continue
            if a0 > cur:
                pieces.append(acc[cur - lo:a0 - lo, :])
            t = acc[a0 - lo:a0 - lo + hd, :]
            rot = jnp.concatenate([t[hd // 2:, :], t[:hd // 2, :]], axis=0)
            cost = jnp.concatenate([cost_scr[cs, k] for k in range(PREP_SLABS)], axis=1)
            sint = jnp.concatenate([sint_scr[cs, k] for k in range(PREP_SLABS)], axis=1)
            pieces.append(t * cost + rot * sint)
            cur = a0 + hd
        if cur < hi:
            pieces.append(acc[cur - lo:, :])
        t_ref[0] = jnp.concatenate(pieces, axis=0).astype(BF16) if len(pieces) > 1 else acc.astype(BF16)
        if lo <= gate_rows[0] and gate_rows[1] <= hi:
            gt_ref[...] = acc[gate_rows[0] - lo:gate_rows[1] - lo, :]

    mult = r > 0
    both = jnp.logical_and

    @pl.when(both(mult, j < PREP_STEP0))
    def _():
        plain()

    @pl.when(both(mult, both(j >= PREP_STEP0, j < rope_j0)))
    def _():
        plain()
        prep_slab()

    @pl.when(both(mult, both(j >= rope_j0, j < n_main)))
    def _():
        roped()
        prep_slab()

    for ti in range(n_t):
        @pl.when(both(mult, j == n_main + ti))
        def _(ti=ti):
            transposed(ti)
            prep_slab()

    @pl.when(both(jnp.logical_not(mult), j >= PREP_STEP0))
    def _():
        prep_slab()


def _inproj(x2, mod3, g1, pos2, w_main, w_t, B, S, NG, tm=1024, tn=1024, n_t=2):
    N, D = x2.shape
    P = w_main.shape[1]
    TR = w_t.shape[0]
    hd = ATTN_HEAD_DIM
    half = hd // 2
    inv = ROPE_THETA ** (-jnp.arange(half, dtype=F32) * 2.0 / hd)
    inv2 = jnp.concatenate([inv, inv]).reshape(1, hd)
    sgn = jnp.concatenate([-jnp.ones((half,), F32), jnp.ones((half,), F32)]).reshape(1, hd)
    MW = D // 2
    KW = ATTN_KV_HEADS * hd
    aq0 = 3 * MW
    ak_rows = (MW + KW, MW + 2 * KW)
    gate_rows = (MW + 2 * KW, MW + 2 * KW + NG)
    tt = TR // n_t
    n_tiles = N // tm
    n_main = P // tn
    n_j = n_main + n_t
    rope_j0 = aq0 // tn
    assert aq0 % tn == 0 and P == aq0 + MW and P % tn == 0 and TR % n_t == 0 and tt % BF16_ROWS == 0
    assert gate_rows[1] <= TR and tm % PREP_SLABS == 0 and n_j == PREP_STEP0 + PREP_SLABS
    assert PREP_STEP0 <= rope_j0
    kern = functools.partial(_inproj_kernel, D=D, tm=tm, n_tiles=n_tiles, n_main=n_main, rope_j0=rope_j0,
                             n_t=n_t, ak_rows=ak_rows, gate_rows=gate_rows)
    per_b = S // tm
    tile_p = lambda r: jnp.minimum(r, n_tiles - 1)
    tile_c = lambda r: jnp.maximum(r - 1, 0)
    live = lambda r: jnp.minimum(r, 1)
    main_j = lambda r, j: jnp.minimum(j, n_main - 1) * live(r)
    t_j = lambda r, j: jnp.maximum(j - n_main, 0) * live(r)
    return pl.pallas_call(
        kern,
        grid=(n_tiles + 1, n_j),
        in_specs=[pl.BlockSpec(memory_space=pl.ANY),
                  pl.BlockSpec((1, 1, mod3.shape[2]), lambda r, j: (tile_p(r) // per_b, 0, 0)),
                  pl.BlockSpec((1, D), lambda r, j: (0, 0)),
                  pl.BlockSpec((tm, 1), lambda r, j: (tile_p(r), 0)),
                  pl.BlockSpec((1, hd), lambda r, j: (0, 0)),
                  pl.BlockSpec((1, hd), lambda r, j: (0, 0)),
                  pl.BlockSpec((D, tn), lambda r, j: (0, jnp.minimum(j, n_main - 1))),
                  pl.BlockSpec((tt, D), lambda r, j: (jnp.maximum(j - n_main, 0), 0))],
        out_specs=[pl.BlockSpec((tm, tn), lambda r, j: (tile_c(r), main_j(r, j))),
                   pl.BlockSpec((1, tt, tm), lambda r, j: (tile_c(r) // per_b, t_j(r, j), tile_c(r) % per_b)),
                   pl.BlockSpec((NG, tm), lambda r, j: (0, tile_c(r)))],
        out_shape=[jax.ShapeDtypeStruct((N, P), BF16),
                   jax.ShapeDtypeStruct((B, TR, S), BF16),
                   jax.ShapeDtypeStruct((NG, N), F32)],
        scratch_shapes=[pltpu.VMEM((tm, D), F32),
                        pltpu.VMEM((2, tm, D), BF16),
                        pltpu.VMEM((2, tm, hd), F32),
                        pltpu.VMEM((2, tm, hd), F32),
                        pltpu.VMEM((2, PREP_SLABS, hd, tm // PREP_SLABS), F32),
                        pltpu.VMEM((2, PREP_SLABS, hd, tm // PREP_SLABS), F32),
                        pltpu.SemaphoreType.DMA(())],
        compiler_params=_params(("arbitrary", "arbitrary")),
        name="inproj",
    )(x2, mod3, g1, pos2, inv2, sgn, w_main, w_t)


_R, _MT, _A, _LIM, _E, _SC = range(6)
_NVEC = 6


def _log_sigmoid(x):
    return jnp.minimum(x, 0.0) - jnp.log1p(jnp.exp(-jnp.abs(x)))


def _lane_scan(x, reverse, op, ident):
    L = x.shape[1]
    lane = lax.broadcasted_iota(jnp.int32, x.shape, 1)
    sh = 1
    while sh < L:
        if reverse:
            x = op(x, jnp.where(lane < L - sh, pltpu.roll(x, L - sh, 1), ident))
        else:
            x = op(x, jnp.where(lane >= sh, pltpu.roll(x, sh, 1), ident))
        sh *= 2
    return x


def _chunk_scan(x, reverse, op, ident):
    n = x.shape[1] // LANES
    parts = [_lane_scan(x[:, i * LANES:(i + 1) * LANES], reverse, op, ident) for i in range(n)]
    order = range(n - 2, -1, -1) if reverse else range(1, n)
    for i in order:
        prev = parts[i + 1][:, 0:1] if reverse else parts[i - 1][:, LANES - 1:LANES]
        parts[i] = op(parts[i], prev)
    return jnp.concatenate(parts, axis=1)


def _gate_vectors(li, lf_pre, reverse, nc):
    rows, Lc = li.shape
    hb = rows // nc
    lf = _log_sigmoid(lf_pre)
    b = _chunk_scan(lf, reverse, jnp.add, 0.0)
    r = li - b
    rcm = _chunk_scan(r, reverse, jnp.maximum, NEG_INF)
    end = slice(0, 1) if reverse else slice(Lc - 1, Lc)
    bl = jnp.broadcast_to(b[:, end], (rows, Lc))
    rmax = jnp.broadcast_to(rcm[:, end], (rows, Lc))
    m = jnp.zeros((hb, Lc), F32)
    m_parts = [None] * nc
    for j in (range(nc - 1, -1, -1) if reverse else range(nc)):
        m_parts[j] = m
        sl = slice(j * hb, (j + 1) * hb)
        m = bl[sl, :] + jnp.maximum(m, rmax[sl, :])
    m_all = jnp.concatenate(m_parts, axis=0)
    mt = jnp.maximum(m_all, rcm)
    m_end = jnp.maximum(m_all, rmax)
    return r, mt, jnp.exp(m_all - mt), jnp.exp(-(b + mt)), jnp.exp(r - m_end), jnp.exp(m_all - m_end)


def _gatevec_kernel(g_ref, gb_ref, o_ref, *, nc):
    for dr in range(2):
        li = g_ref[2 * dr] + gb_ref[2 * dr]
        lf_pre = g_ref[2 * dr + 1] + gb_ref[2 * dr + 1]
        for idx, val in enumerate(_gate_vectors(li, lf_pre, dr == 1, nc)):
            o_ref[dr * _NVEC + idx] = val


def _gatevec(gates, gbias, nc):
    _, rows, Lc = gates.shape
    return pl.pallas_call(
        functools.partial(_gatevec_kernel, nc=nc),
        out_shape=jax.ShapeDtypeStruct((2 * _NVEC, rows, Lc), F32),
        compiler_params=pltpu.CompilerParams(vmem_limit_bytes=VMEM_LIMIT),
        name="gatevec",
    )(gates, gbias)


CONV_ROWS = 128


def _mlstm_kernel(q_ref, k_ref, vt_ref, mo_ref, vec_ref, wq_ref, wk_ref, bq_ref, bk_ref, go_ref,
                  o_ref, pad_scr, qs_scr, big_scr, vt_scr, ns_scr, ct_scr, n_scr, p0_scr, p1_scr, *, S, d, Lc, CW):
    nc = S // Lc
    half_w = CW // 2
    PADR = 8
    CR = CONV_ROWS
    N0 = Lc
    C0 = Lc + BF16_ROWS

    zero_rows = jnp.zeros((PADR, d), F32)

    def conv_silu(src_ref, w_ref, b_ref, store, scale):
        pad_scr[0:PADR, :] = zero_rows
        pad_scr[PADR + S:2 * PADR + S, :] = zero_rows

        def fill(j, c):
            r0 = pl.multiple_of(j * CR, CR)
            pad_scr[pl.ds(PADR + r0, CR), :] = src_ref[0, pl.ds(r0, CR), :].astype(F32)
            return c
        lax.fori_loop(0, S // CR, fill, 0)
        w = w_ref[...]
        bias = b_ref[...]

        def body(j, c):
            r0 = pl.multiple_of(j * CR, CR)
            win = pad_scr[pl.ds(r0, CR + 2 * PADR), :]
            y = bias
            for i in range(CW):
                o = PADR - half_w + i
                y = y + win[o:o + CR, :] * w[i:i + 1, :]
            y = y * _sigmoid(y)
            if scale != 1.0:
                y = y * scale
            store(j, r0, y.astype(BF16))
            return c
        lax.fori_loop(0, S // CR, body, 0)

    def store_q(j, r0, y):
        qs_scr[pl.ds(r0, CR), :] = y

    per = Lc // CR

    def store_k(j, r0, y):
        big_scr[j // per, pl.ds(pl.multiple_of((j % per) * CR, CR), CR), :] = y

    conv_silu(q_ref, wq_ref, bq_ref, store_q, 1.0)
    conv_silu(k_ref, wk_ref, bk_ref, store_k, d ** -0.5)

    for j in range(nc):
        vt_scr[j] = vt_ref[0, :, j * Lc:(j + 1) * Lc]

    ns_scr[...] = jnp.zeros(ns_scr.shape, F32)
    ct_scr[...] = jnp.zeros(ct_scr.shape, F32)
    n_scr[...] = jnp.zeros(n_scr.shape, F32)

    def state_step(dr, jc):
        base = dr * _NVEC
        e_row = vec_ref[0, 0, base + _E, pl.ds(jc, 1), :]
        sc = vec_ref[0, 0, base + _SC, pl.ds(jc, 1), :]
        lhs = jnp.concatenate([vt_scr[jc].astype(F32) * e_row, jnp.broadcast_to(e_row, (BF16_ROWS, Lc))], axis=0)
        upd = jnp.dot(lhs.astype(BF16), big_scr[jc, 0:Lc, :], preferred_element_type=F32)
        ct = ct_scr[dr]
        n = n_scr[dr]
        big_scr[jc, C0 + dr * d:C0 + (dr + 1) * d, :] = ct.astype(BF16)
        ns_scr[jc, dr:dr + 1, :] = n[0:1, :]
        ct_scr[dr] = sc * ct + upd[0:d, :]
        n_scr[dr] = sc * n + upd[d:d + 8, :]

    for j in range(nc):
        state_step(0, j)
        state_step(1, nc - 1 - j)
    for j in range(nc):
        big_scr[j, N0:N0 + BF16_ROWS, :] = ns_scr[j].astype(BF16)

    rr = lax.broadcasted_iota(jnp.int32, (Lc, Lc), 0)
    cc = lax.broadcasted_iota(jnp.int32, (Lc, Lc), 1)
    r1 = lax.broadcasted_iota(jnp.int32, (LANES, LANES), 0)
    c1 = lax.broadcasted_iota(jnp.int32, (LANES, LANES), 1)
    eye = r1 == c1
    gout = go_ref[...]

    def to_col(row):
        return jnp.concatenate(
            [jnp.sum(jnp.where(eye, row[:, i * LANES:(i + 1) * LANES], 0.0), axis=1, keepdims=True)
             for i in range(Lc // LANES)], axis=0)

    hrows = C0 + d

    def head(j, p_scr):
        qj = qs_scr[pl.ds(pl.multiple_of(j * Lc, Lc), Lc), :]
        p_scr[0:hrows, :] = lax.dot_general(big_scr[j, 0:hrows, :], qj, _NT, preferred_element_type=F32)
        p_scr[hrows:hrows + d, :] = lax.dot_general(big_scr[j, hrows:hrows + d, :], qj, _NT,
                                                    preferred_element_type=F32)

    def tail(j, p_scr):
        r0 = pl.multiple_of(j * Lc, Lc)
        st = p_scr[0:Lc, :]
        qkw = []
        inv_den = []
        a_rows = []
        for dr in range(2):
            base = dr * _NVEC
            r_row = vec_ref[0, 0, base + _R, pl.ds(j, 1), :]
            mt_row = vec_ref[0, 0, base + _MT, pl.ds(j, 1), :]
            a_row = vec_ref[0, 0, base + _A, pl.ds(j, 1), :]
            lim_row = vec_ref[0, 0, base + _LIM, pl.ds(j, 1), :]
            valid = (rr <= cc) if dr == 0 else (rr >= cc)
            w = jnp.exp(jnp.where(valid, to_col(r_row) - mt_row, NEG_INF))
            qkw_d = st * w
            den = a_row * p_scr[N0 + dr:N0 + dr + 1, :] + jnp.sum(qkw_d, axis=0, keepdims=True)
            inv_den.append(1.0 / jnp.maximum(jnp.abs(den), lim_row))
            a_rows.append(a_row)
            qkw.append(qkw_d.astype(BF16))
        intra = jnp.dot(vt_scr[j], jnp.concatenate(qkw, axis=1), preferred_element_type=F32)
        hs = None
        for dr in range(2):
            inter = p_scr[C0 + dr * d:C0 + (dr + 1) * d, :]
            hd = (a_rows[dr] * inter + intra[:, dr * Lc:(dr + 1) * Lc]) * inv_den[dr]
            hs = hd if hs is None else hs + hd
        mu = jnp.mean(hs, axis=0, keepdims=True)
        hc = hs - mu
        hn = (hc * lax.rsqrt(jnp.mean(hc * hc, axis=0, keepdims=True) + EPS)).T
        mo = mo_ref[0, pl.ds(r0, Lc), :].astype(F32)
        o_ref[0, pl.ds(r0, Lc), :] = (_sigmoid(mo) * hn * gout).astype(BF16)

    bufs = (p0_scr, p1_scr)
    head(0, bufs[0])
    for j in range(nc):
        if j + 1 < nc:
            head(j + 1, bufs[(j + 1) % 2])
        tail(j, bufs[j % 2])


def _mlstm(proj3, vt3, vec5, w_conv, b_conv, g_out):
    B, S, _ = proj3.shape
    H = MLSTM_HEADS
    d = g_out.shape[1] // H
    Lc = MLSTM_CHUNK
    nc = S // Lc
    CW = w_conv.shape[0]
    assert Lc % LANES == 0 and Lc % CONV_ROWS == 0 and d == Lc and CW // 2 <= 8 and nc % 2 == 0 and nc >= 4
    kern = functools.partial(_mlstm_kernel, S=S, d=d, Lc=Lc, CW=CW)
    col = lambda off: (lambda b, h: (b, 0, off + h))
    return pl.pallas_call(
        kern,
        grid=(B, H),
        in_specs=[pl.BlockSpec((1, S, d), col(0)),
                  pl.BlockSpec((1, S, d), col(H)),
                  pl.BlockSpec((1, d, S), lambda b, h: (b, h, 0)),
                  pl.BlockSpec((1, S, d), col(2 * H)),
                  pl.BlockSpec((1, 1, 2 * _NVEC, nc, Lc), lambda b, h: (b, h, 0, 0, 0)),
                  pl.BlockSpec((CW, d), lambda b, h: (0, h)),
                  pl.BlockSpec((CW, d), lambda b, h: (0, H + h)),
                  pl.BlockSpec((1, d), lambda b, h: (0, h)),
                  pl.BlockSpec((1, d), lambda b, h: (0, H + h)),
                  pl.BlockSpec((1, d), lambda b, h: (0, h))],
        out_specs=pl.BlockSpec((1, S, d), lambda b, h: (b, 0, h)),
        out_shape=jax.ShapeDtypeStruct((B, S, H * d), BF16),
        scratch_shapes=[pltpu.VMEM((S + 16, d), F32),
                        pltpu.VMEM((S, d), BF16),
                        pltpu.VMEM((nc, Lc + BF16_ROWS + 2 * d, d), BF16),
                        pltpu.VMEM((nc, d, Lc), BF16),
                        pltpu.VMEM((nc, BF16_ROWS, d), F32),
                        pltpu.VMEM((2, d, d), F32),
                        pltpu.VMEM((2, 8, d), F32),
                        pltpu.VMEM((Lc + BF16_ROWS + 2 * d, Lc), F32),
                        pltpu.VMEM((Lc + BF16_ROWS + 2 * d, Lc), F32)],
        compiler_params=_params(("arbitrary", "arbitrary")),
        name="mlstm",
    )(proj3, proj3, vt3, proj3, vec5, w_conv, w_conv, b_conv, b_conv, g_out)


LOG2E = 1.4426950408889634
ATTN_QBLOCKS = 4


def _attn_kernel(sink_ref, q_ref, ktp_ref, ktc_ref, ktn_ref, vtp_ref, vtc_ref, vtn_ref, g_ref, o_ref,
                 s0_scr, s1_scr, *, H, G, QB):
    n4 = pl.program_id(1)
    last4 = pl.num_programs(1) - 1
    blk = ATTN_BLOCK
    hd = ATTN_HEAD_DIM
    R = H // G
    c1 = (hd ** -0.5) * LOG2E
    rr = lax.broadcasted_iota(jnp.int32, (blk, blk), 0)
    cc = lax.broadcasted_iota(jnp.int32, (blk, blk), 1)
    ones = jnp.ones((hd, 3 * blk), BF16)

    def kv_window(i, g, prev_ref, cur_ref, next_ref):
        rs = slice(g * hd, (g + 1) * hd)
        lo = prev_ref[0, rs, :] if i == 0 else cur_ref[0, rs, (i - 1) * blk:i * blk]
        hi = next_ref[0, rs, :] if i == QB - 1 else cur_ref[0, rs, (i + 1) * blk:(i + 2) * blk]
        return jnp.concatenate([lo, cur_ref[0, rs, i * blk:(i + 1) * blk], hi], axis=1)

    def head(i, s_scr):
        q = q_ref[0, i * blk:(i + 1) * blk, :]
        for g in range(G):
            qg = jnp.concatenate([q[:, (g * R + r) * hd:(g * R + r + 1) * hd] for r in range(R)], axis=0)
            s_scr[g] = jnp.dot(qg, kv_window(i, g, ktp_ref, ktc_ref, ktn_ref), preferred_element_type=F32)

    def tail(i, s_scr):
        m_prev = (cc >= rr) if i > 0 else jnp.logical_and(cc >= rr, n4 > 0)
        m_next = (cc <= rr) if i < QB - 1 else jnp.logical_and(cc <= rr, n4 < last4)
        outs = []
        for g in range(G):
            vt = kv_window(i, g, vtp_ref, vtc_ref, vtn_ref)
            vt_aug = jnp.concatenate([vt, ones], axis=0)
            ps = []
            extras = []
            for r in range(R):
                sr = s_scr[g, r * blk:(r + 1) * blk, :] * c1
                t0 = jnp.where(m_prev, sr[:, 0:blk], NEG_INF)
                t1 = sr[:, blk:2 * blk]
                t2 = jnp.where(m_next, sr[:, 2 * blk:3 * blk], NEG_INF)
                sink2 = sink_ref[g * R + r] * LOG2E
                m = jnp.maximum(jnp.max(jnp.maximum(jnp.maximum(t0, t1), t2), axis=1, keepdims=True), sink2)
                ps.append(jnp.concatenate([jnp.exp2(t0 - m), jnp.exp2(t1 - m), jnp.exp2(t2 - m)],
                                          axis=1).astype(BF16))
                extras.append(jnp.exp2(sink2 - m))
            oa = lax.dot_general(jnp.concatenate(ps, axis=0), vt_aug, _NT, preferred_element_type=F32)
            for r in range(R):
                o_r = oa[r * blk:(r + 1) * blk, :]
                outs.append(o_r[:, 0:hd] / (o_r[:, hd:2 * hd] + extras[r]))
        oall = jnp.concatenate(outs, axis=1)
        y = oall * lax.rsqrt(jnp.mean(oall * oall, axis=-1, keepdims=True) + EPS) * g_ref[...]
        o_ref[0, i * blk:(i + 1) * blk, :] = y.astype(BF16)

    bufs = (s0_scr, s1_scr)
    head(0, bufs[0])
    for i in range(QB):
        if i + 1 < QB:
            head(i + 1, bufs[(i + 1) % 2])
        tail(i, bufs[i % 2])


def _attn(proj3, t3, sink, g_attn, q_col0, vt_row0, kt_row0):
    B, S, _ = proj3.shape
    blk = ATTN_BLOCK
    QB = ATTN_QBLOCKS
    nb = S // blk
    H = sink.shape[0]
    G = ATTN_KV_HEADS
    AW = H * ATTN_HEAD_DIM
    KW = G * ATTN_HEAD_DIM
    assert WINDOW == blk and q_col0 % AW == 0 and vt_row0 % KW == 0 and kt_row0 % KW == 0 and nb % QB == 0
    qb, vb, kb = q_col0 // AW, vt_row0 // KW, kt_row0 // KW
    kern = functools.partial(_attn_kernel, H=H, G=G, QB=QB)
    prev = lambda rb: (lambda b, n: (b, rb, jnp.maximum(n * QB - 1, 0)))
    cur = lambda rb: (lambda b, n: (b, rb, n))
    nxt = lambda rb: (lambda b, n: (b, rb, jnp.minimum(n * QB + QB, nb - 1)))
    return pl.pallas_call(
        kern,
        grid=(B, nb // QB),
        in_specs=[pl.BlockSpec(memory_space=pltpu.SMEM),
                  pl.BlockSpec((1, QB * blk, AW), lambda b, n: (b, n, qb)),
                  pl.BlockSpec((1, KW, blk), prev(kb)),
                  pl.BlockSpec((1, KW, QB * blk), cur(kb)),
                  pl.BlockSpec((1, KW, blk), nxt(kb)),
                  pl.BlockSpec((1, KW, blk), prev(vb)),
                  pl.BlockSpec((1, KW, QB * blk), cur(vb)),
                  pl.BlockSpec((1, KW, blk), nxt(vb)),
                  pl.BlockSpec((1, AW), lambda b, n: (0, 0))],
        out_specs=pl.BlockSpec((1, QB * blk, AW), lambda b, n: (b, n, 0)),
        out_shape=jax.ShapeDtypeStruct((B, S, AW), BF16),
        scratch_shapes=[pltpu.VMEM((G, (H // G) * blk, 3 * blk), F32),
                        pltpu.VMEM((G, (H // G) * blk, 3 * blk), F32)],
        compiler_params=_params(("arbitrary", "arbitrary")),
        name="attn",
    )(sink, proj3, t3, t3, t3, t3, t3, t3, g_attn)


def _outproj_kernel(m_ref, a_ref, x_ref, mod_ref, g2_ref, wt_ref, wb_ref, o_ref, h_ref, *, D):
    acc = jnp.dot(m_ref[...], wt_ref[...], preferred_element_type=F32)
    acc = acc + jnp.dot(a_ref[...], wb_ref[...], preferred_element_type=F32)
    gt = mod_ref[0, :, 2 * D:3 * D]
    x1 = x_ref[...] + gt * acc
    o_ref[...] = x1
    sh = mod_ref[0, :, 3 * D:4 * D]
    gsc = g2_ref[...] * (1.0 + mod_ref[0, :, 4 * D:5 * D])
    h_ref[...] = (x1 * lax.rsqrt(jnp.mean(x1 * x1, axis=-1, keepdims=True) + EPS) * gsc + sh).astype(BF16)


def _outproj(m2, a2, x2, mod3, g2, w_o, S, tm=512):
    N, D = x2.shape
    K = m2.shape[1]
    per_b = S // tm
    return pl.pallas_call(
        functools.partial(_outproj_kernel, D=D),
        grid=(N // tm,),
        in_specs=[pl.BlockSpec((tm, K), lambda i: (i, 0)),
                  pl.BlockSpec((tm, K), lambda i: (i, 0)),
                  pl.BlockSpec((tm, D), lambda i: (i, 0)),
                  pl.BlockSpec((1, 1, mod3.shape[2]), lambda i: (i // per_b, 0, 0)),
                  pl.BlockSpec((1, D), lambda i: (0, 0)),
                  pl.BlockSpec((K, D), lambda i: (0, 0)),
                  pl.BlockSpec((K, D), lambda i: (1, 0))],
        out_specs=[pl.BlockSpec((tm, D), lambda i: (i, 0)),
                   pl.BlockSpec((tm, D), lambda i: (i, 0))],
        out_shape=[jax.ShapeDtypeStruct((N, D), F32),
                   jax.ShapeDtypeStruct((N, D), BF16)],
        compiler_params=_params(("arbitrary",)),
        name="outproj",
    )(m2, a2, x2, mod3, g2, w_o, w_o)


def _ffn_up_kernel(h_ref, wg_ref, wu_ref, o_ref):
    hb = h_ref[...]
    gate = jnp.dot(hb, wg_ref[...].astype(BF16), preferred_element_type=F32)
    up = jnp.dot(hb, wu_ref[...].astype(BF16), preferred_element_type=F32)
    o_ref[...] = (gate * _sigmoid(gate) * up).astype(BF16)


def _ffn_up(h2, w_gate, w_up, tm=2048, tf=512):
    N, D = h2.shape
    FF = w_gate.shape[1]
    return pl.pallas_call(
        _ffn_up_kernel,
        grid=(N // tm, FF // tf),
        in_specs=[pl.BlockSpec((tm, D), lambda i, j: (i, 0)),
                  pl.BlockSpec((D, tf), lambda i, j: (0, j)),
                  pl.BlockSpec((D, tf), lambda i, j: (0, j))],
        out_specs=pl.BlockSpec((tm, tf), lambda i, j: (i, j)),
        out_shape=jax.ShapeDtypeStruct((N, FF), BF16),
        compiler_params=_params(("arbitrary", "arbitrary")),
        name="ffn_up",
    )(h2, w_gate, w_up)


def _ffn_down_kernel(u_ref, x_hbm, mod_ref, gf_ref, wd_ref, o_ref, xbuf, sem, *, D, tm):
    i = pl.program_id(0)
    j = pl.program_id(1)

    def x_copy():
        return pltpu.make_async_copy(x_hbm.at[pl.ds(pl.multiple_of(i * tm, tm), tm), :], xbuf, sem)

    @pl.when(j == 0)
    def _():
        x_copy().start()
        o_ref[...] = jnp.zeros(o_ref.shape, F32)

    o_ref[...] += jnp.dot(u_ref[...], wd_ref[...], preferred_element_type=F32)

    @pl.when(j == pl.num_programs(1) - 1)
    def _():
        x_copy().wait()
        gt = mod_ref[0, :, 5 * D:6 * D]
        y = xbuf[...] + gt * o_ref[...]
        o_ref[...] = y * lax.rsqrt(jnp.mean(y * y, axis=-1, keepdims=True) + EPS) * gf_ref[...]


def _ffn_down(u2, x2, mod3, gf, w_down, S, tm=1024, tk=1408):
    N, D = x2.shape
    FF = u2.shape[1]
    per_b = S // tm
    return pl.pallas_call(
        functools.partial(_ffn_down_kernel, D=D, tm=tm),
        grid=(N // tm, FF // tk),
        in_specs=[pl.BlockSpec((tm, tk), lambda i, j: (i, j)),
                  pl.BlockSpec(memory_space=pl.ANY),
                  pl.BlockSpec((1, 1, mod3.shape[2]), lambda i, j: (i // per_b, 0, 0)),
                  pl.BlockSpec((1, D), lambda i, j: (0, 0)),
                  pl.BlockSpec((tk, D), lambda i, j: (j, 0))],
        out_specs=pl.BlockSpec((tm, D), lambda i, j: (i, 0)),
        out_shape=jax.ShapeDtypeStruct((N, D), F32),
        scratch_shapes=[pltpu.VMEM((tm, D), F32), pltpu.SemaphoreType.DMA(())],
        compiler_params=_params(("arbitrary", "arbitrary")),
        name="ffn_down",
    )(u2, x2, mod3, gf, w_down)


def kernel(x, c, positions, w_ada, b_ada, g_norm1, g_norm2, w_in, b_gates, w_conv, b_conv, g_mlstm_out,
           sink, g_attn_out, w_out, w_gate, w_up, w_down, g_final):
    B, S, D = x.shape
    depth = w_ada.shape[0]
    H = MLSTM_HEADS
    MW = D // 2
    KW = ATTN_KV_HEADS * ATTN_HEAD_DIM
    NG = 4 * H
    L = MLSTM_CHUNK
    nc = S // L
    g0 = 4 * MW
    x2 = x.reshape(B * S, D)
    pos2 = positions.reshape(B * S, 1)
    for l in range(depth):
        mod3 = _ada(c, w_ada[l], b_ada[l]).reshape(B, 1, 6 * D)
        wl = w_in[l].astype(BF16)
        a0 = g0 + NG
        w_main = jnp.concatenate([wl[:, :2 * MW], wl[:, 3 * MW:g0], wl[:, a0:a0 + MW]], axis=1)
        t_rows = MW + 2 * KW + NG
        t_pad = -t_rows % (2 * BF16_ROWS)
        w_t = jnp.concatenate([wl[:, 2 * MW:3 * MW], wl[:, a0 + MW + KW:], wl[:, a0 + MW:a0 + MW + KW],
                               wl[:, g0:a0], jnp.zeros((D, t_pad), BF16)], axis=1).T
        proj, t3, gates_t = _inproj(x2, mod3, g_norm1[l].reshape(1, D), pos2, w_main, w_t, B, S, NG)
        proj3 = proj.reshape(B, S, proj.shape[1])
        gates = gates_t.reshape(4, H, B, nc, L).transpose(0, 3, 1, 2, 4).reshape(4, nc * H * B, L)
        gbias = jnp.broadcast_to(b_gates[l].reshape(4, 1, H, 1), (4, nc, H, B)).reshape(4, nc * H * B, 1)
        vec = _gatevec(gates, gbias, nc)
        vec5 = vec.reshape(2 * _NVEC, nc, H, B, L).transpose(3, 2, 0, 1, 4)
        m_out = _mlstm(proj3, t3, vec5, w_conv[l], b_conv[l].reshape(1, 2 * MW), g_mlstm_out[l].reshape(1, MW))
        a_out = _attn(proj3, t3, sink[l], g_attn_out[l].reshape(1, MW), q_col0=3 * MW,
                      vt_row0=MW, kt_row0=MW + KW)
        x2, h2 = _outproj(m_out.reshape(B * S, MW), a_out.reshape(B * S, MW), x2, mod3,
                          g_norm2[l].reshape(1, D), w_out[l].astype(BF16), S)
        assert l == depth - 1, "final norm is fused into the last layer's FFN"
        u2 = _ffn_up(h2, w_gate[l], w_up[l])
        x2 = _ffn_down(u2, x2, mod3, g_final.reshape(1, D), w_down[l].astype(BF16), S)
    return x2.reshape(B, S, D)
```

```python
import functools

import jax
import jax.numpy as jnp
from jax import lax
from jax.experimental import pallas as pl
from jax.experimental.pallas import tpu as pltpu

F32 = jnp.float32
BF16 = jnp.bfloat16

EPS = 1e-6
NEG_INF = -1e30
ROPE_THETA = 10000.0

MLSTM_HEADS = 4
MLSTM_CHUNK = 256
ATTN_HEAD_DIM = 128
ATTN_KV_HEADS = 2
WINDOW = 128
ATTN_BLOCK = 128

LANES = 128
BF16_ROWS = 16
VMEM_LIMIT = 56 * 1024 * 1024

_NT = (((1,), (1,)), ((), ()))


def _sigmoid(x):
    return 1.0 / (1.0 + jnp.exp(-x))


def _params(sem, vmem=VMEM_LIMIT):
    return pltpu.CompilerParams(dimension_semantics=sem, vmem_limit_bytes=vmem)


def _split_bf16(v):
    hi = v.astype(BF16)
    return hi, (v - hi.astype(F32)).astype(BF16)


def _ada_kernel(c_ref, w_ref, b_ref, o_ref):
    c = c_ref[...]
    s_hi, s_lo = _split_bf16(c * _sigmoid(c))
    w_hi, w_lo = _split_bf16(w_ref[...])
    dot = functools.partial(jnp.dot, preferred_element_type=F32)
    o_ref[...] = dot(s_hi, w_hi) + (dot(s_lo, w_hi) + dot(s_hi, w_lo)) + b_ref[...]


def _ada(c, w_ada, b_ada, tn=1024):
    B, D = c.shape
    N = w_ada.shape[1]
    return pl.pallas_call(
        _ada_kernel,
        grid=(N // tn,),
        in_specs=[pl.BlockSpec((B, D), lambda j: (0, 0)),
                  pl.BlockSpec((D, tn), lambda j: (0, j)),
                  pl.BlockSpec((1, tn), lambda j: (0, j))],
        out_specs=pl.BlockSpec((B, tn), lambda j: (0, j)),
        out_shape=jax.ShapeDtypeStruct((B, N), F32),
        compiler_params=_params(("arbitrary",)),
        name="ada",
    )(c, w_ada, b_ada.reshape(1, N))


def _rope(t, cos2, sin2):
    return t * cos2 + pltpu.roll(t, ATTN_HEAD_DIM // 2, 1) * sin2


PREP_SLABS = 4
PREP_STEP0 = 2


def _inproj_kernel(x_hbm, mod_ref, g_ref, pos_ref, inv_ref, sgn_ref, w_ref, wq_ref, wt_ref,
                   o_ref, t_ref, gt_ref, xbuf, h_scr, cos_scr, sin_scr, cost_scr, sint_scr, sem,
                   *, D, tm, n_tiles, n_main, rope_j0, n_t, ak_rows, gate_rows):
    r = pl.program_id(0)
    j = pl.program_id(1)
    hd = ATTN_HEAD_DIM
    rows = tm // PREP_SLABS
    ps = r % 2
    cs = 1 - ps
    tile_p = jnp.minimum(r, n_tiles - 1)

    def x_copy():
        return pltpu.make_async_copy(x_hbm.at[pl.ds(pl.multiple_of(tile_p * tm, tm), tm), :], xbuf, sem)

    @pl.when(j == 0)
    def _():
        x_copy().start()

    @pl.when(j == PREP_STEP0 - 1)
    def _():
        x_copy().wait()

    def prep_slab():
        k = j - PREP_STEP0
        r0 = pl.multiple_of(k * rows, rows)
        x = xbuf[pl.ds(r0, rows), :]
        sh = mod_ref[0, :, 0:D]
        gsc = g_ref[...] * (1.0 + mod_ref[0, :, D:2 * D])
        h_scr[ps, pl.ds(r0, rows), :] = (
            x * lax.rsqrt(jnp.mean(x * x, axis=-1, keepdims=True) + EPS) * gsc + sh).astype(BF16)
        ang = pos_ref[pl.ds(r0, rows), :].astype(F32) * inv_ref[...]
        cos2 = jnp.cos(ang)
        sin2 = jnp.sin(ang) * sgn_ref[...]
        cos_scr[ps, pl.ds(r0, rows), :] = cos2
        sin_scr[ps, pl.ds(r0, rows), :] = sin2
        cost_scr[ps, k] = cos2.T
        sint_scr[ps, k] = sin2.T

    def plain():
        o_ref[...] = jnp.dot(h_scr[cs], w_ref[...], preferred_element_type=F32).astype(BF16)

    def roped():
        acc = jnp.dot(h_scr[cs], wq_ref[...], preferred_element_type=F32)
        cos2 = cos_scr[cs]
        sin2 = sin_scr[cs]
        parts = [_rope(acc[:, hh * hd:(hh + 1) * hd], cos2, sin2) for hh in range(acc.shape[1] // hd)]
        o_ref[...] = jnp.concatenate(parts, axis=1).astype(BF16)

    tt = t_ref.shape[1]

    def transposed(ti):
        acc = lax.dot_general(wt_ref[...], h_scr[cs], _NT, preferred_element_type=F32)
        lo, hi = ti * tt, (ti + 1) * tt
        pieces = []
        cur = lo
        for a0 in range(ak_rows[0], ak_rows[1], hd):
            if a0 < lo or a0 + hd > hi:
                assert a0 + hd <= lo or a0 >= hi, "a k head straddles two transposed tiles"
                continue
            if a0 > cur:
                pieces.append(acc[cur - lo:a0 - lo, :])
            t = acc[a0 - lo:a0 - lo + hd, :]
            rot = jnp.concatenate([t[hd // 2:, :], t[:hd // 2, :]], axis=0)
            cost = jnp.concatenate([cost_scr[cs, k] for k in range(PREP_SLABS)], axis=1)
            sint = jnp.concatenate([sint_scr[cs, k] for k in range(PREP_SLABS)], axis=1)
            pieces.append(t * cost + rot * sint)
            cur = a0 + hd
        if cur < hi:
            pieces.append(acc[cur - lo:, :])
        t_ref[0] = jnp.concatenate(pieces, axis=0).astype(BF16) if len(pieces) > 1 else acc.astype(BF16)
        if lo <= gate_rows[0] and gate_rows[1] <= hi:
            gt_ref[...] = acc[gate_rows[0] - lo:gate_rows[1] - lo, :]

    mult = r > 0
    both = jnp.logical_and

    @pl.when(both(mult, j < PREP_STEP0))
    def _():
        plain()

    @pl.when(both(mult, both(j >= PREP_STEP0, j < rope_j0)))
    def _():
        plain()
        prep_slab()

    @pl.when(both(mult, both(j >= rope_j0, j < n_main)))
    def _():
        roped()
        prep_slab()

    for ti in range(n_t):
        @pl.when(both(mult, j == n_main + ti))
        def _(ti=ti):
            transposed(ti)
            prep_slab()

    @pl.when(both(jnp.logical_not(mult), j >= PREP_STEP0))
    def _():
        prep_slab()


def _inproj(x2, mod3, g1, pos2, w_all, w_aq, w_t, B, S, NG, tm=1024, tn=1024, n_t=2):
    N, D = x2.shape
    P = 4 * (D // 2)
    TR = w_t.shape[0]
    hd = ATTN_HEAD_DIM
    half = hd // 2
    inv = ROPE_THETA ** (-jnp.arange(half, dtype=F32) * 2.0 / hd)
    inv2 = jnp.concatenate([inv, inv]).reshape(1, hd)
    sgn = jnp.concatenate([-jnp.ones((half,), F32), jnp.ones((half,), F32)]).reshape(1, hd)
    MW = D // 2
    KW = ATTN_KV_HEADS * hd
    aq0 = 3 * MW
    ak_rows = (MW + KW, MW + 2 * KW)
    gate_rows = (MW + 2 * KW, MW + 2 * KW + NG)
    tt = TR // n_t
    n_tiles = N // tm
    n_main = P // tn
    n_j = n_main + n_t
    rope_j0 = aq0 // tn
    assert aq0 % tn == 0 and MW == tn and w_aq.shape == (D, MW) and TR % n_t == 0 and tt % BF16_ROWS == 0
    assert gate_rows[1] <= TR and tm % PREP_SLABS == 0 and n_j == PREP_STEP0 + PREP_SLABS
    assert PREP_STEP0 <= rope_j0
    kern = functools.partial(_inproj_kernel, D=D, tm=tm, n_tiles=n_tiles, n_main=n_main, rope_j0=rope_j0,
                             n_t=n_t, ak_rows=ak_rows, gate_rows=gate_rows)
    per_b = S // tm
    tile_p = lambda r: jnp.minimum(r, n_tiles - 1)
    tile_c = lambda r: jnp.maximum(r - 1, 0)
    live = lambda r: jnp.minimum(r, 1)
    main_j = lambda r, j: jnp.minimum(j, n_main - 1) * live(r)
    t_j = lambda r, j: jnp.maximum(j - n_main, 0) * live(r)
    return pl.pallas_call(
        kern,
        grid=(n_tiles + 1, n_j),
        in_specs=[pl.BlockSpec(memory_space=pl.ANY),
                  pl.BlockSpec((1, 1, mod3.shape[2]), lambda r, j: (tile_p(r) // per_b, 0, 0)),
                  pl.BlockSpec((1, D), lambda r, j: (0, 0)),
                  pl.BlockSpec((tm, 1), lambda r, j: (tile_p(r), 0)),
                  pl.BlockSpec((1, hd), lambda r, j: (0, 0)),
                  pl.BlockSpec((1, hd), lambda r, j: (0, 0)),
                  pl.BlockSpec((D, tn), lambda r, j: (0, jnp.where(j < 2 * MW // tn, j, 3 * MW // tn))),
                  pl.BlockSpec((D, tn), lambda r, j: (0, 0)),
                  pl.BlockSpec((tt, D), lambda r, j: (jnp.maximum(j - n_main, 0), 0))],
        out_specs=[pl.BlockSpec((tm, tn), lambda r, j: (tile_c(r), main_j(r, j))),
                   pl.BlockSpec((1, tt, tm), lambda r, j: (tile_c(r) // per_b, t_j(r, j), tile_c(r) % per_b)),
                   pl.BlockSpec((NG, tm), lambda r, j: (0, tile_c(r)))],
        out_shape=[jax.ShapeDtypeStruct((N, P), BF16),
                   jax.ShapeDtypeStruct((B, TR, S), BF16),
                   jax.ShapeDtypeStruct((NG, N), F32)],
        scratch_shapes=[pltpu.VMEM((tm, D), F32),
                        pltpu.VMEM((2, tm, D), BF16),
                        pltpu.VMEM((2, tm, hd), F32),
                        pltpu.VMEM((2, tm, hd), F32),
                        pltpu.VMEM((2, PREP_SLABS, hd, tm // PREP_SLABS), F32),
                        pltpu.VMEM((2, PREP_SLABS, hd, tm // PREP_SLABS), F32),
                        pltpu.SemaphoreType.DMA(())],
        compiler_params=_params(("arbitrary", "arbitrary")),
        name="inproj",
    )(x2, mod3, g1, pos2, inv2, sgn, w_all, w_aq, w_t)


_R, _MT, _A, _LIM, _E, _SC = range(6)
_NVEC = 6


def _log_sigmoid(x):
    return jnp.minimum(x, 0.0) - jnp.log1p(jnp.exp(-jnp.abs(x)))


def _lane_scan(x, reverse, op, ident):
    L = x.shape[1]
    lane = lax.broadcasted_iota(jnp.int32, x.shape, 1)
    sh = 1
    while sh < L:
        if reverse:
            x = op(x, jnp.where(lane < L - sh, pltpu.roll(x, L - sh, 1), ident))
        else:
            x = op(x, jnp.where(lane >= sh, pltpu.roll(x, sh, 1), ident))
        sh *= 2
    return x


def _chunk_scan(x, reverse, op, ident):
    n = x.shape[1] // LANES
    parts = [_lane_scan(x[:, i * LANES:(i + 1) * LANES], reverse, op, ident) for i in range(n)]
    order = range(n - 2, -1, -1) if reverse else range(1, n)
    for i in order:
        prev = parts[i + 1][:, 0:1] if reverse else parts[i - 1][:, LANES - 1:LANES]
        parts[i] = op(parts[i], prev)
    return jnp.concatenate(parts, axis=1)


def _gate_vectors(li, lf_pre, reverse, nc):
    rows, Lc = li.shape
    hb = rows // nc
    lf = _log_sigmoid(lf_pre)
    b = _chunk_scan(lf, reverse, jnp.add, 0.0)
    r = li - b
    rcm = _chunk_scan(r, reverse, jnp.maximum, NEG_INF)
    end = slice(0, 1) if reverse else slice(Lc - 1, Lc)
    bl = jnp.broadcast_to(b[:, end], (rows, Lc))
    rmax = jnp.broadcast_to(rcm[:, end], (rows, Lc))
    m = jnp.zeros((hb, Lc), F32)
    m_parts = [None] * nc
    for j in (range(nc - 1, -1, -1) if reverse else range(nc)):
        m_parts[j] = m
        sl = slice(j * hb, (j + 1) * hb)
        m = bl[sl, :] + jnp.maximum(m, rmax[sl, :])
    m_all = jnp.concatenate(m_parts, axis=0)
    mt = jnp.maximum(m_all, rcm)
    m_end = jnp.maximum(m_all, rmax)
    return r, mt, jnp.exp(m_all - mt), jnp.exp(-(b + mt)), jnp.exp(r - m_end), jnp.exp(m_all - m_end)


def _gatevec_kernel(g_ref, gb_ref, o_ref, *, nc):
    for dr in range(2):
        li = g_ref[2 * dr] + gb_ref[2 * dr]
        lf_pre = g_ref[2 * dr + 1] + gb_ref[2 * dr + 1]
        for idx, val in enumerate(_gate_vectors(li, lf_pre, dr == 1, nc)):
            o_ref[dr * _NVEC + idx] = val


def _gatevec(gates, gbias, nc):
    _, rows, Lc = gates.shape
    return pl.pallas_call(
        functools.partial(_gatevec_kernel, nc=nc),
        out_shape=jax.ShapeDtypeStruct((2 * _NVEC, rows, Lc), F32),
        compiler_params=pltpu.CompilerParams(vmem_limit_bytes=VMEM_LIMIT),
        name="gatevec",
    )(gates, gbias)


CONV_ROWS = 128


def _mlstm_kernel(q_ref, k_ref, vt_ref, mo_ref, vec_ref, wq_ref, wk_ref, bq_ref, bk_ref, go_ref,
                  o_ref, pad_scr, qs_scr, big_scr, vt_scr, ns_scr, ct_scr, n_scr, p0_scr, p1_scr, *, S, d, Lc, CW):
    nc = S // Lc
    half_w = CW // 2
    PADR = 8
    CR = CONV_ROWS
    N0 = Lc
    C0 = Lc + BF16_ROWS

    zero_rows = jnp.zeros((PADR, d), F32)

    def conv_silu(src_ref, w_ref, b_ref, store, scale):
        pad_scr[0:PADR, :] = zero_rows
        pad_scr[PADR + S:2 * PADR + S, :] = zero_rows

        def fill(j, c):
            r0 = pl.multiple_of(j * CR, CR)
            pad_scr[pl.ds(PADR + r0, CR), :] = src_ref[0, pl.ds(r0, CR), :].astype(F32)
            return c
        lax.fori_loop(0, S // CR, fill, 0)
        w = w_ref[...]
        bias = b_ref[...]

        def body(j, c):
            r0 = pl.multiple_of(j * CR, CR)
            win = pad_scr[pl.ds(r0, CR + 2 * PADR), :]
            y = bias
            for i in range(CW):
                o = PADR - half_w + i
                y = y + win[o:o + CR, :] * w[i:i + 1, :]
            y = y * _sigmoid(y)
            if scale != 1.0:
                y = y * scale
            store(j, r0, y.astype(BF16))
            return c
        lax.fori_loop(0, S // CR, body, 0)

    def store_q(j, r0, y):
        qs_scr[pl.ds(r0, CR), :] = y

    per = Lc // CR

    def store_k(j, r0, y):
        big_scr[j // per, pl.ds(pl.multiple_of((j % per) * CR, CR), CR), :] = y

    conv_silu(q_ref, wq_ref, bq_ref, store_q, 1.0)
    conv_silu(k_ref, wk_ref, bk_ref, store_k, d ** -0.5)

    for j in range(nc):
        vt_scr[j] = vt_ref[0, :, j * Lc:(j + 1) * Lc]

    ns_scr[...] = jnp.zeros(ns_scr.shape, F32)
    ct_scr[...] = jnp.zeros(ct_scr.shape, F32)
    n_scr[...] = jnp.zeros(n_scr.shape, F32)

    def state_step(dr, jc):
        base = dr * _NVEC
        e_row = vec_ref[0, 0, base + _E, pl.ds(jc, 1), :]
        sc = vec_ref[0, 0, base + _SC, pl.ds(jc, 1), :]
        lhs = jnp.concatenate([vt_scr[jc].astype(F32) * e_row, jnp.broadcast_to(e_row, (BF16_ROWS, Lc))], axis=0)
        upd = jnp.dot(lhs.astype(BF16), big_scr[jc, 0:Lc, :], preferred_element_type=F32)
        ct = ct_scr[dr]
        n = n_scr[dr]
        big_scr[jc, C0 + dr * d:C0 + (dr + 1) * d, :] = ct.astype(BF16)
        ns_scr[jc, dr:dr + 1, :] = n[0:1, :]
        ct_scr[dr] = sc * ct + upd[0:d, :]
        n_scr[dr] = sc * n + upd[d:d + 8, :]

    for j in range(nc):
        state_step(0, j)
        state_step(1, nc - 1 - j)
    for j in range(nc):
        big_scr[j, N0:N0 + BF16_ROWS, :] = ns_scr[j].astype(BF16)

    rr = lax.broadcasted_iota(jnp.int32, (Lc, Lc), 0)
    cc = lax.broadcasted_iota(jnp.int32, (Lc, Lc), 1)
    r1 = lax.broadcasted_iota(jnp.int32, (LANES, LANES), 0)
    c1 = lax.broadcasted_iota(jnp.int32, (LANES, LANES), 1)
    eye = r1 == c1
    gout = go_ref[...]

    def to_col(row):
        return jnp.concatenate(
            [jnp.sum(jnp.where(eye, row[:, i * LANES:(i + 1) * LANES], 0.0), axis=1, keepdims=True)
             for i in range(Lc // LANES)], axis=0)

    hrows = C0 + d

    def head(j, p_scr):
        qj = qs_scr[pl.ds(pl.multiple_of(j * Lc, Lc), Lc), :]
        p_scr[0:hrows, :] = lax.dot_general(big_scr[j, 0:hrows, :], qj, _NT, preferred_element_type=F32)
        p_scr[hrows:hrows + d, :] = lax.dot_general(big_scr[j, hrows:hrows + d, :], qj, _NT,
                                                    preferred_element_type=F32)

    def tail(j, p_scr):
        r0 = pl.multiple_of(j * Lc, Lc)
        st = p_scr[0:Lc, :]
        qkw = []
        inv_den = []
        a_rows = []
        for dr in range(2):
            base = dr * _NVEC
            r_row = vec_ref[0, 0, base + _R, pl.ds(j, 1), :]
            mt_row = vec_ref[0, 0, base + _MT, pl.ds(j, 1), :]
            a_row = vec_ref[0, 0, base + _A, pl.ds(j, 1), :]
            lim_row = vec_ref[0, 0, base + _LIM, pl.ds(j, 1), :]
            valid = (rr <= cc) if dr == 0 else (rr >= cc)
            w = jnp.exp(jnp.where(valid, to_col(r_row) - mt_row, NEG_INF))
            qkw_d = st * w
            den = a_row * p_scr[N0 + dr:N0 + dr + 1, :] + jnp.sum(qkw_d, axis=0, keepdims=True)
            inv_den.append(1.0 / jnp.maximum(jnp.abs(den), lim_row))
            a_rows.append(a_row)
            qkw.append(qkw_d.astype(BF16))
        intra = jnp.dot(vt_scr[j], jnp.concatenate(qkw, axis=1), preferred_element_type=F32)
        hs = None
        for dr in range(2):
            inter = p_scr[C0 + dr * d:C0 + (dr + 1) * d, :]
            hd = (a_rows[dr] * inter + intra[:, dr * Lc:(dr + 1) * Lc]) * inv_den[dr]
            hs = hd if hs is None else hs + hd
        mu = jnp.mean(hs, axis=0, keepdims=True)
        hc = hs - mu
        hn = (hc * lax.rsqrt(jnp.mean(hc * hc, axis=0, keepdims=True) + EPS)).T
        mo = mo_ref[0, pl.ds(r0, Lc), :].astype(F32)
        o_ref[0, pl.ds(r0, Lc), :] = (_sigmoid(mo) * hn * gout).astype(BF16)

    bufs = (p0_scr, p1_scr)
    head(0, bufs[0])
    for j in range(nc):
        if j + 1 < nc:
            head(j + 1, bufs[(j + 1) % 2])
        tail(j, bufs[j % 2])


def _mlstm(proj3, vt3, vec5, w_conv, b_conv, g_out):
    B, S, _ = proj3.shape
    H = MLSTM_HEADS
    d = g_out.shape[1] // H
    Lc = MLSTM_CHUNK
    nc = S // Lc
    CW = w_conv.shape[0]
    assert Lc % LANES == 0 and Lc % CONV_ROWS == 0 and d == Lc and CW // 2 <= 8 and nc % 2 == 0 and nc >= 4
    kern = functools.partial(_mlstm_kernel, S=S, d=d, Lc=Lc, CW=CW)
    col = lambda off: (lambda b, h: (b, 0, off + h))
    return pl.pallas_call(
        kern,
        grid=(B, H),
        in_specs=[pl.BlockSpec((1, S, d), col(0)),
                  pl.BlockSpec((1, S, d), col(H)),
                  pl.BlockSpec((1, d, S), lambda b, h: (b, h, 0)),
                  pl.BlockSpec((1, S, d), col(2 * H)),
                  pl.BlockSpec((1, 1, 2 * _NVEC, nc, Lc), lambda b, h: (b, h, 0, 0, 0)),
                  pl.BlockSpec((CW, d), lambda b, h: (0, h)),
                  pl.BlockSpec((CW, d), lambda b, h: (0, H + h)),
                  pl.BlockSpec((1, d), lambda b, h: (0, h)),
                  pl.BlockSpec((1, d), lambda b, h: (0, H + h)),
                  pl.BlockSpec((1, d), lambda b, h: (0, h))],
        out_specs=pl.BlockSpec((1, S, d), lambda b, h: (b, 0, h)),
        out_shape=jax.ShapeDtypeStruct((B, S, H * d), BF16),
        scratch_shapes=[pltpu.VMEM((S + 16, d), F32),
                        pltpu.VMEM((S, d), BF16),
                        pltpu.VMEM((nc, Lc + BF16_ROWS + 2 * d, d), BF16),
                        pltpu.VMEM((nc, d, Lc), BF16),
                        pltpu.VMEM((nc, BF16_ROWS, d), F32),
                        pltpu.VMEM((2, d, d), F32),
                        pltpu.VMEM((2, 8, d), F32),
                        pltpu.VMEM((Lc + BF16_ROWS + 2 * d, Lc), F32),
                        pltpu.VMEM((Lc + BF16_ROWS + 2 * d, Lc), F32)],
        compiler_params=_params(("arbitrary", "arbitrary")),
        name="mlstm",
    )(proj3, proj3, vt3, proj3, vec5, w_conv, w_conv, b_conv, b_conv, g_out)


LOG2E = 1.4426950408889634
ATTN_QBLOCKS = 4


def _attn_kernel(sink_ref, q_ref, ktp_ref, ktc_ref, ktn_ref, vtp_ref, vtc_ref, vtn_ref, g_ref, o_ref,
                 s0_scr, s1_scr, *, H, G, QB):
    n4 = pl.program_id(1)
    last4 = pl.num_programs(1) - 1
    blk = ATTN_BLOCK
    hd = ATTN_HEAD_DIM
    R = H // G
    c1 = (hd ** -0.5) * LOG2E
    rr = lax.broadcasted_iota(jnp.int32, (blk, blk), 0)
    cc = lax.broadcasted_iota(jnp.int32, (blk, blk), 1)
    ones = jnp.ones((hd, 3 * blk), BF16)

    def kv_window(i, g, prev_ref, cur_ref, next_ref):
        rs = slice(g * hd, (g + 1) * hd)
        lo = prev_ref[0, rs, :] if i == 0 else cur_ref[0, rs, (i - 1) * blk:i * blk]
        hi = next_ref[0, rs, :] if i == QB - 1 else cur_ref[0, rs, (i + 1) * blk:(i + 2) * blk]
        return jnp.concatenate([lo, cur_ref[0, rs, i * blk:(i + 1) * blk], hi], axis=1)

    def head(i, s_scr):
        q = q_ref[0, i * blk:(i + 1) * blk, :]
        for g in range(G):
            qg = jnp.concatenate([q[:, (g * R + r) * hd:(g * R + r + 1) * hd] for r in range(R)], axis=0)
            s_scr[g] = jnp.dot(qg, kv_window(i, g, ktp_ref, ktc_ref, ktn_ref), preferred_element_type=F32)

    def tail(i, s_scr):
        m_prev = (cc >= rr) if i > 0 else jnp.logical_and(cc >= rr, n4 > 0)
        m_next = (cc <= rr) if i < QB - 1 else jnp.logical_and(cc <= rr, n4 < last4)
        outs = []
        for g in range(G):
            vt = kv_window(i, g, vtp_ref, vtc_ref, vtn_ref)
            vt_aug = jnp.concatenate([vt, ones], axis=0)
            ps = []
            extras = []
            for r in range(R):
                sr = s_scr[g, r * blk:(r + 1) * blk, :] * c1
                t0 = jnp.where(m_prev, sr[:, 0:blk], NEG_INF)
                t1 = sr[:, blk:2 * blk]
                t2 = jnp.where(m_next, sr[:, 2 * blk:3 * blk], NEG_INF)
                sink2 = sink_ref[g * R + r] * LOG2E
                m = jnp.maximum(jnp.max(jnp.maximum(jnp.maximum(t0, t1), t2), axis=1, keepdims=True), sink2)
                ps.append(jnp.concatenate([jnp.exp2(t0 - m), jnp.exp2(t1 - m), jnp.exp2(t2 - m)],
                                          axis=1).astype(BF16))
                extras.append(jnp.exp2(sink2 - m))
            oa = lax.dot_general(jnp.concatenate(ps, axis=0), vt_aug, _NT, preferred_element_type=F32)
            for r in range(R):
                o_r = oa[r * blk:(r + 1) * blk, :]
                outs.append(o_r[:, 0:hd] / (o_r[:, hd:2 * hd] + extras[r]))
        oall = jnp.concatenate(outs, axis=1)
        y = oall * lax.rsqrt(jnp.mean(oall * oall, axis=-1, keepdims=True) + EPS) * g_ref[...]
        o_ref[0, i * blk:(i + 1) * blk, :] = y.astype(BF16)

    bufs = (s0_scr, s1_scr)
    head(0, bufs[0])
    for i in range(QB):
        if i + 1 < QB:
            head(i + 1, bufs[(i + 1) % 2])
        tail(i, bufs[i % 2])


def _attn(proj3, t3, sink, g_attn, q_col0, vt_row0, kt_row0):
    B, S, _ = proj3.shape
    blk = ATTN_BLOCK
    QB = ATTN_QBLOCKS
    nb = S // blk
    H = sink.shape[0]
    G = ATTN_KV_HEADS
    AW = H * ATTN_HEAD_DIM
    KW = G * ATTN_HEAD_DIM
    assert WINDOW == blk and q_col0 % AW == 0 and vt_row0 % KW == 0 and kt_row0 % KW == 0 and nb % QB == 0
    qb, vb, kb = q_col0 // AW, vt_row0 // KW, kt_row0 // KW
    kern = functools.partial(_attn_kernel, H=H, G=G, QB=QB)
    prev = lambda rb: (lambda b, n: (b, rb, jnp.maximum(n * QB - 1, 0)))
    cur = lambda rb: (lambda b, n: (b, rb, n))
    nxt = lambda rb: (lambda b, n: (b, rb, jnp.minimum(n * QB + QB, nb - 1)))
    return pl.pallas_call(
        kern,
        grid=(B, nb // QB),
        in_specs=[pl.BlockSpec(memory_space=pltpu.SMEM),
                  pl.BlockSpec((1, QB * blk, AW), lambda b, n: (b, n, qb)),
                  pl.BlockSpec((1, KW, blk), prev(kb)),
                  pl.BlockSpec((1, KW, QB * blk), cur(kb)),
                  pl.BlockSpec((1, KW, blk), nxt(kb)),
                  pl.BlockSpec((1, KW, blk), prev(vb)),
                  pl.BlockSpec((1, KW, QB * blk), cur(vb)),
                  pl.BlockSpec((1, KW, blk), nxt(vb)),
                  pl.BlockSpec((1, AW), lambda b, n: (0, 0))],
        out_specs=pl.BlockSpec((1, QB * blk, AW), lambda b, n: (b, n, 0)),
        out_shape=jax.ShapeDtypeStruct((B, S, AW), BF16),
        scratch_shapes=[pltpu.VMEM((G, (H // G) * blk, 3 * blk), F32),
                        pltpu.VMEM((G, (H // G) * blk, 3 * blk), F32)],
        compiler_params=_params(("arbitrary", "arbitrary")),
        name="attn",
    )(sink, proj3, t3, t3, t3, t3, t3, t3, g_attn)


def _outproj_kernel(m_ref, a_ref, x_ref, mod_ref, g2_ref, wt_ref, wb_ref, o_ref, h_ref, *, D):
    acc = jnp.dot(m_ref[...], wt_ref[...], preferred_element_type=F32)
    acc = acc + jnp.dot(a_ref[...], wb_ref[...], preferred_element_type=F32)
    gt = mod_ref[0, :, 2 * D:3 * D]
    x1 = x_ref[...] + gt * acc
    o_ref[...] = x1
    sh = mod_ref[0, :, 3 * D:4 * D]
    gsc = g2_ref[...] * (1.0 + mod_ref[0, :, 4 * D:5 * D])
    h_ref[...] = (x1 * lax.rsqrt(jnp.mean(x1 * x1, axis=-1, keepdims=True) + EPS) * gsc + sh).astype(BF16)


def _outproj(m2, a2, x2, mod3, g2, w_o, S, tm=512):
    N, D = x2.shape
    K = m2.shape[1]
    per_b = S // tm
    return pl.pallas_call(
        functools.partial(_outproj_kernel, D=D),
        grid=(N // tm,),
        in_specs=[pl.BlockSpec((tm, K), lambda i: (i, 0)),
                  pl.BlockSpec((tm, K), lambda i: (i, 0)),
                  pl.BlockSpec((tm, D), lambda i: (i, 0)),
                  pl.BlockSpec((1, 1, mod3.shape[2]), lambda i: (i // per_b, 0, 0)),
                  pl.BlockSpec((1, D), lambda i: (0, 0)),
                  pl.BlockSpec((K, D), lambda i: (0, 0)),
                  pl.BlockSpec((K, D), lambda i: (1, 0))],
        out_specs=[pl.BlockSpec((tm, D), lambda i: (i, 0)),
                   pl.BlockSpec((tm, D), lambda i: (i, 0))],
        out_shape=[jax.ShapeDtypeStruct((N, D), F32),
                   jax.ShapeDtypeStruct((N, D), BF16)],
        compiler_params=_params(("arbitrary",)),
        name="outproj",
    )(m2, a2, x2, mod3, g2, w_o, w_o)


def _ffn_up_kernel(h_ref, wg_ref, wu_ref, o_ref):
    hb = h_ref[...]
    gate = jnp.dot(hb, wg_ref[...].astype(BF16), preferred_element_type=F32)
    up = jnp.dot(hb, wu_ref[...].astype(BF16), preferred_element_type=F32)
    o_ref[...] = (gate * _sigmoid(gate) * up).astype(BF16)


def _ffn_up(h2, w_gate, w_up, tm=2048, tf=512):
    N, D = h2.shape
    FF = w_gate.shape[1]
    return pl.pallas_call(
        _ffn_up_kernel,
        grid=(N // tm, FF // tf),
        in_specs=[pl.BlockSpec((tm, D), lambda i, j: (i, 0)),
                  pl.BlockSpec((D, tf), lambda i, j: (0, j)),
                  pl.BlockSpec((D, tf), lambda i, j: (0, j))],
        out_specs=pl.BlockSpec((tm, tf), lambda i, j: (i, j)),
        out_shape=jax.ShapeDtypeStruct((N, FF), BF16),
        compiler_params=_params(("arbitrary", "arbitrary")),
        name="ffn_up",
    )(h2, w_gate, w_up)


def _ffn_down_kernel(u_ref, x_hbm, mod_ref, gf_ref, wd_ref, o_ref, xbuf, sem, *, D, tm):
    i = pl.program_id(0)
    j = pl.program_id(1)

    def x_copy():
        return pltpu.make_async_copy(x_hbm.at[pl.ds(pl.multiple_of(i * tm, tm), tm), :], xbuf, sem)

    @pl.when(j == 0)
    def _():
        x_copy().start()
        o_ref[...] = jnp.zeros(o_ref.shape, F32)

    last = pl.num_programs(1) - 1

    @pl.when(j < last)
    def _():
        o_ref[...] += jnp.dot(u_ref[...], wd_ref[...], preferred_element_type=F32)

    @pl.when(j == last)
    def _():
        x_copy().wait()
        gt = mod_ref[0, :, 5 * D:6 * D]
        for rows in (slice(0, tm // 2), slice(tm // 2, tm)):
            acc = o_ref[rows, :] + jnp.dot(u_ref[rows, :], wd_ref[...], preferred_element_type=F32)
            y = xbuf[rows, :] + gt * acc
            o_ref[rows, :] = y * lax.rsqrt(jnp.mean(y * y, axis=-1, keepdims=True) + EPS) * gf_ref[...]


def _ffn_down(u2, x2, mod3, gf, w_down, S, tm=1024, tk=1408):
    N, D = x2.shape
    FF = u2.shape[1]
    per_b = S // tm
    return pl.pallas_call(
        functools.partial(_ffn_down_kernel, D=D, tm=tm),
        grid=(N // tm, FF // tk),
        in_specs=[pl.BlockSpec((tm, tk), lambda i, j: (i, j)),
                  pl.BlockSpec(memory_space=pl.ANY),
                  pl.BlockSpec((1, 1, mod3.shape[2]), lambda i, j: (i // per_b, 0, 0)),
                  pl.BlockSpec((1, D), lambda i, j: (0, 0)),
                  pl.BlockSpec((tk, D), lambda i, j: (j, 0))],
        out_specs=pl.BlockSpec((tm, D), lambda i, j: (i, 0)),
        out_shape=jax.ShapeDtypeStruct((N, D), F32),
        scratch_shapes=[pltpu.VMEM((tm, D), F32), pltpu.SemaphoreType.DMA(())],
        compiler_params=_params(("arbitrary", "arbitrary")),
        name="ffn_down",
    )(u2, x2, mod3, gf, w_down)


def kernel(x, c, positions, w_ada, b_ada, g_norm1, g_norm2, w_in, b_gates, w_conv, b_conv, g_mlstm_out,
           sink, g_attn_out, w_out, w_gate, w_up, w_down, g_final):
    B, S, D = x.shape
    depth = w_ada.shape[0]
    H = MLSTM_HEADS
    MW = D // 2
    KW = ATTN_KV_HEADS * ATTN_HEAD_DIM
    NG = 4 * H
    L = MLSTM_CHUNK
    nc = S // L
    g0 = 4 * MW
    x2 = x.reshape(B * S, D)
    pos2 = positions.reshape(B * S, 1)
    for l in range(depth):
        mod3 = _ada(c, w_ada[l], b_ada[l]).reshape(B, 1, 6 * D)
        wl = w_in[l].astype(BF16)
        a0 = g0 + NG
        t_rows = MW + 2 * KW + NG
        t_pad = -t_rows % (2 * BF16_ROWS)
        w_t = jnp.concatenate([wl[:, 2 * MW:3 * MW], wl[:, a0 + MW + KW:], wl[:, a0 + MW:a0 + MW + KW],
                               wl[:, g0:a0], jnp.zeros((D, t_pad), BF16)], axis=1).T
        proj, t3, gates_t = _inproj(x2, mod3, g_norm1[l].reshape(1, D), pos2, wl, wl[:, a0:a0 + MW], w_t, B, S, NG)
        proj3 = proj.reshape(B, S, proj.shape[1])
        gates = gates_t.reshape(4, H, B, nc, L).transpose(0, 3, 1, 2, 4).reshape(4, nc * H * B, L)
        gbias = jnp.broadcast_to(b_gates[l].reshape(4, 1, H, 1), (4, nc, H, B)).reshape(4, nc * H * B, 1)
        vec = _gatevec(gates, gbias, nc)
        vec5 = vec.reshape(2 * _NVEC, nc, H, B, L).transpose(3, 2, 0, 1, 4)
        m_out = _mlstm(proj3, t3, vec5, w_conv[l], b_conv[l].reshape(1, 2 * MW), g_mlstm_out[l].reshape(1, MW))
        a_out = _attn(proj3, t3, sink[l], g_attn_out[l].reshape(1, MW), q_col0=3 * MW,
                      vt_row0=MW, kt_row0=MW + KW)
        x2, h2 = _outproj(m_out.reshape(B * S, MW), a_out.reshape(B * S, MW), x2, mod3,
                          g_norm2[l].reshape(1, D), w_out[l].astype(BF16), S)
        assert l == depth - 1, "final norm is fused into the last layer's FFN"
        u2 = _ffn_up(h2, w_gate[l], w_up[l])
        x2 = _ffn_down(u2, x2, mod3, g_final.reshape(1, D), w_down[l].astype(BF16), S)
    return x2.reshape(B, S, D)
```

```python
import functools

import jax
import jax.numpy as jnp
from jax import lax
from jax.experimental import pallas as pl
from jax.experimental.pallas import tpu as pltpu

F32 = jnp.float32
BF16 = jnp.bfloat16

EPS = 1e-6
NEG_INF = -1e30
ROPE_THETA = 10000.0

MLSTM_HEADS = 4
MLSTM_CHUNK = 256
ATTN_HEAD_DIM = 128
ATTN_KV_HEADS = 2
WINDOW = 128
ATTN_BLOCK = 128

LANES = 128
BF16_ROWS = 16
VMEM_LIMIT = 56 * 1024 * 1024

_NT = (((1,), (1,)), ((), ()))


def _sigmoid(x):
    return 1.0 / (1.0 + jnp.exp(-x))


def _params(sem, vmem=VMEM_LIMIT):
    return pltpu.CompilerParams(dimension_semantics=sem, vmem_limit_bytes=vmem)


def _split_bf16(v):
    hi = v.astype(BF16)
    return hi, (v - hi.astype(F32)).astype(BF16)


def _ada_kernel(c_ref, w_ref, b_ref, o_ref):
    c = c_ref[...]
    s_hi, s_lo = _split_bf16(c * _sigmoid(c))
    w_hi, w_lo = _split_bf16(w_ref[...])
    dot = functools.partial(jnp.dot, preferred_element_type=F32)
    o_ref[...] = dot(s_hi, w_hi) + (dot(s_lo, w_hi) + dot(s_hi, w_lo)) + b_ref[...]


def _ada(c, w_ada, b_ada, tn=1024):
    B, D = c.shape
    N = w_ada.shape[1]
    return pl.pallas_call(
        _ada_kernel,
        grid=(N // tn,),
        in_specs=[pl.BlockSpec((B, D), lambda j: (0, 0)),
                  pl.BlockSpec((D, tn), lambda j: (0, j)),
                  pl.BlockSpec((1, tn), lambda j: (0, j))],
        out_specs=pl.BlockSpec((B, tn), lambda j: (0, j)),
        out_shape=jax.ShapeDtypeStruct((B, N), F32),
        compiler_params=_params(("arbitrary",)),
        name="ada",
    )(c, w_ada, b_ada.reshape(1, N))


def _rope(t, cos2, sin2):
    return t * cos2 + pltpu.roll(t, ATTN_HEAD_DIM // 2, 1) * sin2


PREP_SLABS = 4
PREP_STEP0 = 2


def _inproj_kernel(x_hbm, mod_ref, g_ref, pos_ref, inv_ref, sgn_ref, w_ref, wq_ref, wt_ref,
                   o_ref, t_ref, gt_ref, xbuf, h_scr, cos_scr, sin_scr, cost_scr, sint_scr, sem,
                   *, D, tm, n_tiles, n_main, rope_j0, n_t, ak_rows, gate_rows):
    r = pl.program_id(0)
    j = pl.program_id(1)
    hd = ATTN_HEAD_DIM
    rows = tm // PREP_SLABS
    ps = r % 2
    cs = 1 - ps
    tile_p = jnp.minimum(r, n_tiles - 1)

    def x_copy():
        return pltpu.make_async_copy(x_hbm.at[pl.ds(pl.multiple_of(tile_p * tm, tm), tm), :], xbuf, sem)

    @pl.when(j == 0)
    def _():
        x_copy().start()

    @pl.when(j == PREP_STEP0 - 1)
    def _():
        x_copy().wait()

    def prep_slab():
        k = j - PREP_STEP0
        r0 = pl.multiple_of(k * rows, rows)
        x = xbuf[pl.ds(r0, rows), :]
        sh = mod_ref[0, :, 0:D]
        gsc = g_ref[...] * (1.0 + mod_ref[0, :, D:2 * D])
        h_scr[ps, pl.ds(r0, rows), :] = (
            x * lax.rsqrt(jnp.mean(x * x, axis=-1, keepdims=True) + EPS) * gsc + sh).astype(BF16)
        ang = pos_ref[pl.ds(r0, rows), :].astype(F32) * inv_ref[...]
        cos2 = jnp.cos(ang)
        sin2 = jnp.sin(ang) * sgn_ref[...]
        cos_scr[ps, pl.ds(r0, rows), :] = cos2
        sin_scr[ps, pl.ds(r0, rows), :] = sin2
        cost_scr[ps, k] = cos2.T
        sint_scr[ps, k] = sin2.T

    def plain():
        o_ref[...] = jnp.dot(h_scr[cs], w_ref[...], preferred_element_type=F32).astype(BF16)

    def roped():
        acc = jnp.dot(h_scr[cs], wq_ref[...], preferred_element_type=F32)
        cos2 = cos_scr[cs]
        sin2 = sin_scr[cs]
        parts = [_rope(acc[:, hh * hd:(hh + 1) * hd], cos2, sin2) for hh in range(acc.shape[1] // hd)]
        o_ref[...] = jnp.concatenate(parts, axis=1).astype(BF16)

    tt = t_ref.shape[1]

    def transposed(ti):
        acc = lax.dot_general(wt_ref[...], h_scr[cs], _NT, preferred_element_type=F32)
        lo, hi = ti * tt, (ti + 1) * tt
        pieces = []
        cur = lo
        for a0 in range(ak_rows[0], ak_rows[1], hd):
            if a0 < lo or a0 + hd > hi:
                assert a0 + hd <= lo or a0 >= hi, "a k head straddles two transposed tiles"
                continue
            if a0 > cur:
                pieces.append(acc[cur - lo:a0 - lo, :])
            t = acc[a0 - lo:a0 - lo + hd, :]
            rot = jnp.concatenate([t[hd // 2:, :], t[:hd // 2, :]], axis=0)
            cost = jnp.concatenate([cost_scr[cs, k] for k in range(PREP_SLABS)], axis=1)
            sint = jnp.concatenate([sint_scr[cs, k] for k in range(PREP_SLABS)], axis=1)
            pieces.append(t * cost + rot * sint)
            cur = a0 + hd
        if cur < hi:
            pieces.append(acc[cur - lo:, :])
        t_ref[0] = jnp.concatenate(pieces, axis=0).astype(BF16) if len(pieces) > 1 else acc.astype(BF16)
        if lo <= gate_rows[0] and gate_rows[1] <= hi:
            gt_ref[...] = acc[gate_rows[0] - lo:gate_rows[1] - lo, :]

    mult = r > 0
    both = jnp.logical_and

    @pl.when(both(mult, j < PREP_STEP0))
    def _():
        plain()

    @pl.when(both(mult, both(j >= PREP_STEP0, j < rope_j0)))
    def _():
        plain()
        prep_slab()

    @pl.when(both(mult, both(j >= rope_j0, j < n_main)))
    def _():
        roped()
        prep_slab()

    for ti in range(n_t):
        @pl.when(both(mult, j == n_main + ti))
        def _(ti=ti):
            transposed(ti)
            prep_slab()

    @pl.when(both(jnp.logical_not(mult), j >= PREP_STEP0))
    def _():
        prep_slab()


def _inproj(x2, mod3, g1, pos2, w_all, w_aq, w_t, B, S, NG, tm=1024, tn=1024, n_t=2):
    N, D = x2.shape
    P = 4 * (D // 2)
    TR = w_t.shape[0]
    hd = ATTN_HEAD_DIM
    half = hd // 2
    inv = ROPE_THETA ** (-jnp.arange(half, dtype=F32) * 2.0 / hd)
    inv2 = jnp.concatenate([inv, inv]).reshape(1, hd)
    sgn = jnp.concatenate([-jnp.ones((half,), F32), jnp.ones((half,), F32)]).reshape(1, hd)
    MW = D // 2
    KW = ATTN_KV_HEADS * hd
    aq0 = 3 * MW
    ak_rows = (MW + KW, MW + 2 * KW)
    gate_rows = (MW + 2 * KW, MW + 2 * KW + NG)
    tt = TR // n_t
    n_tiles = N // tm
    n_main = P // tn
    n_j = n_main + n_t
    rope_j0 = aq0 // tn
    assert aq0 % tn == 0 and MW == tn and w_aq.shape == (D, MW) and TR % n_t == 0 and tt % BF16_ROWS == 0
    assert gate_rows[1] <= TR and tm % PREP_SLABS == 0 and n_j == PREP_STEP0 + PREP_SLABS
    assert PREP_STEP0 <= rope_j0
    kern = functools.partial(_inproj_kernel, D=D, tm=tm, n_tiles=n_tiles, n_main=n_main, rope_j0=rope_j0,
                             n_t=n_t, ak_rows=ak_rows, gate_rows=gate_rows)
    per_b = S // tm
    tile_p = lambda r: jnp.minimum(r, n_tiles - 1)
    tile_c = lambda r: jnp.maximum(r - 1, 0)
    live = lambda r: jnp.minimum(r, 1)
    main_j = lambda r, j: jnp.minimum(j, n_main - 1) * live(r)
    t_j = lambda r, j: jnp.maximum(j - n_main, 0) * live(r)
    return pl.pallas_call(
        kern,
        grid=(n_tiles + 1, n_j),
        in_specs=[pl.BlockSpec(memory_space=pl.ANY),
                  pl.BlockSpec((1, 1, mod3.shape[2]), lambda r, j: (tile_p(r) // per_b, 0, 0)),
                  pl.BlockSpec((1, D), lambda r, j: (0, 0)),
                  pl.BlockSpec((tm, 1), lambda r, j: (tile_p(r), 0)),
                  pl.BlockSpec((1, hd), lambda r, j: (0, 0)),
                  pl.BlockSpec((1, hd), lambda r, j: (0, 0)),
                  pl.BlockSpec((D, tn), lambda r, j: (0, jnp.where(j < 2 * MW // tn, j, 3 * MW // tn))),
                  pl.BlockSpec((D, tn), lambda r, j: (0, 0)),
                  pl.BlockSpec((tt, D), lambda r, j: (jnp.maximum(j - n_main, 0), 0))],
        out_specs=[pl.BlockSpec((tm, tn), lambda r, j: (tile_c(r), main_j(r, j))),
                   pl.BlockSpec((1, tt, tm), lambda r, j: (tile_c(r) // per_b, t_j(r, j), tile_c(r) % per_b)),
                   pl.BlockSpec((NG, tm), lambda r, j: (0, tile_c(r)))],
        out_shape=[jax.ShapeDtypeStruct((N, P), BF16),
                   jax.ShapeDtypeStruct((B, TR, S), BF16),
                   jax.ShapeDtypeStruct((NG, N), F32)],
        scratch_shapes=[pltpu.VMEM((tm, D), F32),
                        pltpu.VMEM((2, tm, D), BF16),
                        pltpu.VMEM((2, tm, hd), F32),
                        pltpu.VMEM((2, tm, hd), F32),
                        pltpu.VMEM((2, PREP_SLABS, hd, tm // PREP_SLABS), F32),
                        pltpu.VMEM((2, PREP_SLABS, hd, tm // PREP_SLABS), F32),
                        pltpu.SemaphoreType.DMA(())],
        compiler_params=_params(("arbitrary", "arbitrary")),
        name="inproj",
    )(x2, mod3, g1, pos2, inv2, sgn, w_all, w_aq, w_t)


_R, _MT, _A, _LIM, _E, _SC = range(6)
_NVEC = 6


def _log_sigmoid(x):
    return jnp.minimum(x, 0.0) - jnp.log1p(jnp.exp(-jnp.abs(x)))


def _lane_scan(x, reverse, op, ident):
    L = x.shape[1]
    lane = lax.broadcasted_iota(jnp.int32, x.shape, 1)
    sh = 1
    while sh < L:
        if reverse:
            x = op(x, jnp.where(lane < L - sh, pltpu.roll(x, L - sh, 1), ident))
        else:
            x = op(x, jnp.where(lane >= sh, pltpu.roll(x, sh, 1), ident))
        sh *= 2
    return x


def _chunk_scan(x, reverse, op, ident):
    n = x.shape[1] // LANES
    parts = [_lane_scan(x[:, i * LANES:(i + 1) * LANES], reverse, op, ident) for i in range(n)]
    order = range(n - 2, -1, -1) if reverse else range(1, n)
    for i in order:
        prev = parts[i + 1][:, 0:1] if reverse else parts[i - 1][:, LANES - 1:LANES]
        parts[i] = op(parts[i], prev)
    return jnp.concatenate(parts, axis=1)


def _gate_vectors(li, lf_pre, reverse, nc):
    rows, Lc = li.shape
    hb = rows // nc
    lf = _log_sigmoid(lf_pre)
    b = _chunk_scan(lf, reverse, jnp.add, 0.0)
    r = li - b
    rcm = _chunk_scan(r, reverse, jnp.maximum, NEG_INF)
    end = slice(0, 1) if reverse else slice(Lc - 1, Lc)
    bl = jnp.broadcast_to(b[:, end], (rows, Lc))
    rmax = jnp.broadcast_to(rcm[:, end], (rows, Lc))
    m = jnp.zeros((hb, Lc), F32)
    m_parts = [None] * nc
    for j in (range(nc - 1, -1, -1) if reverse else range(nc)):
        m_parts[j] = m
        sl = slice(j * hb, (j + 1) * hb)
        m = bl[sl, :] + jnp.maximum(m, rmax[sl, :])
    m_all = jnp.concatenate(m_parts, axis=0)
    mt = jnp.maximum(m_all, rcm)
    m_end = jnp.maximum(m_all, rmax)
    return r, mt, jnp.exp(m_all - mt), jnp.exp(-(b + mt)), jnp.exp(r - m_end), jnp.exp(m_all - m_end)


def _gatevec_kernel(g_ref, gb_ref, o_ref, *, nc):
    for dr in range(2):
        li = g_ref[2 * dr] + gb_ref[2 * dr]
        lf_pre = g_ref[2 * dr + 1] + gb_ref[2 * dr + 1]
        for idx, val in enumerate(_gate_vectors(li, lf_pre, dr == 1, nc)):
            o_ref[dr * _NVEC + idx] = val


def _gatevec(gates, gbias, nc):
    _, rows, Lc = gates.shape
    return pl.pallas_call(
        functools.partial(_gatevec_kernel, nc=nc),
        out_shape=jax.ShapeDtypeStruct((2 * _NVEC, rows, Lc), F32),
        compiler_params=pltpu.CompilerParams(vmem_limit_bytes=VMEM_LIMIT),
        name="gatevec",
    )(gates, gbias)


CONV_ROWS = 128


def _mlstm_kernel(q_ref, k_ref, vt_ref, mo_ref, vec_ref, wq_ref, wk_ref, bq_ref, bk_ref, go_ref,
                  o_ref, pad_scr, qs_scr, big_scr, vt_scr, ns_scr, ct_scr, n_scr, p0_scr, p1_scr, *, S, d, Lc, CW):
    nc = S // Lc
    half_w = CW // 2
    PADR = 8
    CR = CONV_ROWS
    N0 = Lc
    C0 = Lc + BF16_ROWS

    zero_rows = jnp.zeros((PADR, d), F32)

    def conv_silu(src_ref, w_ref, b_ref, store, scale):
        pad_scr[0:PADR, :] = zero_rows
        pad_scr[PADR + S:2 * PADR + S, :] = zero_rows

        def fill(j, c):
            r0 = pl.multiple_of(j * CR, CR)
            pad_scr[pl.ds(PADR + r0, CR), :] = src_ref[0, pl.ds(r0, CR), :].astype(F32)
            return c
        lax.fori_loop(0, S // CR, fill, 0)
        w = w_ref[...]
        bias = b_ref[...]

        def body(j, c):
            r0 = pl.multiple_of(j * CR, CR)
            win = pad_scr[pl.ds(r0, CR + 2 * PADR), :]
            y = bias
            for i in range(CW):
                o = PADR - half_w + i
                y = y + win[o:o + CR, :] * w[i:i + 1, :]
            y = y * _sigmoid(y)
            if scale != 1.0:
                y = y * scale
            store(j, r0, y.astype(BF16))
            return c
        lax.fori_loop(0, S // CR, body, 0)

    def store_q(j, r0, y):
        qs_scr[pl.ds(r0, CR), :] = y

    per = Lc // CR

    def store_k(j, r0, y):
        big_scr[j // per, pl.ds(pl.multiple_of((j % per) * CR, CR), CR), :] = y

    conv_silu(q_ref, wq_ref, bq_ref, store_q, 1.0)
    conv_silu(k_ref, wk_ref, bk_ref, store_k, d ** -0.5)

    for j in range(nc):
        vt_scr[j] = vt_ref[0, :, j * Lc:(j + 1) * Lc]

    ns_scr[...] = jnp.zeros(ns_scr.shape, F32)
    ct_scr[...] = jnp.zeros(ct_scr.shape, F32)
    n_scr[...] = jnp.zeros(n_scr.shape, F32)

    def state_step(dr, jc):
        base = dr * _NVEC
        e_row = vec_ref[0, 0, base + _E, pl.ds(jc, 1), :]
        sc = vec_ref[0, 0, base + _SC, pl.ds(jc, 1), :]
        lhs = jnp.concatenate([vt_scr[jc].astype(F32) * e_row, jnp.broadcast_to(e_row, (BF16_ROWS, Lc))], axis=0)
        upd = jnp.dot(lhs.astype(BF16), big_scr[jc, 0:Lc, :], preferred_element_type=F32)
        ct = ct_scr[dr]
        n = n_scr[dr]
        big_scr[jc, C0 + dr * d:C0 + (dr + 1) * d, :] = ct.astype(BF16)
        ns_scr[jc, dr:dr + 1, :] = n[0:1, :]
        ct_scr[dr] = sc * ct + upd[0:d, :]
        n_scr[dr] = sc * n + upd[d:d + 8, :]

    for j in range(nc):
        state_step(0, j)
        state_step(1, nc - 1 - j)
    for j in range(nc):
        big_scr[j, N0:N0 + BF16_ROWS, :] = ns_scr[j].astype(BF16)

    rr = lax.broadcasted_iota(jnp.int32, (Lc, Lc), 0)
    cc = lax.broadcasted_iota(jnp.int32, (Lc, Lc), 1)
    r1 = lax.broadcasted_iota(jnp.int32, (LANES, LANES), 0)
    c1 = lax.broadcasted_iota(jnp.int32, (LANES, LANES), 1)
    eye = r1 == c1
    gout = go_ref[...]

    def to_col(row):
        return jnp.concatenate(
            [jnp.sum(jnp.where(eye, row[:, i * LANES:(i + 1) * LANES], 0.0), axis=1, keepdims=True)
             for i in range(Lc // LANES)], axis=0)

    hrows = C0 + d

    def head(j, p_scr):
        qj = qs_scr[pl.ds(pl.multiple_of(j * Lc, Lc), Lc), :]
        p_scr[0:hrows, :] = lax.dot_general(big_scr[j, 0:hrows, :], qj, _NT, preferred_element_type=F32)
        p_scr[hrows:hrows + d, :] = lax.dot_general(big_scr[j, hrows:hrows + d, :], qj, _NT,
                                                    preferred_element_type=F32)

    def tail(j, p_scr):
        r0 = pl.multiple_of(j * Lc, Lc)
        st = p_scr[0:Lc, :]
        qkw = []
        inv_den = []
        a_rows = []
        for dr in range(2):
            base = dr * _NVEC
            r_row = vec_ref[0, 0, base + _R, pl.ds(j, 1), :]
            mt_row = vec_ref[0, 0, base + _MT, pl.ds(j, 1), :]
            a_row = vec_ref[0, 0, base + _A, pl.ds(j, 1), :]
            lim_row = vec_ref[0, 0, base + _LIM, pl.ds(j, 1), :]
            valid = (rr <= cc) if dr == 0 else (rr >= cc)
            w = jnp.exp(jnp.where(valid, to_col(r_row) - mt_row, NEG_INF))
            qkw_d = st * w
            den = a_row * p_scr[N0 + dr:N0 + dr + 1, :] + jnp.sum(qkw_d, axis=0, keepdims=True)
            inv_den.append(1.0 / jnp.maximum(jnp.abs(den), lim_row))
            a_rows.append(a_row)
            qkw.append(qkw_d.astype(BF16))
        intra = jnp.dot(vt_scr[j], jnp.concatenate(qkw, axis=1), preferred_element_type=F32)
        hs = None
        for dr in range(2):
            inter = p_scr[C0 + dr * d:C0 + (dr + 1) * d, :]
            hd = (a_rows[dr] * inter + intra[:, dr * Lc:(dr + 1) * Lc]) * inv_den[dr]
            hs = hd if hs is None else hs + hd
        mu = jnp.mean(hs, axis=0, keepdims=True)
        hc = hs - mu
        hn = (hc * lax.rsqrt(jnp.mean(hc * hc, axis=0, keepdims=True) + EPS)).T
        mo = mo_ref[0, pl.ds(r0, Lc), :].astype(F32)
        o_ref[0, pl.ds(r0, Lc), :] = (_sigmoid(mo) * hn * gout).astype(BF16)

    bufs = (p0_scr, p1_scr)
    head(0, bufs[0])
    for j in range(nc):
        if j + 1 < nc:
            head(j + 1, bufs[(j + 1) % 2])
        tail(j, bufs[j % 2])


def _mlstm(proj3, vt3, vec5, w_conv, b_conv, g_out):
    B, S, _ = proj3.shape
    H = MLSTM_HEADS
    d = g_out.shape[1] // H
    Lc = MLSTM_CHUNK
    nc = S // Lc
    CW = w_conv.shape[0]
    assert Lc % LANES == 0 and Lc % CONV_ROWS == 0 and d == Lc and CW // 2 <= 8 and nc % 2 == 0 and nc >= 4
    kern = functools.partial(_mlstm_kernel, S=S, d=d, Lc=Lc, CW=CW)
    col = lambda off: (lambda b, h: (b, 0, off + h))
    return pl.pallas_call(
        kern,
        grid=(B, H),
        in_specs=[pl.BlockSpec((1, S, d), col(0)),
                  pl.BlockSpec((1, S, d), col(H)),
                  pl.BlockSpec((1, d, S), lambda b, h: (b, h, 0)),
                  pl.BlockSpec((1, S, d), col(2 * H)),
                  pl.BlockSpec((1, 1, 2 * _NVEC, nc, Lc), lambda b, h: (b, h, 0, 0, 0)),
                  pl.BlockSpec((CW, d), lambda b, h: (0, h)),
                  pl.BlockSpec((CW, d), lambda b, h: (0, H + h)),
                  pl.BlockSpec((1, d), lambda b, h: (0, h)),
                  pl.BlockSpec((1, d), lambda b, h: (0, H + h)),
                  pl.BlockSpec((1, d), lambda b, h: (0, h))],
        out_specs=pl.BlockSpec((1, S, d), lambda b, h: (b, 0, h)),
        out_shape=jax.ShapeDtypeStruct((B, S, H * d), BF16),
        scratch_shapes=[pltpu.VMEM((S + 16, d), F32),
                        pltpu.VMEM((S, d), BF16),
                        pltpu.VMEM((nc, Lc + BF16_ROWS + 2 * d, d), BF16),
                        pltpu.VMEM((nc, d, Lc), BF16),
                        pltpu.VMEM((nc, BF16_ROWS, d), F32),
                        pltpu.VMEM((2, d, d), F32),
                        pltpu.VMEM((2, 8, d), F32),
                        pltpu.VMEM((Lc + BF16_ROWS + 2 * d, Lc), F32),
                        pltpu.VMEM((Lc + BF16_ROWS + 2 * d, Lc), F32)],
        compiler_params=_params(("arbitrary", "arbitrary")),
        name="mlstm",
    )(proj3, proj3, vt3, proj3, vec5, w_conv, w_conv, b_conv, b_conv, g_out)


LOG2E = 1.4426950408889634
ATTN_QBLOCKS = 4


def _attn_kernel(sink_ref, q_ref, ktp_ref, ktc_ref, ktn_ref, vtp_ref, vtc_ref, vtn_ref, g_ref, o_ref,
                 s0_scr, s1_scr, *, H, G, QB):
    n4 = pl.program_id(1)
    last4 = pl.num_programs(1) - 1
    blk = ATTN_BLOCK
    hd = ATTN_HEAD_DIM
    R = H // G
    c1 = (hd ** -0.5) * LOG2E
    rr = lax.broadcasted_iota(jnp.int32, (blk, blk), 0)
    cc = lax.broadcasted_iota(jnp.int32, (blk, blk), 1)
    ones = jnp.ones((hd, 3 * blk), BF16)

    def kv_window(i, g, prev_ref, cur_ref, next_ref):
        rs = slice(g * hd, (g + 1) * hd)
        lo = prev_ref[0, rs, :] if i == 0 else cur_ref[0, rs, (i - 1) * blk:i * blk]
        hi = next_ref[0, rs, :] if i == QB - 1 else cur_ref[0, rs, (i + 1) * blk:(i + 2) * blk]
        return jnp.concatenate([lo, cur_ref[0, rs, i * blk:(i + 1) * blk], hi], axis=1)

    def head(i, s_scr):
        q = q_ref[0, i * blk:(i + 1) * blk, :]
        for g in range(G):
            qg = jnp.concatenate([q[:, (g * R + r) * hd:(g * R + r + 1) * hd] for r in range(R)], axis=0)
            s_scr[g] = jnp.dot(qg, kv_window(i, g, ktp_ref, ktc_ref, ktn_ref), preferred_element_type=F32)

    def tail(i, s_scr):
        m_prev = (cc >= rr) if i > 0 else jnp.logical_and(cc >= rr, n4 > 0)
        m_next = (cc <= rr) if i < QB - 1 else jnp.logical_and(cc <= rr, n4 < last4)
        outs = []
        for g in range(G):
            vt = kv_window(i, g, vtp_ref, vtc_ref, vtn_ref)
            vt_aug = jnp.concatenate([vt, ones], axis=0)
            ps = []
            extras = []
            for r in range(R):
                sr = s_scr[g, r * blk:(r + 1) * blk, :] * c1
                t0 = jnp.where(m_prev, sr[:, 0:blk], NEG_INF)
                t1 = sr[:, blk:2 * blk]
                t2 = jnp.where(m_next, sr[:, 2 * blk:3 * blk], NEG_INF)
                sink2 = sink_ref[g * R + r] * LOG2E
                m = jnp.maximum(jnp.max(jnp.maximum(jnp.maximum(t0, t1), t2), axis=1, keepdims=True), sink2)
                ps.append(jnp.concatenate([jnp.exp2(t0 - m), jnp.exp2(t1 - m), jnp.exp2(t2 - m)],
                                          axis=1).astype(BF16))
                extras.append(jnp.exp2(sink2 - m))
            oa = lax.dot_general(jnp.concatenate(ps, axis=0), vt_aug, _NT, preferred_element_type=F32)
            for r in range(R):
                o_r = oa[r * blk:(r + 1) * blk, :]
                outs.append(o_r[:, 0:hd] / (o_r[:, hd:2 * hd] + extras[r]))
        oall = jnp.concatenate(outs, axis=1)
        y = oall * lax.rsqrt(jnp.mean(oall * oall, axis=-1, keepdims=True) + EPS) * g_ref[...]
        o_ref[0, i * blk:(i + 1) * blk, :] = y.astype(BF16)

    bufs = (s0_scr, s1_scr)
    head(0, bufs[0])
    for i in range(QB):
        if i + 1 < QB:
            head(i + 1, bufs[(i + 1) % 2])
        tail(i, bufs[i % 2])


def _attn(proj3, t3, sink, g_attn, q_col0, vt_row0, kt_row0):
    B, S, _ = proj3.shape
    blk = ATTN_BLOCK
    QB = ATTN_QBLOCKS
    nb = S // blk
    H = sink.shape[0]
    G = ATTN_KV_HEADS
    AW = H * ATTN_HEAD_DIM
    KW = G * ATTN_HEAD_DIM
    assert WINDOW == blk and q_col0 % AW == 0 and vt_row0 % KW == 0 and kt_row0 % KW == 0 and nb % QB == 0
    qb, vb, kb = q_col0 // AW, vt_row0 // KW, kt_row0 // KW
    kern = functools.partial(_attn_kernel, H=H, G=G, QB=QB)
    prev = lambda rb: (lambda b, n: (b, rb, jnp.maximum(n * QB - 1, 0)))
    cur = lambda rb: (lambda b, n: (b, rb, n))
    nxt = lambda rb: (lambda b, n: (b, rb, jnp.minimum(n * QB + QB, nb - 1)))
    return pl.pallas_call(
        kern,
        grid=(B, nb // QB),
        in_specs=[pl.BlockSpec(memory_space=pltpu.SMEM),
                  pl.BlockSpec((1, QB * blk, AW), lambda b, n: (b, n, qb)),
                  pl.BlockSpec((1, KW, blk), prev(kb)),
                  pl.BlockSpec((1, KW, QB * blk), cur(kb)),
                  pl.BlockSpec((1, KW, blk), nxt(kb)),
                  pl.BlockSpec((1, KW, blk), prev(vb)),
                  pl.BlockSpec((1, KW, QB * blk), cur(vb)),
                  pl.BlockSpec((1, KW, blk), nxt(vb)),
                  pl.BlockSpec((1, AW), lambda b, n: (0, 0))],
        out_specs=pl.BlockSpec((1, QB * blk, AW), lambda b, n: (b, n, 0)),
        out_shape=jax.ShapeDtypeStruct((B, S, AW), BF16),
        scratch_shapes=[pltpu.VMEM((G, (H // G) * blk, 3 * blk), F32),
                        pltpu.VMEM((G, (H // G) * blk, 3 * blk), F32)],
        compiler_params=_params(("arbitrary", "arbitrary")),
        name="attn",
    )(sink, proj3, t3, t3, t3, t3, t3, t3, g_attn)


def _outproj_kernel(m_ref, a_ref, x_ref, mod_ref, g2_ref, wt_ref, wb_ref, o_ref, h_ref, *, D):
    acc = jnp.dot(m_ref[...], wt_ref[...], preferred_element_type=F32)
    acc = acc + jnp.dot(a_ref[...], wb_ref[...], preferred_element_type=F32)
    gt = mod_ref[0, :, 2 * D:3 * D]
    x1 = x_ref[...] + gt * acc
    o_ref[...] = x1
    sh = mod_ref[0, :, 3 * D:4 * D]
    gsc = g2_ref[...] * (1.0 + mod_ref[0, :, 4 * D:5 * D])
    h_ref[...] = (x1 * lax.rsqrt(jnp.mean(x1 * x1, axis=-1, keepdims=True) + EPS) * gsc + sh).astype(BF16)


def _outproj(m2, a2, x2, mod3, g2, w_o, S, tm=512):
    N, D = x2.shape
    K = m2.shape[1]
    per_b = S // tm
    return pl.pallas_call(
        functools.partial(_outproj_kernel, D=D),
        grid=(N // tm,),
        in_specs=[pl.BlockSpec((tm, K), lambda i: (i, 0)),
                  pl.BlockSpec((tm, K), lambda i: (i, 0)),
                  pl.BlockSpec((tm, D), lambda i: (i, 0)),
                  pl.BlockSpec((1, 1, mod3.shape[2]), lambda i: (i // per_b, 0, 0)),
                  pl.BlockSpec((1, D), lambda i: (0, 0)),
                  pl.BlockSpec((K, D), lambda i: (0, 0)),
                  pl.BlockSpec((K, D), lambda i: (1, 0))],
        out_specs=[pl.BlockSpec((tm, D), lambda i: (i, 0)),
                   pl.BlockSpec((tm, D), lambda i: (i, 0))],
        out_shape=[jax.ShapeDtypeStruct((N, D), F32),
                   jax.ShapeDtypeStruct((N, D), BF16)],
        compiler_params=_params(("arbitrary",)),
        name="outproj",
    )(m2, a2, x2, mod3, g2, w_o, w_o)


def _ffn_up_kernel(h_ref, wg_ref, wu_ref, o_ref):
    hb = h_ref[...]
    gate = jnp.dot(hb, wg_ref[...].astype(BF16), preferred_element_type=F32)
    up = jnp.dot(hb, wu_ref[...].astype(BF16), preferred_element_type=F32)
    o_ref[...] = (gate * _sigmoid(gate) * up).astype(BF16)


def _ffn_up(h2, w_gate, w_up, tm=2048, tf=512):
    N, D = h2.shape
    FF = w_gate.shape[1]
    return pl.pallas_call(
        _ffn_up_kernel,
        grid=(N // tm, FF // tf),
        in_specs=[pl.BlockSpec((tm, D), lambda i, j: (i, 0)),
                  pl.BlockSpec((D, tf), lambda i, j: (0, j)),
                  pl.BlockSpec((D, tf), lambda i, j: (0, j))],
        out_specs=pl.BlockSpec((tm, tf), lambda i, j: (i, j)),
        out_shape=jax.ShapeDtypeStruct((N, FF), BF16),
        compiler_params=_params(("arbitrary", "arbitrary")),
        name="ffn_up",
    )(h2, w_gate, w_up)


def _ffn_down_kernel(u_ref, x_hbm, mod_ref, gf_ref, wd_ref, o_ref, xbuf, sem, *, D, tm):
    i = pl.program_id(0)
    j = pl.program_id(1)

    def x_copy():
        return pltpu.make_async_copy(x_hbm.at[pl.ds(pl.multiple_of(i * tm, tm), tm), :], xbuf, sem)

    last = pl.num_programs(1) - 1

    @pl.when(j == 0)
    def _():
        x_copy().start()
        o_ref[...] = jnp.dot(u_ref[...], wd_ref[...], preferred_element_type=F32)

    @pl.when(jnp.logical_and(j > 0, j < last))
    def _():
        o_ref[...] += jnp.dot(u_ref[...], wd_ref[...], preferred_element_type=F32)

    @pl.when(j == last)
    def _():
        x_copy().wait()
        gt = mod_ref[0, :, 5 * D:6 * D]
        for rows in (slice(0, tm // 2), slice(tm // 2, tm)):
            acc = o_ref[rows, :] + jnp.dot(u_ref[rows, :], wd_ref[...], preferred_element_type=F32)
            y = xbuf[rows, :] + gt * acc
            o_ref[rows, :] = y * lax.rsqrt(jnp.mean(y * y, axis=-1, keepdims=True) + EPS) * gf_ref[...]


def _ffn_down(u2, x2, mod3, gf, w_down, S, tm=1024, tk=1408):
    N, D = x2.shape
    FF = u2.shape[1]
    per_b = S // tm
    assert FF // tk >= 2, "the first and the last K step are distinct branches"
    return pl.pallas_call(
        functools.partial(_ffn_down_kernel, D=D, tm=tm),
        grid=(N // tm, FF // tk),
        in_specs=[pl.BlockSpec((tm, tk), lambda i, j: (i, j)),
                  pl.BlockSpec(memory_space=pl.ANY),
                  pl.BlockSpec((1, 1, mod3.shape[2]), lambda i, j: (i // per_b, 0, 0)),
                  pl.BlockSpec((1, D), lambda i, j: (0, 0)),
                  pl.BlockSpec((tk, D), lambda i, j: (j, 0))],
        out_specs=pl.BlockSpec((tm, D), lambda i, j: (i, 0)),
        out_shape=jax.ShapeDtypeStruct((N, D), F32),
        scratch_shapes=[pltpu.VMEM((tm, D), F32), pltpu.SemaphoreType.DMA(())],
        compiler_params=_params(("arbitrary", "arbitrary")),
        name="ffn_down",
    )(u2, x2, mod3, gf, w_down)


def kernel(x, c, positions, w_ada, b_ada, g_norm1, g_norm2, w_in, b_gates, w_conv, b_conv, g_mlstm_out,
           sink, g_attn_out, w_out, w_gate, w_up, w_down, g_final):
    B, S, D = x.shape
    depth = w_ada.shape[0]
    H = MLSTM_HEADS
    MW = D // 2
    KW = ATTN_KV_HEADS * ATTN_HEAD_DIM
    NG = 4 * H
    L = MLSTM_CHUNK
    nc = S // L
    g0 = 4 * MW
    x2 = x.reshape(B * S, D)
    pos2 = positions.reshape(B * S, 1)
    for l in range(depth):
        mod3 = _ada(c, w_ada[l], b_ada[l]).reshape(B, 1, 6 * D)
        wl = w_in[l].astype(BF16)
        a0 = g0 + NG
        t_rows = MW + 2 * KW + NG
        t_pad = -t_rows % (2 * BF16_ROWS)
        w_t = jnp.concatenate([wl[:, 2 * MW:3 * MW], wl[:, a0 + MW + KW:], wl[:, a0 + MW:a0 + MW + KW],
                               wl[:, g0:a0], jnp.zeros((D, t_pad), BF16)], axis=1).T
        proj, t3, gates_t = _inproj(x2, mod3, g_norm1[l].reshape(1, D), pos2, wl, wl[:, a0:a0 + MW], w_t, B, S, NG)
        proj3 = proj.reshape(B, S, proj.shape[1])
        gates = gates_t.reshape(4, H, B, nc, L).transpose(0, 3, 1, 2, 4).reshape(4, nc * H * B, L)
        gbias = jnp.broadcast_to(b_gates[l].reshape(4, 1, H, 1), (4, nc, H, B)).reshape(4, nc * H * B, 1)
        vec = _gatevec(gates, gbias, nc)
        vec5 = vec.reshape(2 * _NVEC, nc, H, B, L).transpose(3, 2, 0, 1, 4)
        m_out = _mlstm(proj3, t3, vec5, w_conv[l], b_conv[l].reshape(1, 2 * MW), g_mlstm_out[l].reshape(1, MW))
        a_out = _attn(proj3, t3, sink[l], g_attn_out[l].reshape(1, MW), q_col0=3 * MW,
                      vt_row0=MW, kt_row0=MW + KW)
        x2, h2 = _outproj(m_out.reshape(B * S, MW), a_out.reshape(B * S, MW), x2, mod3,
                          g_norm2[l].reshape(1, D), w_out[l].astype(BF16), S)
        assert l == depth - 1, "final norm is fused into the last layer's FFN"
        u2 = _ffn_up(h2, w_gate[l], w_up[l])
        x2 = _ffn_down(u2, x2, mod3, g_final.reshape(1, D), w_down[l].astype(BF16), S)
    return x2.reshape(B, S, D)
```

```python
import functools

import jax
import jax.numpy as jnp
from jax import lax
from jax.experimental import pallas as pl
from jax.experimental.pallas import tpu as pltpu

F32 = jnp.float32
BF16 = jnp.bfloat16

EPS = 1e-6
NEG_INF = -1e30
ROPE_THETA = 10000.0

MLSTM_HEADS = 4
MLSTM_CHUNK = 256
ATTN_HEAD_DIM = 128
ATTN_KV_HEADS = 2
WINDOW = 128
ATTN_BLOCK = 128

LANES = 128
BF16_ROWS = 16
VMEM_LIMIT = 56 * 1024 * 1024

_NT = (((1,), (1,)), ((), ()))


def _sigmoid(x):
    return 1.0 / (1.0 + jnp.exp(-x))


def _params(sem, vmem=VMEM_LIMIT):
    return pltpu.CompilerParams(dimension_semantics=sem, vmem_limit_bytes=vmem)


def _split_bf16(v):
    hi = v.astype(BF16)
    return hi, (v - hi.astype(F32)).astype(BF16)


def _ada_kernel(c_ref, w_ref, b_ref, o_ref):
    c = c_ref[...]
    s_hi, s_lo = _split_bf16(c * _sigmoid(c))
    w_hi, w_lo = _split_bf16(w_ref[...])
    dot = functools.partial(jnp.dot, preferred_element_type=F32)
    o_ref[...] = dot(s_hi, w_hi) + (dot(s_lo, w_hi) + dot(s_hi, w_lo)) + b_ref[...]


def _ada(c, w_ada, b_ada, tn=1024):
    B, D = c.shape
    N = w_ada.shape[1]
    return pl.pallas_call(
        _ada_kernel,
        grid=(N // tn,),
        in_specs=[pl.BlockSpec((B, D), lambda j: (0, 0)),
                  pl.BlockSpec((D, tn), lambda j: (0, j)),
                  pl.BlockSpec((1, tn), lambda j: (0, j))],
        out_specs=pl.BlockSpec((B, tn), lambda j: (0, j)),
        out_shape=jax.ShapeDtypeStruct((B, N), F32),
        compiler_params=_params(("arbitrary",)),
        name="ada",
    )(c, w_ada, b_ada.reshape(1, N))


def _rope(t, cos2, sin2):
    return t * cos2 + pltpu.roll(t, ATTN_HEAD_DIM // 2, 1) * sin2


PREP_SLABS = 4
PREP_STEP0 = 2


def _inproj_kernel(x_hbm, mod_ref, g_ref, pos_ref, inv_ref, sgn_ref, w_ref, wq_ref, wt_ref,
                   o_ref, t_ref, gt_ref, xbuf, h_scr, cos_scr, sin_scr, cost_scr, sint_scr, sem,
                   *, D, tm, n_tiles, n_main, rope_j0, n_t, ak_rows, gate_rows):
    r = pl.program_id(0)
    j = pl.program_id(1)
    hd = ATTN_HEAD_DIM
    rows = tm // PREP_SLABS
    ps = r % 2
    cs = 1 - ps
    tile_p = jnp.minimum(r, n_tiles - 1)

    def x_copy():
        return pltpu.make_async_copy(x_hbm.at[pl.ds(pl.multiple_of(tile_p * tm, tm), tm), :], xbuf, sem)

    @pl.when(j == 0)
    def _():
        x_copy().start()

    @pl.when(j == PREP_STEP0 - 1)
    def _():
        x_copy().wait()

    def prep_slab():
        k = j - PREP_STEP0
        r0 = pl.multiple_of(k * rows, rows)
        x = xbuf[pl.ds(r0, rows), :]
        sh = mod_ref[0, :, 0:D]
        gsc = g_ref[...] * (1.0 + mod_ref[0, :, D:2 * D])
        h_scr[ps, pl.ds(r0, rows), :] = (
            x * lax.rsqrt(jnp.mean(x * x, axis=-1, keepdims=True) + EPS) * gsc + sh).astype(BF16)
        ang = pos_ref[pl.ds(r0, rows), :].astype(F32) * inv_ref[...]
        cos2 = jnp.cos(ang)
        sin2 = jnp.sin(ang) * sgn_ref[...]
        cos_scr[ps, pl.ds(r0, rows), :] = cos2
        sin_scr[ps, pl.ds(r0, rows), :] = sin2
        cost_scr[ps, k] = cos2.T
        sint_scr[ps, k] = sin2.T

    def plain():
        o_ref[...] = jnp.dot(h_scr[cs], w_ref[...], preferred_element_type=F32).astype(BF16)

    def roped():
        acc = jnp.dot(h_scr[cs], wq_ref[...], preferred_element_type=F32)
        cos2 = cos_scr[cs]
        sin2 = sin_scr[cs]
        parts = [_rope(acc[:, hh * hd:(hh + 1) * hd], cos2, sin2) for hh in range(acc.shape[1] // hd)]
        o_ref[...] = jnp.concatenate(parts, axis=1).astype(BF16)

    tt = t_ref.shape[1]

    def transposed(ti):
        acc = lax.dot_general(wt_ref[...], h_scr[cs], _NT, preferred_element_type=F32)
        lo, hi = ti * tt, (ti + 1) * tt
        pieces = []
        cur = lo
        for a0 in range(ak_rows[0], ak_rows[1], hd):
            if a0 < lo or a0 + hd > hi:
                assert a0 + hd <= lo or a0 >= hi, "a k head straddles two transposed tiles"
                continue
            if a0 > cur:
                pieces.append(acc[cur - lo:a0 - lo, :])
            t = acc[a0 - lo:a0 - lo + hd, :]
            rot = jnp.concatenate([t[hd // 2:, :], t[:hd // 2, :]], axis=0)
            cost = jnp.concatenate([cost_scr[cs, k] for k in range(PREP_SLABS)], axis=1)
            sint = jnp.concatenate([sint_scr[cs, k] for k in range(PREP_SLABS)], axis=1)
            pieces.append(t * cost + rot * sint)
            cur = a0 + hd
        if cur < hi:
            pieces.append(acc[cur - lo:, :])
        t_ref[0] = jnp.concatenate(pieces, axis=0).astype(BF16) if len(pieces) > 1 else acc.astype(BF16)
        if lo <= gate_rows[0] and gate_rows[1] <= hi:
            gt_ref[...] = acc[gate_rows[0] - lo:gate_rows[1] - lo, :]

    mult = r > 0
    both = jnp.logical_and

    @pl.when(both(mult, j < PREP_STEP0))
    def _():
        plain()

    @pl.when(both(mult, both(j >= PREP_STEP0, j < rope_j0)))
    def _():
        plain()
        prep_slab()

    @pl.when(both(mult, both(j >= rope_j0, j < n_main)))
    def _():
        roped()
        prep_slab()

    for ti in range(n_t):
        @pl.when(both(mult, j == n_main + ti))
        def _(ti=ti):
            transposed(ti)
            prep_slab()

    @pl.when(both(jnp.logical_not(mult), j >= PREP_STEP0))
    def _():
        prep_slab()


def _inproj(x2, mod3, g1, pos2, w_all, w_aq, w_t, B, S, NG, tm=1024, tn=1024, n_t=2):
    N, D = x2.shape
    P = 4 * (D // 2)
    TR = w_t.shape[0]
    hd = ATTN_HEAD_DIM
    half = hd // 2
    inv = ROPE_THETA ** (-jnp.arange(half, dtype=F32) * 2.0 / hd)
    inv2 = jnp.concatenate([inv, inv]).reshape(1, hd)
    sgn = jnp.concatenate([-jnp.ones((half,), F32), jnp.ones((half,), F32)]).reshape(1, hd)
    MW = D // 2
    KW = ATTN_KV_HEADS * hd
    aq0 = 3 * MW
    ak_rows = (MW + KW, MW + 2 * KW)
    gate_rows = (MW + 2 * KW, MW + 2 * KW + NG)
    tt = TR // n_t
    n_tiles = N // tm
    n_main = P // tn
    n_j = n_main + n_t
    rope_j0 = aq0 // tn
    assert aq0 % tn == 0 and MW == tn and w_aq.shape == (D, MW) and TR % n_t == 0 and tt % BF16_ROWS == 0
    assert gate_rows[1] <= TR and tm % PREP_SLABS == 0 and n_j == PREP_STEP0 + PREP_SLABS
    assert PREP_STEP0 <= rope_j0
    kern = functools.partial(_inproj_kernel, D=D, tm=tm, n_tiles=n_tiles, n_main=n_main, rope_j0=rope_j0,
                             n_t=n_t, ak_rows=ak_rows, gate_rows=gate_rows)
    per_b = S // tm
    tile_p = lambda r: jnp.minimum(r, n_tiles - 1)
    tile_c = lambda r: jnp.maximum(r - 1, 0)
    live = lambda r: jnp.minimum(r, 1)
    main_j = lambda r, j: jnp.minimum(j, n_main - 1) * live(r)
    t_j = lambda r, j: jnp.maximum(j - n_main, 0) * live(r)
    return pl.pallas_call(
        kern,
        grid=(n_tiles + 1, n_j),
        in_specs=[pl.BlockSpec(memory_space=pl.ANY),
                  pl.BlockSpec((1, 1, mod3.shape[2]), lambda r, j: (tile_p(r) // per_b, 0, 0)),
                  pl.BlockSpec((1, D), lambda r, j: (0, 0)),
                  pl.BlockSpec((tm, 1), lambda r, j: (tile_p(r), 0)),
                  pl.BlockSpec((1, hd), lambda r, j: (0, 0)),
                  pl.BlockSpec((1, hd), lambda r, j: (0, 0)),
                  pl.BlockSpec((D, tn), lambda r, j: (0, jnp.where(j < 2 * MW // tn, j, 3 * MW // tn))),
                  pl.BlockSpec((D, tn), lambda r, j: (0, 0)),
                  pl.BlockSpec((tt, D), lambda r, j: (jnp.maximum(j - n_main, 0), 0))],
        out_specs=[pl.BlockSpec((tm, tn), lambda r, j: (tile_c(r), main_j(r, j))),
                   pl.BlockSpec((1, tt, tm), lambda r, j: (tile_c(r) // per_b, t_j(r, j), tile_c(r) % per_b)),
                   pl.BlockSpec((NG, tm), lambda r, j: (0, tile_c(r)))],
        out_shape=[jax.ShapeDtypeStruct((N, P), BF16),
                   jax.ShapeDtypeStruct((B, TR, S), BF16),
                   jax.ShapeDtypeStruct((NG, N), F32)],
        scratch_shapes=[pltpu.VMEM((tm, D), F32),
                        pltpu.VMEM((2, tm, D), BF16),
                        pltpu.VMEM((2, tm, hd), F32),
                        pltpu.VMEM((2, tm, hd), F32),
                        pltpu.VMEM((2, PREP_SLABS, hd, tm // PREP_SLABS), F32),
                        pltpu.VMEM((2, PREP_SLABS, hd, tm // PREP_SLABS), F32),
                        pltpu.SemaphoreType.DMA(())],
        compiler_params=_params(("arbitrary", "arbitrary")),
        name="inproj",
    )(x2, mod3, g1, pos2, inv2, sgn, w_all, w_aq, w_t)


_R, _MT, _A, _LIM, _E, _SC = range(6)
_NVEC = 6


def _log_sigmoid(x):
    return jnp.minimum(x, 0.0) - jnp.log1p(jnp.exp(-jnp.abs(x)))


def _lane_scan(x, reverse, op, ident):
    L = x.shape[1]
    lane = lax.broadcasted_iota(jnp.int32, x.shape, 1)
    sh = 1
    while sh < L:
        if reverse:
            x = op(x, jnp.where(lane < L - sh, pltpu.roll(x, L - sh, 1), ident))
        else:
            x = op(x, jnp.where(lane >= sh, pltpu.roll(x, sh, 1), ident))
        sh *= 2
    return x


def _chunk_scan(x, reverse, op, ident):
    n = x.shape[1] // LANES
    parts = [_lane_scan(x[:, i * LANES:(i + 1) * LANES], reverse, op, ident) for i in range(n)]
    order = range(n - 2, -1, -1) if reverse else range(1, n)
    for i in order:
        prev = parts[i + 1][:, 0:1] if reverse else parts[i - 1][:, LANES - 1:LANES]
        parts[i] = op(parts[i], prev)
    return jnp.concatenate(parts, axis=1)


def _gate_vectors(li, lf_pre, reverse, nc):
    rows, Lc = li.shape
    hb = rows // nc
    lf = _log_sigmoid(lf_pre)
    b = _chunk_scan(lf, reverse, jnp.add, 0.0)
    r = li - b
    rcm = _chunk_scan(r, reverse, jnp.maximum, NEG_INF)
    end = slice(0, 1) if reverse else slice(Lc - 1, Lc)
    bl = jnp.broadcast_to(b[:, end], (rows, Lc))
    rmax = jnp.broadcast_to(rcm[:, end], (rows, Lc))
    m = jnp.zeros((hb, Lc), F32)
    m_parts = [None] * nc
    for j in (range(nc - 1, -1, -1) if reverse else range(nc)):
        m_parts[j] = m
        sl = slice(j * hb, (j + 1) * hb)
        m = bl[sl, :] + jnp.maximum(m, rmax[sl, :])
    m_all = jnp.concatenate(m_parts, axis=0)
    mt = jnp.maximum(m_all, rcm)
    m_end = jnp.maximum(m_all, rmax)
    return r, mt, jnp.exp(m_all - mt), jnp.exp(-(b + mt)), jnp.exp(r - m_end), jnp.exp(m_all - m_end)


def _gatevec_kernel(g_ref, gb_ref, o_ref, *, nc):
    for dr in range(2):
        li = g_ref[2 * dr] + gb_ref[2 * dr]
        lf_pre = g_ref[2 * dr + 1] + gb_ref[2 * dr + 1]
        for idx, val in enumerate(_gate_vectors(li, lf_pre, dr == 1, nc)):
            o_ref[dr * _NVEC + idx] = val


def _gatevec(gates, gbias, nc):
    _, rows, Lc = gates.shape
    return pl.pallas_call(
        functools.partial(_gatevec_kernel, nc=nc),
        out_shape=jax.ShapeDtypeStruct((2 * _NVEC, rows, Lc), F32),
        compiler_params=pltpu.CompilerParams(vmem_limit_bytes=VMEM_LIMIT),
        name="gatevec",
    )(gates, gbias)


CONV_ROWS = 128


def _mlstm_kernel(q_ref, k_ref, vt_ref, mo_ref, vec_ref, wq_ref, wk_ref, bq_ref, bk_ref, go_ref, wo_ref,
                  o_ref, wob_ref, pad_scr, qs_scr, big_scr, vt_scr, ns_scr, ct_scr, n_scr, p0_scr, p1_scr,
                  *, S, d, Lc, CW):
    wob_ref[...] = wo_ref[...].astype(BF16)
    nc = S // Lc
    half_w = CW // 2
    PADR = 8
    CR = CONV_ROWS
    N0 = Lc
    C0 = Lc + BF16_ROWS

    zero_rows = jnp.zeros((PADR, d), F32)

    def conv_silu(src_ref, w_ref, b_ref, store, scale):
        pad_scr[0:PADR, :] = zero_rows
        pad_scr[PADR + S:2 * PADR + S, :] = zero_rows

        def fill(j, c):
            r0 = pl.multiple_of(j * CR, CR)
            pad_scr[pl.ds(PADR + r0, CR), :] = src_ref[0, pl.ds(r0, CR), :].astype(F32)
            return c
        lax.fori_loop(0, S // CR, fill, 0)
        w = w_ref[...]
        bias = b_ref[...]

        def body(j, c):
            r0 = pl.multiple_of(j * CR, CR)
            win = pad_scr[pl.ds(r0, CR + 2 * PADR), :]
            y = bias
            for i in range(CW):
                o = PADR - half_w + i
                y = y + win[o:o + CR, :] * w[i:i + 1, :]
            y = y * _sigmoid(y)
            if scale != 1.0:
                y = y * scale
            store(j, r0, y.astype(BF16))
            return c
        lax.fori_loop(0, S // CR, body, 0)

    def store_q(j, r0, y):
        qs_scr[pl.ds(r0, CR), :] = y

    per = Lc // CR

    def store_k(j, r0, y):
        big_scr[j // per, pl.ds(pl.multiple_of((j % per) * CR, CR), CR), :] = y

    conv_silu(q_ref, wq_ref, bq_ref, store_q, 1.0)
    conv_silu(k_ref, wk_ref, bk_ref, store_k, d ** -0.5)

    for j in range(nc):
        vt_scr[j] = vt_ref[0, :, j * Lc:(j + 1) * Lc]

    ns_scr[...] = jnp.zeros(ns_scr.shape, F32)
    ct_scr[...] = jnp.zeros(ct_scr.shape, F32)
    n_scr[...] = jnp.zeros(n_scr.shape, F32)

    def state_step(dr, jc):
        base = dr * _NVEC
        e_row = vec_ref[0, 0, base + _E, pl.ds(jc, 1), :]
        sc = vec_ref[0, 0, base + _SC, pl.ds(jc, 1), :]
        lhs = jnp.concatenate([vt_scr[jc].astype(F32) * e_row, jnp.broadcast_to(e_row, (BF16_ROWS, Lc))], axis=0)
        upd = jnp.dot(lhs.astype(BF16), big_scr[jc, 0:Lc, :], preferred_element_type=F32)
        ct = ct_scr[dr]
        n = n_scr[dr]
        big_scr[jc, C0 + dr * d:C0 + (dr + 1) * d, :] = ct.astype(BF16)
        ns_scr[jc, dr:dr + 1, :] = n[0:1, :]
        ct_scr[dr] = sc * ct + upd[0:d, :]
        n_scr[dr] = sc * n + upd[d:d + 8, :]

    for j in range(nc):
        state_step(0, j)
        state_step(1, nc - 1 - j)
    for j in range(nc):
        big_scr[j, N0:N0 + BF16_ROWS, :] = ns_scr[j].astype(BF16)

    rr = lax.broadcasted_iota(jnp.int32, (Lc, Lc), 0)
    cc = lax.broadcasted_iota(jnp.int32, (Lc, Lc), 1)
    r1 = lax.broadcasted_iota(jnp.int32, (LANES, LANES), 0)
    c1 = lax.broadcasted_iota(jnp.int32, (LANES, LANES), 1)
    eye = r1 == c1
    gout = go_ref[...]

    def to_col(row):
        return jnp.concatenate(
            [jnp.sum(jnp.where(eye, row[:, i * LANES:(i + 1) * LANES], 0.0), axis=1, keepdims=True)
             for i in range(Lc // LANES)], axis=0)

    hrows = C0 + d

    def head(j, p_scr):
        qj = qs_scr[pl.ds(pl.multiple_of(j * Lc, Lc), Lc), :]
        p_scr[0:hrows, :] = lax.dot_general(big_scr[j, 0:hrows, :], qj, _NT, preferred_element_type=F32)
        p_scr[hrows:hrows + d, :] = lax.dot_general(big_scr[j, hrows:hrows + d, :], qj, _NT,
                                                    preferred_element_type=F32)

    def tail(j, p_scr):
        r0 = pl.multiple_of(j * Lc, Lc)
        st = p_scr[0:Lc, :]
        qkw = []
        inv_den = []
        a_rows = []
        for dr in range(2):
            base = dr * _NVEC
            r_row = vec_ref[0, 0, base + _R, pl.ds(j, 1), :]
            mt_row = vec_ref[0, 0, base + _MT, pl.ds(j, 1), :]
            a_row = vec_ref[0, 0, base + _A, pl.ds(j, 1), :]
            lim_row = vec_ref[0, 0, base + _LIM, pl.ds(j, 1), :]
            valid = (rr <= cc) if dr == 0 else (rr >= cc)
            w = jnp.exp(jnp.where(valid, to_col(r_row) - mt_row, NEG_INF))
            qkw_d = st * w
            den = a_row * p_scr[N0 + dr:N0 + dr + 1, :] + jnp.sum(qkw_d, axis=0, keepdims=True)
            inv_den.append(1.0 / jnp.maximum(jnp.abs(den), lim_row))
            a_rows.append(a_row)
            qkw.append(qkw_d.astype(BF16))
        intra = jnp.dot(vt_scr[j], jnp.concatenate(qkw, axis=1), preferred_element_type=F32)
        hs = None
        for dr in range(2):
            inter = p_scr[C0 + dr * d:C0 + (dr + 1) * d, :]
            hd = (a_rows[dr] * inter + intra[:, dr * Lc:(dr + 1) * Lc]) * inv_den[dr]
            hs = hd if hs is None else hs + hd
        mu = jnp.mean(hs, axis=0, keepdims=True)
        hc = hs - mu
        hn = (hc * lax.rsqrt(jnp.mean(hc * hc, axis=0, keepdims=True) + EPS)).T
        mo = mo_ref[0, pl.ds(r0, Lc), :].astype(F32)
        o_ref[0, pl.ds(r0, Lc), :] = (_sigmoid(mo) * hn * gout).astype(BF16)

    bufs = (p0_scr, p1_scr)
    head(0, bufs[0])
    for j in range(nc):
        if j + 1 < nc:
            head(j + 1, bufs[(j + 1) % 2])
        tail(j, bufs[j % 2])


def _mlstm(proj3, vt3, vec5, w_conv, b_conv, g_out, w_out):
    B, S, _ = proj3.shape
    H = MLSTM_HEADS
    d = g_out.shape[1] // H
    Lc = MLSTM_CHUNK
    nc = S // Lc
    CW = w_conv.shape[0]
    assert Lc % LANES == 0 and Lc % CONV_ROWS == 0 and d == Lc and CW // 2 <= 8 and nc % 2 == 0 and nc >= 4
    wo_slab = w_out.shape[0] // (B * H)
    assert wo_slab * B * H == w_out.shape[0] and wo_slab % BF16_ROWS == 0
    kern = functools.partial(_mlstm_kernel, S=S, d=d, Lc=Lc, CW=CW)
    col = lambda off: (lambda b, h: (b, 0, off + h))
    return pl.pallas_call(
        kern,
        grid=(B, H),
        in_specs=[pl.BlockSpec((1, S, d), col(0)),
                  pl.BlockSpec((1, S, d), col(H)),
                  pl.BlockSpec((1, d, S), lambda b, h: (b, h, 0)),
                  pl.BlockSpec((1, S, d), col(2 * H)),
                  pl.BlockSpec((1, 1, 2 * _NVEC, nc, Lc), lambda b, h: (b, h, 0, 0, 0)),
                  pl.BlockSpec((CW, d), lambda b, h: (0, h)),
                  pl.BlockSpec((CW, d), lambda b, h: (0, H + h)),
                  pl.BlockSpec((1, d), lambda b, h: (0, h)),
                  pl.BlockSpec((1, d), lambda b, h: (0, H + h)),
                  pl.BlockSpec((1, d), lambda b, h: (0, h)),
                  pl.BlockSpec((wo_slab, w_out.shape[1]), lambda b, h: (b * H + h, 0))],
        out_specs=[pl.BlockSpec((1, S, d), lambda b, h: (b, 0, h)),
                   pl.BlockSpec((wo_slab, w_out.shape[1]), lambda b, h: (b * H + h, 0))],
        out_shape=[jax.ShapeDtypeStruct((B, S, H * d), BF16),
                   jax.ShapeDtypeStruct(w_out.shape, BF16)],
        scratch_shapes=[pltpu.VMEM((S + 16, d), F32),
                        pltpu.VMEM((S, d), BF16),
                        pltpu.VMEM((nc, Lc + BF16_ROWS + 2 * d, d), BF16),
                        pltpu.VMEM((nc, d, Lc), BF16),
                        pltpu.VMEM((nc, BF16_ROWS, d), F32),
                        pltpu.VMEM((2, d, d), F32),
                        pltpu.VMEM((2, 8, d), F32),
                        pltpu.VMEM((Lc + BF16_ROWS + 2 * d, Lc), F32),
                        pltpu.VMEM((Lc + BF16_ROWS + 2 * d, Lc), F32)],
        compiler_params=_params(("arbitrary", "arbitrary")),
        name="mlstm",
    )(proj3, proj3, vt3, proj3, vec5, w_conv, w_conv, b_conv, b_conv, g_out, w_out)


LOG2E = 1.4426950408889634
ATTN_QBLOCKS = 4


def _attn_kernel(sink_ref, q_ref, ktp_ref, ktc_ref, ktn_ref, vtp_ref, vtc_ref, vtn_ref, g_ref, o_ref,
                 s0_scr, s1_scr, *, H, G, QB):
    n4 = pl.program_id(1)
    last4 = pl.num_programs(1) - 1
    blk = ATTN_BLOCK
    hd = ATTN_HEAD_DIM
    R = H // G
    c1 = (hd ** -0.5) * LOG2E
    rr = lax.broadcasted_iota(jnp.int32, (blk, blk), 0)
    cc = lax.broadcasted_iota(jnp.int32, (blk, blk), 1)
    ones = jnp.ones((hd, 3 * blk), BF16)

    def kv_window(i, g, prev_ref, cur_ref, next_ref):
        rs = slice(g * hd, (g + 1) * hd)
        lo = prev_ref[0, rs, :] if i == 0 else cur_ref[0, rs, (i - 1) * blk:i * blk]
        hi = next_ref[0, rs, :] if i == QB - 1 else cur_ref[0, rs, (i + 1) * blk:(i + 2) * blk]
        return jnp.concatenate([lo, cur_ref[0, rs, i * blk:(i + 1) * blk], hi], axis=1)

    def head(i, s_scr):
        q = q_ref[0, i * blk:(i + 1) * blk, :]
        for g in range(G):
            qg = jnp.concatenate([q[:, (g * R + r) * hd:(g * R + r + 1) * hd] for r in range(R)], axis=0)
            s_scr[g] = jnp.dot(qg, kv_window(i, g, ktp_ref, ktc_ref, ktn_ref), preferred_element_type=F32)

    def tail(i, s_scr):
        m_prev = (cc >= rr) if i > 0 else jnp.logical_and(cc >= rr, n4 > 0)
        m_next = (cc <= rr) if i < QB - 1 else jnp.logical_and(cc <= rr, n4 < last4)
        outs = []
        for g in range(G):
            vt = kv_window(i, g, vtp_ref, vtc_ref, vtn_ref)
            vt_aug = jnp.concatenate([vt, ones], axis=0)
            ps = []
            extras = []
            for r in range(R):
                sr = s_scr[g, r * blk:(r + 1) * blk, :] * c1
                t0 = jnp.where(m_prev, sr[:, 0:blk], NEG_INF)
                t1 = sr[:, blk:2 * blk]
                t2 = jnp.where(m_next, sr[:, 2 * blk:3 * blk], NEG_INF)
                sink2 = sink_ref[g * R + r] * LOG2E
                m = jnp.maximum(jnp.max(jnp.maximum(jnp.maximum(t0, t1), t2), axis=1, keepdims=True), sink2)
                ps.append(jnp.concatenate([jnp.exp2(t0 - m), jnp.exp2(t1 - m), jnp.exp2(t2 - m)],
                                          axis=1).astype(BF16))
                extras.append(jnp.exp2(sink2 - m))
            oa = lax.dot_general(jnp.concatenate(ps, axis=0), vt_aug, _NT, preferred_element_type=F32)
            for r in range(R):
                o_r = oa[r * blk:(r + 1) * blk, :]
                outs.append(o_r[:, 0:hd] / (o_r[:, hd:2 * hd] + extras[r]))
        oall = jnp.concatenate(outs, axis=1)
        y = oall * lax.rsqrt(jnp.mean(oall * oall, axis=-1, keepdims=True) + EPS) * g_ref[...]
        o_ref[0, i * blk:(i + 1) * blk, :] = y.astype(BF16)

    bufs = (s0_scr, s1_scr)
    head(0, bufs[0])
    for i in range(QB):
        if i + 1 < QB:
            head(i + 1, bufs[(i + 1) % 2])
        tail(i, bufs[i % 2])


def _attn(proj3, t3, sink, g_attn, q_col0, vt_row0, kt_row0):
    B, S, _ = proj3.shape
    blk = ATTN_BLOCK
    QB = ATTN_QBLOCKS
    nb = S // blk
    H = sink.shape[0]
    G = ATTN_KV_HEADS
    AW = H * ATTN_HEAD_DIM
    KW = G * ATTN_HEAD_DIM
    assert WINDOW == blk and q_col0 % AW == 0 and vt_row0 % KW == 0 and kt_row0 % KW == 0 and nb % QB == 0
    qb, vb, kb = q_col0 // AW, vt_row0 // KW, kt_row0 // KW
    kern = functools.partial(_attn_kernel, H=H, G=G, QB=QB)
    prev = lambda rb: (lambda b, n: (b, rb, jnp.maximum(n * QB - 1, 0)))
    cur = lambda rb: (lambda b, n: (b, rb, n))
    nxt = lambda rb: (lambda b, n: (b, rb, jnp.minimum(n * QB + QB, nb - 1)))
    return pl.pallas_call(
        kern,
        grid=(B, nb // QB),
        in_specs=[pl.BlockSpec(memory_space=pltpu.SMEM),
                  pl.BlockSpec((1, QB * blk, AW), lambda b, n: (b, n, qb)),
                  pl.BlockSpec((1, KW, blk), prev(kb)),
                  pl.BlockSpec((1, KW, QB * blk), cur(kb)),
                  pl.BlockSpec((1, KW, blk), nxt(kb)),
                  pl.BlockSpec((1, KW, blk), prev(vb)),
                  pl.BlockSpec((1, KW, QB * blk), cur(vb)),
                  pl.BlockSpec((1, KW, blk), nxt(vb)),
                  pl.BlockSpec((1, AW), lambda b, n: (0, 0))],
        out_specs=pl.BlockSpec((1, QB * blk, AW), lambda b, n: (b, n, 0)),
        out_shape=jax.ShapeDtypeStruct((B, S, AW), BF16),
        scratch_shapes=[pltpu.VMEM((G, (H // G) * blk, 3 * blk), F32),
                        pltpu.VMEM((G, (H // G) * blk, 3 * blk), F32)],
        compiler_params=_params(("arbitrary", "arbitrary")),
        name="attn",
    )(sink, proj3, t3, t3, t3, t3, t3, t3, g_attn)


def _outproj_kernel(m_ref, a_ref, x_ref, mod_ref, g2_ref, wt_ref, wb_ref, o_ref, h_ref, *, D):
    acc = jnp.dot(m_ref[...], wt_ref[...], preferred_element_type=F32)
    acc = acc + jnp.dot(a_ref[...], wb_ref[...], preferred_element_type=F32)
    gt = mod_ref[0, :, 2 * D:3 * D]
    x1 = x_ref[...] + gt * acc
    o_ref[...] = x1
    sh = mod_ref[0, :, 3 * D:4 * D]
    gsc = g2_ref[...] * (1.0 + mod_ref[0, :, 4 * D:5 * D])
    h_ref[...] = (x1 * lax.rsqrt(jnp.mean(x1 * x1, axis=-1, keepdims=True) + EPS) * gsc + sh).astype(BF16)


def _outproj(m2, a2, x2, mod3, g2, w_o, S, tm=512):
    N, D = x2.shape
    K = m2.shape[1]
    per_b = S // tm
    return pl.pallas_call(
        functools.partial(_outproj_kernel, D=D),
        grid=(N // tm,),
        in_specs=[pl.BlockSpec((tm, K), lambda i: (i, 0)),
                  pl.BlockSpec((tm, K), lambda i: (i, 0)),
                  pl.BlockSpec((tm, D), lambda i: (i, 0)),
                  pl.BlockSpec((1, 1, mod3.shape[2]), lambda i: (i // per_b, 0, 0)),
                  pl.BlockSpec((1, D), lambda i: (0, 0)),
                  pl.BlockSpec((K, D), lambda i: (0, 0)),
                  pl.BlockSpec((K, D), lambda i: (1, 0))],
        out_specs=[pl.BlockSpec((tm, D), lambda i: (i, 0)),
                   pl.BlockSpec((tm, D), lambda i: (i, 0))],
        out_shape=[jax.ShapeDtypeStruct((N, D), F32),
                   jax.ShapeDtypeStruct((N, D), BF16)],
        compiler_params=_params(("arbitrary",)),
        name="outproj",
    )(m2, a2, x2, mod3, g2, w_o, w_o)


def _ffn_up_kernel(h_ref, wg_ref, wu_ref, wd_ref, o_ref, wdb_ref):
    hb = h_ref[...]
    gate = jnp.dot(hb, wg_ref[...].astype(BF16), preferred_element_type=F32)
    up = jnp.dot(hb, wu_ref[...].astype(BF16), preferred_element_type=F32)
    o_ref[...] = (gate * _sigmoid(gate) * up).astype(BF16)
    wdb_ref[...] = wd_ref[...].astype(BF16)


def _ffn_up(h2, w_gate, w_up, w_down, tm=2048, tf=512):
    N, D = h2.shape
    FF = w_gate.shape[1]
    n_i, n_j = N // tm, FF // tf
    slab = FF // (n_i * n_j)
    assert slab * n_i * n_j == FF and slab % BF16_ROWS == 0
    step = lambda i, j: i * n_j + j
    return pl.pallas_call(
        _ffn_up_kernel,
        grid=(n_i, n_j),
        in_specs=[pl.BlockSpec((tm, D), lambda i, j: (i, 0)),
                  pl.BlockSpec((D, tf), lambda i, j: (0, j)),
                  pl.BlockSpec((D, tf), lambda i, j: (0, j)),
                  pl.BlockSpec((slab, D), lambda i, j: (step(i, j), 0))],
        out_specs=[pl.BlockSpec((tm, tf), lambda i, j: (i, j)),
                   pl.BlockSpec((slab, D), lambda i, j: (step(i, j), 0))],
        out_shape=[jax.ShapeDtypeStruct((N, FF), BF16),
                   jax.ShapeDtypeStruct(w_down.shape, BF16)],
        compiler_params=_params(("arbitrary", "arbitrary")),
        name="ffn_up",
    )(h2, w_gate, w_up, w_down)


def _ffn_down_kernel(u_ref, x_hbm, mod_ref, gf_ref, wd_ref, o_ref, xbuf, sem, *, D, tm):
    i = pl.program_id(0)
    j = pl.program_id(1)

    def x_copy():
        return pltpu.make_async_copy(x_hbm.at[pl.ds(pl.multiple_of(i * tm, tm), tm), :], xbuf, sem)

    last = pl.num_programs(1) - 1

    @pl.when(j == 0)
    def _():
        x_copy().start()
        o_ref[...] = jnp.dot(u_ref[...], wd_ref[...], preferred_element_type=F32)

    @pl.when(jnp.logical_and(j > 0, j < last))
    def _():
        o_ref[...] += jnp.dot(u_ref[...], wd_ref[...], preferred_element_type=F32)

    @pl.when(j == last)
    def _():
        x_copy().wait()
        gt = mod_ref[0, :, 5 * D:6 * D]
        for rows in (slice(0, tm // 2), slice(tm // 2, tm)):
            acc = o_ref[rows, :] + jnp.dot(u_ref[rows, :], wd_ref[...], preferred_element_type=F32)
            y = xbuf[rows, :] + gt * acc
            o_ref[rows, :] = y * lax.rsqrt(jnp.mean(y * y, axis=-1, keepdims=True) + EPS) * gf_ref[...]


def _ffn_down(u2, x2, mod3, gf, w_down, S, tm=1024, tk=1408):
    N, D = x2.shape
    FF = u2.shape[1]
    per_b = S // tm
    assert FF // tk >= 2, "the first and the last K step are distinct branches"
    return pl.pallas_call(
        functools.partial(_ffn_down_kernel, D=D, tm=tm),
        grid=(N // tm, FF // tk),
        in_specs=[pl.BlockSpec((tm, tk), lambda i, j: (i, j)),
                  pl.BlockSpec(memory_space=pl.ANY),
                  pl.BlockSpec((1, 1, mod3.shape[2]), lambda i, j: (i // per_b, 0, 0)),
                  pl.BlockSpec((1, D), lambda i, j: (0, 0)),
                  pl.BlockSpec((tk, D), lambda i, j: (j, 0))],
        out_specs=pl.BlockSpec((tm, D), lambda i, j: (i, 0)),
        out_shape=jax.ShapeDtypeStruct((N, D), F32),
        scratch_shapes=[pltpu.VMEM((tm, D), F32), pltpu.SemaphoreType.DMA(())],
        compiler_params=_params(("arbitrary", "arbitrary")),
        name="ffn_down",
    )(u2, x2, mod3, gf, w_down)


def kernel(x, c, positions, w_ada, b_ada, g_norm1, g_norm2, w_in, b_gates, w_conv, b_conv, g_mlstm_out,
           sink, g_attn_out, w_out, w_gate, w_up, w_down, g_final):
    B, S, D = x.shape
    depth = w_ada.shape[0]
    H = MLSTM_HEADS
    MW = D // 2
    KW = ATTN_KV_HEADS * ATTN_HEAD_DIM
    NG = 4 * H
    L = MLSTM_CHUNK
    nc = S // L
    g0 = 4 * MW
    x2 = x.reshape(B * S, D)
    pos2 = positions.reshape(B * S, 1)
    for l in range(depth):
        mod3 = _ada(c, w_ada[l], b_ada[l]).reshape(B, 1, 6 * D)
        wl = w_in[l].astype(BF16)
        a0 = g0 + NG
        t_rows = MW + 2 * KW + NG
        t_pad = -t_rows % (2 * BF16_ROWS)
        w_t = jnp.concatenate([wl[:, 2 * MW:3 * MW], wl[:, a0 + MW + KW:], wl[:, a0 + MW:a0 + MW + KW],
                               wl[:, g0:a0], jnp.zeros((D, t_pad), BF16)], axis=1).T
        proj, t3, gates_t = _inproj(x2, mod3, g_norm1[l].reshape(1, D), pos2, wl, wl[:, a0:a0 + MW], w_t, B, S, NG)
        proj3 = proj.reshape(B, S, proj.shape[1])
        gates = gates_t.reshape(4, H, B, nc, L).transpose(0, 3, 1, 2, 4).reshape(4, nc * H * B, L)
        gbias = jnp.broadcast_to(b_gates[l].reshape(4, 1, H, 1), (4, nc, H, B)).reshape(4, nc * H * B, 1)
        vec = _gatevec(gates, gbias, nc)
        vec5 = vec.reshape(2 * _NVEC, nc, H, B, L).transpose(3, 2, 0, 1, 4)
        m_out, w_out_b = _mlstm(proj3, t3, vec5, w_conv[l], b_conv[l].reshape(1, 2 * MW),
                                g_mlstm_out[l].reshape(1, MW), w_out[l])
        a_out = _attn(proj3, t3, sink[l], g_attn_out[l].reshape(1, MW), q_col0=3 * MW,
                      vt_row0=MW, kt_row0=MW + KW)
        x2, h2 = _outproj(m_out.reshape(B * S, MW), a_out.reshape(B * S, MW), x2, mod3,
                          g_norm2[l].reshape(1, D), w_out_b, S)
        assert l == depth - 1, "final norm is fused into the last layer's FFN"
        u2, w_down_b = _ffn_up(h2, w_gate[l], w_up[l], w_down[l])
        x2 = _ffn_down(u2, x2, mod3, g_final.reshape(1, D), w_down_b, S)
    return x2.reshape(B, S, D)
```

```python
import functools

import jax
import jax.numpy as jnp
from jax import lax
from jax.experimental import pallas as pl
from jax.experimental.pallas import tpu as pltpu

F32 = jnp.float32
BF16 = jnp.bfloat16

EPS = 1e-6
NEG_INF = -1e30
ROPE_THETA = 10000.0

MLSTM_HEADS = 4
MLSTM_CHUNK = 256
ATTN_HEAD_DIM = 128
ATTN_KV_HEADS = 2
WINDOW = 128
ATTN_BLOCK = 128

LANES = 128
BF16_ROWS = 16
VMEM_LIMIT = 56 * 1024 * 1024

_NT = (((1,), (1,)), ((), ()))


def _sigmoid(x):
    return 1.0 / (1.0 + jnp.exp(-x))


def _params(sem, vmem=VMEM_LIMIT):
    return pltpu.CompilerParams(dimension_semantics=sem, vmem_limit_bytes=vmem)


def _split_bf16(v):
    hi = v.astype(BF16)
    return hi, (v - hi.astype(F32)).astype(BF16)


def _ada_kernel(c_ref, w_ref, b_ref, o_ref):
    c = c_ref[...]
    s_hi, s_lo = _split_bf16(c * _sigmoid(c))
    w_hi, w_lo = _split_bf16(w_ref[...])
    dot = functools.partial(jnp.dot, preferred_element_type=F32)
    o_ref[...] = dot(s_hi, w_hi) + (dot(s_lo, w_hi) + dot(s_hi, w_lo)) + b_ref[...]


def _ada(c, w_ada, b_ada, tn=1024):
    B, D = c.shape
    N = w_ada.shape[1]
    return pl.pallas_call(
        _ada_kernel,
        grid=(N // tn,),
        in_specs=[pl.BlockSpec((B, D), lambda j: (0, 0)),
                  pl.BlockSpec((D, tn), lambda j: (0, j)),
                  pl.BlockSpec((1, tn), lambda j: (0, j))],
        out_specs=pl.BlockSpec((B, tn), lambda j: (0, j)),
        out_shape=jax.ShapeDtypeStruct((B, N), F32),
        compiler_params=_params(("arbitrary",)),
        name="ada",
    )(c, w_ada, b_ada.reshape(1, N))


def _rope(t, cos2, sin2):
    return t * cos2 + pltpu.roll(t, ATTN_HEAD_DIM // 2, 1) * sin2


PREP_SLABS = 4
PREP_STEP0 = 2


def _inproj_kernel(x_hbm, mod_ref, g_ref, pos_ref, inv_ref, sgn_ref, w_ref, wq_ref, wt_ref,
                   o_ref, t_ref, gt_ref, xbuf, h_scr, cos_scr, sin_scr, cost_scr, sint_scr, sem,
                   *, D, tm, n_tiles, n_main, rope_j0, n_t, ak_rows, gate_rows):
    r = pl.program_id(0)
    j = pl.program_id(1)
    hd = ATTN_HEAD_DIM
    rows = tm // PREP_SLABS
    ps = r % 2
    cs = 1 - ps
    tile_p = jnp.minimum(r, n_tiles - 1)

    def x_copy():
        return pltpu.make_async_copy(x_hbm.at[pl.ds(pl.multiple_of(tile_p * tm, tm), tm), :], xbuf, sem)

    @pl.when(j == 0)
    def _():
        x_copy().start()

    @pl.when(j == PREP_STEP0 - 1)
    def _():
        x_copy().wait()

    def prep_slab():
        k = j - PREP_STEP0
        r0 = pl.multiple_of(k * rows, rows)
        x = xbuf[pl.ds(r0, rows), :]
        sh = mod_ref[0, :, 0:D]
        gsc = g_ref[...] * (1.0 + mod_ref[0, :, D:2 * D])
        h_scr[ps, pl.ds(r0, rows), :] = (
            x * lax.rsqrt(jnp.mean(x * x, axis=-1, keepdims=True) + EPS) * gsc + sh).astype(BF16)
        ang = pos_ref[pl.ds(r0, rows), :].astype(F32) * inv_ref[...]
        cos2 = jnp.cos(ang)
        sin2 = jnp.sin(ang) * sgn_ref[...]
        cos_scr[ps, pl.ds(r0, rows), :] = cos2
        sin_scr[ps, pl.ds(r0, rows), :] = sin2
        cost_scr[ps, k] = cos2.T
        sint_scr[ps, k] = sin2.T

    def plain():
        o_ref[...] = jnp.dot(h_scr[cs], w_ref[...], preferred_element_type=F32).astype(BF16)

    def roped():
        acc = jnp.dot(h_scr[cs], wq_ref[...], preferred_element_type=F32)
        cos2 = cos_scr[cs]
        sin2 = sin_scr[cs]
        parts = [_rope(acc[:, hh * hd:(hh + 1) * hd], cos2, sin2) for hh in range(acc.shape[1] // hd)]
        o_ref[...] = jnp.concatenate(parts, axis=1).astype(BF16)

    tt = t_ref.shape[1]

    def transposed(ti):
        acc = lax.dot_general(wt_ref[...], h_scr[cs], _NT, preferred_element_type=F32)
        lo, hi = ti * tt, (ti + 1) * tt
        pieces = []
        cur = lo
        for a0 in range(ak_rows[0], ak_rows[1], hd):
            if a0 < lo or a0 + hd > hi:
                assert a0 + hd <= lo or a0 >= hi, "a k head straddles two transposed tiles"
                continue
            if a0 > cur:
                pieces.append(acc[cur - lo:a0 - lo, :])
            t = acc[a0 - lo:a0 - lo + hd, :]
            rot = jnp.concatenate([t[hd // 2:, :], t[:hd // 2, :]], axis=0)
            cost = jnp.concatenate([cost_scr[cs, k] for k in range(PREP_SLABS)], axis=1)
            sint = jnp.concatenate([sint_scr[cs, k] for k in range(PREP_SLABS)], axis=1)
            pieces.append(t * cost + rot * sint)
            cur = a0 + hd
        if cur < hi:
            pieces.append(acc[cur - lo:, :])
        t_ref[0] = jnp.concatenate(pieces, axis=0).astype(BF16) if len(pieces) > 1 else acc.astype(BF16)
        if lo <= gate_rows[0] and gate_rows[1] <= hi:
            gt_ref[...] = acc[gate_rows[0] - lo:gate_rows[1] - lo, :]

    mult = r > 0
    both = jnp.logical_and

    @pl.when(both(mult, j < PREP_STEP0))
    def _():
        plain()

    @pl.when(both(mult, both(j >= PREP_STEP0, j < rope_j0)))
    def _():
        plain()
        prep_slab()

    @pl.when(both(mult, both(j >= rope_j0, j < n_main)))
    def _():
        roped()
        prep_slab()

    for ti in range(n_t):
        @pl.when(both(mult, j == n_main + ti))
        def _(ti=ti):
            transposed(ti)
            prep_slab()

    @pl.when(both(jnp.logical_not(mult), j >= PREP_STEP0))
    def _():
        prep_slab()


def _inproj(x2, mod3, g1, pos2, w_all, w_aq, w_t, B, S, NG, tm=1024, tn=1024, n_t=2):
    N, D = x2.shape
    P = 4 * (D // 2)
    TR = w_t.shape[0]
    hd = ATTN_HEAD_DIM
    half = hd // 2
    inv = ROPE_THETA ** (-jnp.arange(half, dtype=F32) * 2.0 / hd)
    inv2 = jnp.concatenate([inv, inv]).reshape(1, hd)
    sgn = jnp.concatenate([-jnp.ones((half,), F32), jnp.ones((half,), F32)]).reshape(1, hd)
    MW = D // 2
    KW = ATTN_KV_HEADS * hd
    aq0 = 3 * MW
    ak_rows = (MW + KW, MW + 2 * KW)
    gate_rows = (MW + 2 * KW, MW + 2 * KW + NG)
    tt = TR // n_t
    n_tiles = N // tm
    n_main = P // tn
    n_j = n_main + n_t
    rope_j0 = aq0 // tn
    assert aq0 % tn == 0 and MW == tn and w_aq.shape == (D, MW) and TR % n_t == 0 and tt % BF16_ROWS == 0
    assert gate_rows[1] <= TR and tm % PREP_SLABS == 0 and n_j == PREP_STEP0 + PREP_SLABS
    assert PREP_STEP0 <= rope_j0
    kern = functools.partial(_inproj_kernel, D=D, tm=tm, n_tiles=n_tiles, n_main=n_main, rope_j0=rope_j0,
                             n_t=n_t, ak_rows=ak_rows, gate_rows=gate_rows)
    per_b = S // tm
    tile_p = lambda r: jnp.minimum(r, n_tiles - 1)
    tile_c = lambda r: jnp.maximum(r - 1, 0)
    live = lambda r: jnp.minimum(r, 1)
    main_j = lambda r, j: jnp.minimum(j, n_main - 1) * live(r)
    t_j = lambda r, j: jnp.maximum(j - n_main, 0) * live(r)
    return pl.pallas_call(
        kern,
        grid=(n_tiles + 1, n_j),
        in_specs=[pl.BlockSpec(memory_space=pl.ANY),
                  pl.BlockSpec((1, 1, mod3.shape[2]), lambda r, j: (tile_p(r) // per_b, 0, 0)),
                  pl.BlockSpec((1, D), lambda r, j: (0, 0)),
                  pl.BlockSpec((tm, 1), lambda r, j: (tile_p(r), 0)),
                  pl.BlockSpec((1, hd), lambda r, j: (0, 0)),
                  pl.BlockSpec((1, hd), lambda r, j: (0, 0)),
                  pl.BlockSpec((D, tn), lambda r, j: (0, jnp.where(j < 2 * MW // tn, j, 3 * MW // tn))),
                  pl.BlockSpec((D, tn), lambda r, j: (0, 0)),
                  pl.BlockSpec((tt, D), lambda r, j: (jnp.maximum(j - n_main, 0), 0))],
        out_specs=[pl.BlockSpec((tm, tn), lambda r, j: (tile_c(r), main_j(r, j))),
                   pl.BlockSpec((1, tt, tm), lambda r, j: (tile_c(r) // per_b, t_j(r, j), tile_c(r) % per_b)),
                   pl.BlockSpec((NG, tm), lambda r, j: (0, tile_c(r)))],
        out_shape=[jax.ShapeDtypeStruct((N, P), BF16),
                   jax.ShapeDtypeStruct((B, TR, S), BF16),
                   jax.ShapeDtypeStruct((NG, N), F32)],
        scratch_shapes=[pltpu.VMEM((tm, D), F32),
                        pltpu.VMEM((2, tm, D), BF16),
                        pltpu.VMEM((2, tm, hd), F32),
                        pltpu.VMEM((2, tm, hd), F32),
                        pltpu.VMEM((2, PREP_SLABS, hd, tm // PREP_SLABS), F32),
                        pltpu.VMEM((2, PREP_SLABS, hd, tm // PREP_SLABS), F32),
                        pltpu.SemaphoreType.DMA(())],
        compiler_params=_params(("arbitrary", "arbitrary")),
        name="inproj",
    )(x2, mod3, g1, pos2, inv2, sgn, w_all, w_aq, w_t)


_R, _MT, _A, _LIM, _E, _SC = range(6)
_NVEC = 6


def _log_sigmoid(x):
    return jnp.minimum(x, 0.0) - jnp.log1p(jnp.exp(-jnp.abs(x)))


def _lane_scan(x, reverse, op, ident):
    L = x.shape[1]
    lane = lax.broadcasted_iota(jnp.int32, x.shape, 1)
    sh = 1
    while sh < L:
        if reverse:
            x = op(x, jnp.where(lane < L - sh, pltpu.roll(x, L - sh, 1), ident))
        else:
            x = op(x, jnp.where(lane >= sh, pltpu.roll(x, sh, 1), ident))
        sh *= 2
    return x


def _chunk_scan(x, reverse, op, ident):
    n = x.shape[1] // LANES
    parts = [_lane_scan(x[:, i * LANES:(i + 1) * LANES], reverse, op, ident) for i in range(n)]
    order = range(n - 2, -1, -1) if reverse else range(1, n)
    for i in order:
        prev = parts[i + 1][:, 0:1] if reverse else parts[i - 1][:, LANES - 1:LANES]
        parts[i] = op(parts[i], prev)
    return jnp.concatenate(parts, axis=1)


def _gate_vectors(li, lf_pre, reverse, nc):
    rows, Lc = li.shape
    hb = rows // nc
    lf = _log_sigmoid(lf_pre)
    b = _chunk_scan(lf, reverse, jnp.add, 0.0)
    r = li - b
    rcm = _chunk_scan(r, reverse, jnp.maximum, NEG_INF)
    end = slice(0, 1) if reverse else slice(Lc - 1, Lc)
    bl = jnp.broadcast_to(b[:, end], (rows, Lc))
    rmax = jnp.broadcast_to(rcm[:, end], (rows, Lc))
    m = jnp.zeros((hb, Lc), F32)
    m_parts = [None] * nc
    for j in (range(nc - 1, -1, -1) if reverse else range(nc)):
        m_parts[j] = m
        sl = slice(j * hb, (j + 1) * hb)
        m = bl[sl, :] + jnp.maximum(m, rmax[sl, :])
    m_all = jnp.concatenate(m_parts, axis=0)
    mt = jnp.maximum(m_all, rcm)
    m_end = jnp.maximum(m_all, rmax)
    return r, mt, jnp.exp(m_all - mt), jnp.exp(-(b + mt)), jnp.exp(r - m_end), jnp.exp(m_all - m_end)


def _gatevec_kernel(g_ref, gb_ref, o_ref, *, nc):
    for dr in range(2):
        li = g_ref[2 * dr] + gb_ref[2 * dr]
        lf_pre = g_ref[2 * dr + 1] + gb_ref[2 * dr + 1]
        for idx, val in enumerate(_gate_vectors(li, lf_pre, dr == 1, nc)):
            o_ref[dr * _NVEC + idx] = val


def _gatevec(gates, gbias, nc):
    _, rows, Lc = gates.shape
    return pl.pallas_call(
        functools.partial(_gatevec_kernel, nc=nc),
        out_shape=jax.ShapeDtypeStruct((2 * _NVEC, rows, Lc), F32),
        compiler_params=pltpu.CompilerParams(vmem_limit_bytes=VMEM_LIMIT),
        name="gatevec",
    )(gates, gbias)


CONV_ROWS = 128


def _mlstm_kernel(q_ref, k_ref, vt_ref, mo_ref, vec_ref, wq_ref, wk_ref, bq_ref, bk_ref, go_ref, wo_ref,
                  o_ref, wob_ref, pad_scr, qs_scr, big_scr, vt_scr, ns_scr, ct_scr, n_scr, p0_scr, p1_scr,
                  *, S, d, Lc, CW):
    wob_ref[...] = wo_ref[...].astype(BF16)
    nc = S // Lc
    half_w = CW // 2
    PADR = 8
    CR = CONV_ROWS
    N0 = Lc
    C0 = Lc + BF16_ROWS

    zero_rows = jnp.zeros((PADR, d), F32)

    def conv_silu(src_ref, w_ref, b_ref, store, scale):
        pad_scr[0:PADR, :] = zero_rows
        pad_scr[PADR + S:2 * PADR + S, :] = zero_rows

        def fill(j, c):
            r0 = pl.multiple_of(j * CR, CR)
            pad_scr[pl.ds(PADR + r0, CR), :] = src_ref[0, pl.ds(r0, CR), :].astype(F32)
            return c
        lax.fori_loop(0, S // CR, fill, 0)
        w = w_ref[...]
        bias = b_ref[...]

        for j in range(S // CR):
            r0 = j * CR
            y = bias
            for i in range(CW):
                o = r0 + PADR - half_w + i
                y = y + pad_scr[o:o + CR, :] * w[i:i + 1, :]
            y = y * _sigmoid(y)
            if scale != 1.0:
                y = y * scale
            store(j, r0, y.astype(BF16))

    def store_q(j, r0, y):
        qs_scr[r0:r0 + CR, :] = y

    per = Lc // CR

    def store_k(j, r0, y):
        big_scr[j // per, (j % per) * CR:(j % per + 1) * CR, :] = y

    conv_silu(q_ref, wq_ref, bq_ref, store_q, 1.0)
    conv_silu(k_ref, wk_ref, bk_ref, store_k, d ** -0.5)

    for j in range(nc):
        vt_scr[j] = vt_ref[0, :, j * Lc:(j + 1) * Lc]

    ns_scr[...] = jnp.zeros(ns_scr.shape, F32)
    ct_scr[...] = jnp.zeros(ct_scr.shape, F32)
    n_scr[...] = jnp.zeros(n_scr.shape, F32)

    def state_step(dr, jc):
        base = dr * _NVEC
        e_row = vec_ref[0, 0, base + _E, pl.ds(jc, 1), :]
        sc = vec_ref[0, 0, base + _SC, pl.ds(jc, 1), :]
        lhs = jnp.concatenate([vt_scr[jc].astype(F32) * e_row, jnp.broadcast_to(e_row, (BF16_ROWS, Lc))], axis=0)
        upd = jnp.dot(lhs.astype(BF16), big_scr[jc, 0:Lc, :], preferred_element_type=F32)
        ct = ct_scr[dr]
        n = n_scr[dr]
        big_scr[jc, C0 + dr * d:C0 + (dr + 1) * d, :] = ct.astype(BF16)
        ns_scr[jc, dr:dr + 1, :] = n[0:1, :]
        ct_scr[dr] = sc * ct + upd[0:d, :]
        n_scr[dr] = sc * n + upd[d:d + 8, :]

    for j in range(nc):
        state_step(0, j)
        state_step(1, nc - 1 - j)
    for j in range(nc):
        big_scr[j, N0:N0 + BF16_ROWS, :] = ns_scr[j].astype(BF16)

    rr = lax.broadcasted_iota(jnp.int32, (Lc, Lc), 0)
    cc = lax.broadcasted_iota(jnp.int32, (Lc, Lc), 1)
    r1 = lax.broadcasted_iota(jnp.int32, (LANES, LANES), 0)
    c1 = lax.broadcasted_iota(jnp.int32, (LANES, LANES), 1)
    eye = r1 == c1
    gout = go_ref[...]

    def to_col(row):
        return jnp.concatenate(
            [jnp.sum(jnp.where(eye, row[:, i * LANES:(i + 1) * LANES], 0.0), axis=1, keepdims=True)
             for i in range(Lc // LANES)], axis=0)

    hrows = C0 + d

    def head(j, p_scr):
        qj = qs_scr[pl.ds(pl.multiple_of(j * Lc, Lc), Lc), :]
        p_scr[0:hrows, :] = lax.dot_general(big_scr[j, 0:hrows, :], qj, _NT, preferred_element_type=F32)
        p_scr[hrows:hrows + d, :] = lax.dot_general(big_scr[j, hrows:hrows + d, :], qj, _NT,
                                                    preferred_element_type=F32)

    def tail(j, p_scr):
        r0 = pl.multiple_of(j * Lc, Lc)
        st = p_scr[0:Lc, :]
        qkw = []
        inv_den = []
        a_rows = []
        for dr in range(2):
            base = dr * _NVEC
            r_row = vec_ref[0, 0, base + _R, pl.ds(j, 1), :]
            mt_row = vec_ref[0, 0, base + _MT, pl.ds(j, 1), :]
            a_row = vec_ref[0, 0, base + _A, pl.ds(j, 1), :]
            lim_row = vec_ref[0, 0, base + _LIM, pl.ds(j, 1), :]
            valid = (rr <= cc) if dr == 0 else (rr >= cc)
            w = jnp.exp(jnp.where(valid, to_col(r_row) - mt_row, NEG_INF))
            qkw_d = st * w
            den = a_row * p_scr[N0 + dr:N0 + dr + 1, :] + jnp.sum(qkw_d, axis=0, keepdims=True)
            inv_den.append(1.0 / jnp.maximum(jnp.abs(den), lim_row))
            a_rows.append(a_row)
            qkw.append(qkw_d.astype(BF16))
        intra = jnp.dot(vt_scr[j], jnp.concatenate(qkw, axis=1), preferred_element_type=F32)
        hs = None
        for dr in range(2):
            inter = p_scr[C0 + dr * d:C0 + (dr + 1) * d, :]
            hd = (a_rows[dr] * inter + intra[:, dr * Lc:(dr + 1) * Lc]) * inv_den[dr]
            hs = hd if hs is None else hs + hd
        mu = jnp.mean(hs, axis=0, keepdims=True)
        hc = hs - mu
        hn = (hc * lax.rsqrt(jnp.mean(hc * hc, axis=0, keepdims=True) + EPS)).T
        mo = mo_ref[0, pl.ds(r0, Lc), :].astype(F32)
        o_ref[0, pl.ds(r0, Lc), :] = (_sigmoid(mo) * hn * gout).astype(BF16)

    bufs = (p0_scr, p1_scr)
    head(0, bufs[0])
    for j in range(nc):
        if j + 1 < nc:
            head(j + 1, bufs[(j + 1) % 2])
        tail(j, bufs[j % 2])


def _mlstm(proj3, vt3, vec5, w_conv, b_conv, g_out, w_out):
    B, S, _ = proj3.shape
    H = MLSTM_HEADS
    d = g_out.shape[1] // H
    Lc = MLSTM_CHUNK
    nc = S // Lc
    CW = w_conv.shape[0]
    assert Lc % LANES == 0 and Lc % CONV_ROWS == 0 and d == Lc and CW // 2 <= 8 and nc % 2 == 0 and nc >= 4
    wo_slab = w_out.shape[0] // (B * H)
    assert wo_slab * B * H == w_out.shape[0] and wo_slab % BF16_ROWS == 0
    kern = functools.partial(_mlstm_kernel, S=S, d=d, Lc=Lc, CW=CW)
    col = lambda off: (lambda b, h: (b, 0, off + h))
    return pl.pallas_call(
        kern,
        grid=(B, H),
        in_specs=[pl.BlockSpec((1, S, d), col(0)),
                  pl.BlockSpec((1, S, d), col(H)),
                  pl.BlockSpec((1, d, S), lambda b, h: (b, h, 0)),
                  pl.BlockSpec((1, S, d), col(2 * H)),
                  pl.BlockSpec((1, 1, 2 * _NVEC, nc, Lc), lambda b, h: (b, h, 0, 0, 0)),
                  pl.BlockSpec((CW, d), lambda b, h: (0, h)),
                  pl.BlockSpec((CW, d), lambda b, h: (0, H + h)),
                  pl.BlockSpec((1, d), lambda b, h: (0, h)),
                  pl.BlockSpec((1, d), lambda b, h: (0, H + h)),
                  pl.BlockSpec((1, d), lambda b, h: (0, h)),
                  pl.BlockSpec((wo_slab, w_out.shape[1]), lambda b, h: (b * H + h, 0))],
        out_specs=[pl.BlockSpec((1, S, d), lambda b, h: (b, 0, h)),
                   pl.BlockSpec((wo_slab, w_out.shape[1]), lambda b, h: (b * H + h, 0))],
        out_shape=[jax.ShapeDtypeStruct((B, S, H * d), BF16),
                   jax.ShapeDtypeStruct(w_out.shape, BF16)],
        scratch_shapes=[pltpu.VMEM((S + 16, d), F32),
                        pltpu.VMEM((S, d), BF16),
                        pltpu.VMEM((nc, Lc + BF16_ROWS + 2 * d, d), BF16),
                        pltpu.VMEM((nc, d, Lc), BF16),
                        pltpu.VMEM((nc, BF16_ROWS, d), F32),
                        pltpu.VMEM((2, d, d), F32),
                        pltpu.VMEM((2, 8, d), F32),
                        pltpu.VMEM((Lc + BF16_ROWS + 2 * d, Lc), F32),
                        pltpu.VMEM((Lc + BF16_ROWS + 2 * d, Lc), F32)],
        compiler_params=_params(("arbitrary", "arbitrary")),
        name="mlstm",
    )(proj3, proj3, vt3, proj3, vec5, w_conv, w_conv, b_conv, b_conv, g_out, w_out)


LOG2E = 1.4426950408889634
ATTN_QBLOCKS = 4


def _attn_kernel(sink_ref, q_ref, ktp_ref, ktc_ref, ktn_ref, vtp_ref, vtc_ref, vtn_ref, g_ref, o_ref,
                 s0_scr, s1_scr, *, H, G, QB):
    n4 = pl.program_id(1)
    last4 = pl.num_programs(1) - 1
    blk = ATTN_BLOCK
    hd = ATTN_HEAD_DIM
    R = H // G
    c1 = (hd ** -0.5) * LOG2E
    rr = lax.broadcasted_iota(jnp.int32, (blk, blk), 0)
    cc = lax.broadcasted_iota(jnp.int32, (blk, blk), 1)
    ones = jnp.ones((hd, 3 * blk), BF16)

    def kv_window(i, g, prev_ref, cur_ref, next_ref):
        rs = slice(g * hd, (g + 1) * hd)
        lo = prev_ref[0, rs, :] if i == 0 else cur_ref[0, rs, (i - 1) * blk:i * blk]
        hi = next_ref[0, rs, :] if i == QB - 1 else cur_ref[0, rs, (i + 1) * blk:(i + 2) * blk]
        return jnp.concatenate([lo, cur_ref[0, rs, i * blk:(i + 1) * blk], hi], axis=1)

    def head(i, s_scr):
        q = q_ref[0, i * blk:(i + 1) * blk, :]
        for g in range(G):
            qg = jnp.concatenate([q[:, (g * R + r) * hd:(g * R + r + 1) * hd] for r in range(R)], axis=0)
            s_scr[g] = jnp.dot(qg, kv_window(i, g, ktp_ref, ktc_ref, ktn_ref), preferred_element_type=F32)

    def tail(i, s_scr):
        m_prev = (cc >= rr) if i > 0 else jnp.logical_and(cc >= rr, n4 > 0)
        m_next = (cc <= rr) if i < QB - 1 else jnp.logical_and(cc <= rr, n4 < last4)
        outs = []
        for g in range(G):
            vt = kv_window(i, g, vtp_ref, vtc_ref, vtn_ref)
            vt_aug = jnp.concatenate([vt, ones], axis=0)
            ps = []
            extras = []
            for r in range(R):
                sr = s_scr[g, r * blk:(r + 1) * blk, :] * c1
                t0 = jnp.where(m_prev, sr[:, 0:blk], NEG_INF)
                t1 = sr[:, blk:2 * blk]
                t2 = jnp.where(m_next, sr[:, 2 * blk:3 * blk], NEG_INF)
                sink2 = sink_ref[g * R + r] * LOG2E
                m = jnp.maximum(jnp.max(jnp.maximum(jnp.maximum(t0, t1), t2), axis=1, keepdims=True), sink2)
                ps.append(jnp.concatenate([jnp.exp2(t0 - m), jnp.exp2(t1 - m), jnp.exp2(t2 - m)],
                                          axis=1).astype(BF16))
                extras.append(jnp.exp2(sink2 - m))
            oa = lax.dot_general(jnp.concatenate(ps, axis=0), vt_aug, _NT, preferred_element_type=F32)
            for r in range(R):
                o_r = oa[r * blk:(r + 1) * blk, :]
                outs.append(o_r[:, 0:hd] / (o_r[:, hd:2 * hd] + extras[r]))
        oall = jnp.concatenate(outs, axis=1)
        y = oall * lax.rsqrt(jnp.mean(oall * oall, axis=-1, keepdims=True) + EPS) * g_ref[...]
        o_ref[0, i * blk:(i + 1) * blk, :] = y.astype(BF16)

    bufs = (s0_scr, s1_scr)
    head(0, bufs[0])
    for i in range(QB):
        if i + 1 < QB:
            head(i + 1, bufs[(i + 1) % 2])
        tail(i, bufs[i % 2])


def _attn(proj3, t3, sink, g_attn, q_col0, vt_row0, kt_row0):
    B, S, _ = proj3.shape
    blk = ATTN_BLOCK
    QB = ATTN_QBLOCKS
    nb = S // blk
    H = sink.shape[0]
    G = ATTN_KV_HEADS
    AW = H * ATTN_HEAD_DIM
    KW = G * ATTN_HEAD_DIM
    assert WINDOW == blk and q_col0 % AW == 0 and vt_row0 % KW == 0 and kt_row0 % KW == 0 and nb % QB == 0
    qb, vb, kb = q_col0 // AW, vt_row0 // KW, kt_row0 // KW
    kern = functools.partial(_attn_kernel, H=H, G=G, QB=QB)
    prev = lambda rb: (lambda b, n: (b, rb, jnp.maximum(n * QB - 1, 0)))
    cur = lambda rb: (lambda b, n: (b, rb, n))
    nxt = lambda rb: (lambda b, n: (b, rb, jnp.minimum(n * QB + QB, nb - 1)))
    return pl.pallas_call(
        kern,
        grid=(B, nb // QB),
        in_specs=[pl.BlockSpec(memory_space=pltpu.SMEM),
                  pl.BlockSpec((1, QB * blk, AW), lambda b, n: (b, n, qb)),
                  pl.BlockSpec((1, KW, blk), prev(kb)),
                  pl.BlockSpec((1, KW, QB * blk), cur(kb)),
                  pl.BlockSpec((1, KW, blk), nxt(kb)),
                  pl.BlockSpec((1, KW, blk), prev(vb)),
                  pl.BlockSpec((1, KW, QB * blk), cur(vb)),
                  pl.BlockSpec((1, KW, blk), nxt(vb)),
                  pl.BlockSpec((1, AW), lambda b, n: (0, 0))],
        out_specs=pl.BlockSpec((1, QB * blk, AW), lambda b, n: (b, n, 0)),
        out_shape=jax.ShapeDtypeStruct((B, S, AW), BF16),
        scratch_shapes=[pltpu.VMEM((G, (H // G) * blk, 3 * blk), F32),
                        pltpu.VMEM((G, (H // G) * blk, 3 * blk), F32)],
        compiler_params=_params(("arbitrary", "arbitrary")),
        name="attn",
    )(sink, proj3, t3, t3, t3, t3, t3, t3, g_attn)


def _outproj_kernel(m_ref, a_ref, x_ref, mod_ref, g2_ref, wt_ref, wb_ref, o_ref, h_ref, *, D):
    acc = jnp.dot(m_ref[...], wt_ref[...], preferred_element_type=F32)
    acc = acc + jnp.dot(a_ref[...], wb_ref[...], preferred_element_type=F32)
    gt = mod_ref[0, :, 2 * D:3 * D]
    x1 = x_ref[...] + gt * acc
    o_ref[...] = x1
    sh = mod_ref[0, :, 3 * D:4 * D]
    gsc = g2_ref[...] * (1.0 + mod_ref[0, :, 4 * D:5 * D])
    h_ref[...] = (x1 * lax.rsqrt(jnp.mean(x1 * x1, axis=-1, keepdims=True) + EPS) * gsc + sh).astype(BF16)


def _outproj(m2, a2, x2, mod3, g2, w_o, S, tm=512):
    N, D = x2.shape
    K = m2.shape[1]
    per_b = S // tm
    return pl.pallas_call(
        functools.partial(_outproj_kernel, D=D),
        grid=(N // tm,),
        in_specs=[pl.BlockSpec((tm, K), lambda i: (i, 0)),
                  pl.BlockSpec((tm, K), lambda i: (i, 0)),
                  pl.BlockSpec((tm, D), lambda i: (i, 0)),
                  pl.BlockSpec((1, 1, mod3.shape[2]), lambda i: (i // per_b, 0, 0)),
                  pl.BlockSpec((1, D), lambda i: (0, 0)),
                  pl.BlockSpec((K, D), lambda i: (0, 0)),
                  pl.BlockSpec((K, D), lambda i: (1, 0))],
        out_specs=[pl.BlockSpec((tm, D), lambda i: (i, 0)),
                   pl.BlockSpec((tm, D), lambda i: (i, 0))],
        out_shape=[jax.ShapeDtypeStruct((N, D), F32),
                   jax.ShapeDtypeStruct((N, D), BF16)],
        compiler_params=_params(("arbitrary",)),
        name="outproj",
    )(m2, a2, x2, mod3, g2, w_o, w_o)


def _ffn_up_kernel(h_ref, wg_ref, wu_ref, wd_ref, o_ref, wdb_ref):
    hb = h_ref[...]
    tf = o_ref.shape[1]
    cw = 2 * LANES
    for c in range(tf // cw):
        cs = slice(c * cw, (c + 1) * cw)
        w = jnp.concatenate([wg_ref[:, cs].astype(BF16), wu_ref[:, cs].astype(BF16)], axis=1)
        gu = jnp.dot(hb, w, preferred_element_type=F32)
        gate = gu[:, 0:cw]
        o_ref[:, cs] = (gate * _sigmoid(gate) * gu[:, cw:2 * cw]).astype(BF16)
    wdb_ref[...] = wd_ref[...].astype(BF16)


def _ffn_up(h2, w_gate, w_up, w_down, tm=2048, tf=512):
    N, D = h2.shape
    FF = w_gate.shape[1]
    n_i, n_j = N // tm, FF // tf
    slab = FF // (n_i * n_j)
    assert slab * n_i * n_j == FF and slab % BF16_ROWS == 0
    step = lambda i, j: i * n_j + j
    return pl.pallas_call(
        _ffn_up_kernel,
        grid=(n_i, n_j),
        in_specs=[pl.BlockSpec((tm, D), lambda i, j: (i, 0)),
                  pl.BlockSpec((D, tf), lambda i, j: (0, j)),
                  pl.BlockSpec((D, tf), lambda i, j: (0, j)),
                  pl.BlockSpec((slab, D), lambda i, j: (step(i, j), 0))],
        out_specs=[pl.BlockSpec((tm, tf), lambda i, j: (i, j)),
                   pl.BlockSpec((slab, D), lambda i, j: (step(i, j), 0))],
        out_shape=[jax.ShapeDtypeStruct((N, FF), BF16),
                   jax.ShapeDtypeStruct(w_down.shape, BF16)],
        compiler_params=_params(("arbitrary", "arbitrary")),
        name="ffn_up",
    )(h2, w_gate, w_up, w_down)


def _ffn_down_kernel(u_ref, x_hbm, mod_ref, gf_ref, wd_ref, o_ref, xbuf, sem, *, D, tm):
    i = pl.program_id(0)
    j = pl.program_id(1)

    def x_copy():
        return pltpu.make_async_copy(x_hbm.at[pl.ds(pl.multiple_of(i * tm, tm), tm), :], xbuf, sem)

    last = pl.num_programs(1) - 1

    @pl.when(j == 0)
    def _():
        x_copy().start()
        o_ref[...] = jnp.dot(u_ref[...], wd_ref[...], preferred_element_type=F32)

    @pl.when(jnp.logical_and(j > 0, j < last))
    def _():
        o_ref[...] += jnp.dot(u_ref[...], wd_ref[...], preferred_element_type=F32)

    @pl.when(j == last)
    def _():
        x_copy().wait()
        gt = mod_ref[0, :, 5 * D:6 * D]
        for rows in (slice(0, tm // 2), slice(tm // 2, tm)):
            acc = o_ref[rows, :] + jnp.dot(u_ref[rows, :], wd_ref[...], preferred_element_type=F32)
            y = xbuf[rows, :] + gt * acc
            o_ref[rows, :] = y * lax.rsqrt(jnp.mean(y * y, axis=-1, keepdims=True) + EPS) * gf_ref[...]


def _ffn_down(u2, x2, mod3, gf, w_down, S, tm=1024, tk=1408):
    N, D = x2.shape
    FF = u2.shape[1]
    per_b = S // tm
    assert FF // tk >= 2, "the first and the last K step are distinct branches"
    return pl.pallas_call(
        functools.partial(_ffn_down_kernel, D=D, tm=tm),
        grid=(N // tm, FF // tk),
        in_specs=[pl.BlockSpec((tm, tk), lambda i, j: (i, j)),
                  pl.BlockSpec(memory_space=pl.ANY),
                  pl.BlockSpec((1, 1, mod3.shape[2]), lambda i, j: (i // per_b, 0, 0)),
                  pl.BlockSpec((1, D), lambda i, j: (0, 0)),
                  pl.BlockSpec((tk, D), lambda i, j: (j, 0))],
        out_specs=pl.BlockSpec((tm, D), lambda i, j: (i, 0)),
        out_shape=jax.ShapeDtypeStruct((N, D), F32),
        scratch_shapes=[pltpu.VMEM((tm, D), F32), pltpu.SemaphoreType.DMA(())],
        compiler_params=_params(("arbitrary", "arbitrary")),
        name="ffn_down",
    )(u2, x2, mod3, gf, w_down)


def kernel(x, c, positions, w_ada, b_ada, g_norm1, g_norm2, w_in, b_gates, w_conv, b_conv, g_mlstm_out,
           sink, g_attn_out, w_out, w_gate, w_up, w_down, g_final):
    B, S, D = x.shape
    depth = w_ada.shape[0]
    H = MLSTM_HEADS
    MW = D // 2
    KW = ATTN_KV_HEADS * ATTN_HEAD_DIM
    NG = 4 * H
    L = MLSTM_CHUNK
    nc = S // L
    g0 = 4 * MW
    x2 = x.reshape(B * S, D)
    pos2 = positions.reshape(B * S, 1)
    for l in range(depth):
        mod3 = _ada(c, w_ada[l], b_ada[l]).reshape(B, 1, 6 * D)
        wl = w_in[l].astype(BF16)
        a0 = g0 + NG
        t_rows = MW + 2 * KW + NG
        t_pad = -t_rows % (2 * BF16_ROWS)
        w_t = jnp.concatenate([wl[:, 2 * MW:3 * MW], wl[:, a0 + MW + KW:], wl[:, a0 + MW:a0 + MW + KW],
                               wl[:, g0:a0], jnp.zeros((D, t_pad), BF16)], axis=1).T
        proj, t3, gates_t = _inproj(x2, mod3, g_norm1[l].reshape(1, D), pos2, wl, wl[:, a0:a0 + MW], w_t, B, S, NG)
        proj3 = proj.reshape(B, S, proj.shape[1])
        gates = gates_t.reshape(4, H, B, nc, L).transpose(0, 3, 1, 2, 4).reshape(4, nc * H * B, L)
        gbias = jnp.broadcast_to(b_gates[l].reshape(4, 1, H, 1), (4, nc, H, B)).reshape(4, nc * H * B, 1)
        vec = _gatevec(gates, gbias, nc)
        vec5 = vec.reshape(2 * _NVEC, nc, H, B, L).transpose(3, 2, 0, 1, 4)
        m_out, w_out_b = _mlstm(proj3, t3, vec5, w_conv[l], b_conv[l].reshape(1, 2 * MW),
                                g_mlstm_out[l].reshape(1, MW), w_out[l])
        a_out = _attn(proj3, t3, sink[l], g_attn_out[l].reshape(1, MW), q_col0=3 * MW,
                      vt_row0=MW, kt_row0=MW + KW)
        x2, h2 = _outproj(m_out.reshape(B * S, MW), a_out.reshape(B * S, MW), x2, mod3,
                          g_norm2[l].reshape(1, D), w_out_b, S)
        assert l == depth - 1, "final norm is fused into the last layer's FFN"
        u2, w_down_b = _ffn_up(h2, w_gate[l], w_up[l], w_down[l])
        x2 = _ffn_down(u2, x2, mod3, g_final.reshape(1, D), w_down_b, S)
    return x2.reshape(B, S, D)
```

```python
import functools

import jax
import jax.numpy as jnp
from jax import lax
from jax.experimental import pallas as pl
from jax.experimental.pallas import tpu as pltpu

F32 = jnp.float32
BF16 = jnp.bfloat16

EPS = 1e-6
NEG_INF = -1e30
ROPE_THETA = 10000.0

MLSTM_HEADS = 4
MLSTM_CHUNK = 256
ATTN_HEAD_DIM = 128
ATTN_KV_HEADS = 2
WINDOW = 128
ATTN_BLOCK = 128

LANES = 128
BF16_ROWS = 16
VMEM_LIMIT = 56 * 1024 * 1024

_NT = (((1,), (1,)), ((), ()))


def _sigmoid(x):
    return 1.0 / (1.0 + jnp.exp(-x))


def _params(sem, vmem=VMEM_LIMIT):
    return pltpu.CompilerParams(dimension_semantics=sem, vmem_limit_bytes=vmem)


def _split_bf16(v):
    hi = v.astype(BF16)
    return hi, (v - hi.astype(F32)).astype(BF16)


def _ada_kernel(c_ref, w_ref, b_ref, o_ref):
    c = c_ref[...]
    s_hi, s_lo = _split_bf16(c * _sigmoid(c))
    w_hi, w_lo = _split_bf16(w_ref[...])
    dot = functools.partial(jnp.dot, preferred_element_type=F32)
    o_ref[...] = dot(s_hi, w_hi) + (dot(s_lo, w_hi) + dot(s_hi, w_lo)) + b_ref[...]


def _ada(c, w_ada, b_ada, tn=1024):
    B, D = c.shape
    N = w_ada.shape[1]
    return pl.pallas_call(
        _ada_kernel,
        grid=(N // tn,),
        in_specs=[pl.BlockSpec((B, D), lambda j: (0, 0)),
                  pl.BlockSpec((D, tn), lambda j: (0, j)),
                  pl.BlockSpec((1, tn), lambda j: (0, j))],
        out_specs=pl.BlockSpec((B, tn), lambda j: (0, j)),
        out_shape=jax.ShapeDtypeStruct((B, N), F32),
        compiler_params=_params(("arbitrary",)),
        name="ada",
    )(c, w_ada, b_ada.reshape(1, N))


def _rope(t, cos2, sin2):
    return t * cos2 + pltpu.roll(t, ATTN_HEAD_DIM // 2, 1) * sin2


PREP_SLABS = 4
PREP_STEP0 = 2


def _inproj_kernel(x_hbm, mod_ref, g_ref, pos_ref, inv_ref, sgn_ref, w_ref, wq_ref, wt_ref,
                   o_ref, t_ref, gt_ref, xbuf, h_scr, cos_scr, sin_scr, cost_scr, sint_scr, sem,
                   *, D, tm, n_tiles, n_main, rope_j0, n_t, ak_rows, gate_rows):
    r = pl.program_id(0)
    j = pl.program_id(1)
    hd = ATTN_HEAD_DIM
    rows = tm // PREP_SLABS
    ps = r % 2
    cs = 1 - ps
    tile_p = jnp.minimum(r, n_tiles - 1)

    def x_copy():
        return pltpu.make_async_copy(x_hbm.at[pl.ds(pl.multiple_of(tile_p * tm, tm), tm), :], xbuf, sem)

    @pl.when(j == 0)
    def _():
        x_copy().start()

    @pl.when(j == PREP_STEP0 - 1)
    def _():
        x_copy().wait()

    def prep_slab():
        k = j - PREP_STEP0
        r0 = pl.multiple_of(k * rows, rows)
        x = xbuf[pl.ds(r0, rows), :]
        sh = mod_ref[0, :, 0:D]
        gsc = g_ref[...] * (1.0 + mod_ref[0, :, D:2 * D])
        h_scr[ps, pl.ds(r0, rows), :] = (
            x * lax.rsqrt(jnp.mean(x * x, axis=-1, keepdims=True) + EPS) * gsc + sh).astype(BF16)
        posf = pos_ref[pl.ds(r0, rows), :].astype(F32)
        lane = lax.broadcasted_iota(jnp.int32, (rows // 2, hd), 1)
        low = lane < hd // 2
        ang = jnp.where(low, posf[0:rows // 2, :], posf[rows // 2:rows, :]) * inv_ref[...]
        cp = jnp.cos(ang)
        sp = jnp.sin(ang)
        cpr = pltpu.roll(cp, hd // 2, 1)
        spr = pltpu.roll(sp, hd // 2, 1)
        cos2 = jnp.concatenate([jnp.where(low, cp, cpr), jnp.where(low, cpr, cp)], axis=0)
        sin2 = jnp.concatenate([jnp.where(low, sp, spr), jnp.where(low, spr, sp)], axis=0) * sgn_ref[...]
        cos_scr[ps, pl.ds(r0, rows), :] = cos2
        sin_scr[ps, pl.ds(r0, rows), :] = sin2
        cost_scr[ps, k] = cos2.T
        sint_scr[ps, k] = sin2.T

    def plain():
        o_ref[...] = jnp.dot(h_scr[cs], w_ref[...], preferred_element_type=F32).astype(BF16)

    def roped():
        acc = jnp.dot(h_scr[cs], wq_ref[...], preferred_element_type=F32)
        cos2 = cos_scr[cs]
        sin2 = sin_scr[cs]
        parts = [_rope(acc[:, hh * hd:(hh + 1) * hd], cos2, sin2) for hh in range(acc.shape[1] // hd)]
        o_ref[...] = jnp.concatenate(parts, axis=1).astype(BF16)

    tt = t_ref.shape[1]

    def transposed(ti):
        acc = lax.dot_general(wt_ref[...], h_scr[cs], _NT, preferred_element_type=F32)
        lo, hi = ti * tt, (ti + 1) * tt
        pieces = []
        cur = lo
        for a0 in range(ak_rows[0], ak_rows[1], hd):
            if a0 < lo or a0 + hd > hi:
                assert a0 + hd <= lo or a0 >= hi, "a k head straddles two transposed tiles"
                continue
            if a0 > cur:
                pieces.append(acc[cur - lo:a0 - lo, :])
            t = acc[a0 - lo:a0 - lo + hd, :]
            rot = jnp.concatenate([t[hd // 2:, :], t[:hd // 2, :]], axis=0)
            cost = jnp.concatenate([cost_scr[cs, k] for k in range(PREP_SLABS)], axis=1)
            sint = jnp.concatenate([sint_scr[cs, k] for k in range(PREP_SLABS)], axis=1)
            pieces.append(t * cost + rot * sint)
            cur = a0 + hd
        if cur < hi:
            pieces.append(acc[cur - lo:, :])
        t_ref[0] = jnp.concatenate(pieces, axis=0).astype(BF16) if len(pieces) > 1 else acc.astype(BF16)
        if lo <= gate_rows[0] and gate_rows[1] <= hi:
            gt_ref[...] = acc[gate_rows[0] - lo:gate_rows[1] - lo, :]

    mult = r > 0
    both = jnp.logical_and

    @pl.when(both(mult, j < PREP_STEP0))
    def _():
        plain()

    @pl.when(both(mult, both(j >= PREP_STEP0, j < rope_j0)))
    def _():
        plain()
        prep_slab()

    @pl.when(both(mult, both(j >= rope_j0, j < n_main)))
    def _():
        roped()
        prep_slab()

    for ti in range(n_t):
        @pl.when(both(mult, j == n_main + ti))
        def _(ti=ti):
            transposed(ti)
            prep_slab()

    @pl.when(both(jnp.logical_not(mult), j >= PREP_STEP0))
    def _():
        prep_slab()


def _inproj(x2, mod3, g1, pos2, w_all, w_aq, w_t, B, S, NG, tm=1024, tn=1024, n_t=2):
    N, D = x2.shape
    P = 4 * (D // 2)
    TR = w_t.shape[0]
    hd = ATTN_HEAD_DIM
    half = hd // 2
    inv = ROPE_THETA ** (-jnp.arange(half, dtype=F32) * 2.0 / hd)
    inv2 = jnp.concatenate([inv, inv]).reshape(1, hd)
    sgn = jnp.concatenate([-jnp.ones((half,), F32), jnp.ones((half,), F32)]).reshape(1, hd)
    MW = D // 2
    KW = ATTN_KV_HEADS * hd
    aq0 = 3 * MW
    ak_rows = (MW + KW, MW + 2 * KW)
    gate_rows = (MW + 2 * KW, MW + 2 * KW + NG)
    tt = TR // n_t
    n_tiles = N // tm
    n_main = P // tn
    n_j = n_main + n_t
    rope_j0 = aq0 // tn
    assert aq0 % tn == 0 and MW == tn and w_aq.shape == (D, MW) and TR % n_t == 0 and tt % BF16_ROWS == 0
    assert gate_rows[1] <= TR and tm % PREP_SLABS == 0 and n_j == PREP_STEP0 + PREP_SLABS
    assert PREP_STEP0 <= rope_j0
    kern = functools.partial(_inproj_kernel, D=D, tm=tm, n_tiles=n_tiles, n_main=n_main, rope_j0=rope_j0,
                             n_t=n_t, ak_rows=ak_rows, gate_rows=gate_rows)
    per_b = S // tm
    tile_p = lambda r: jnp.minimum(r, n_tiles - 1)
    tile_c = lambda r: jnp.maximum(r - 1, 0)
    live = lambda r: jnp.minimum(r, 1)
    main_j = lambda r, j: jnp.minimum(j, n_main - 1) * live(r)
    t_j = lambda r, j: jnp.maximum(j - n_main, 0) * live(r)
    return pl.pallas_call(
        kern,
        grid=(n_tiles + 1, n_j),
        in_specs=[pl.BlockSpec(memory_space=pl.ANY),
                  pl.BlockSpec((1, 1, mod3.shape[2]), lambda r, j: (tile_p(r) // per_b, 0, 0)),
                  pl.BlockSpec((1, D), lambda r, j: (0, 0)),
                  pl.BlockSpec((tm, 1), lambda r, j: (tile_p(r), 0)),
                  pl.BlockSpec((1, hd), lambda r, j: (0, 0)),
                  pl.BlockSpec((1, hd), lambda r, j: (0, 0)),
                  pl.BlockSpec((D, tn), lambda r, j: (0, jnp.where(j < 2 * MW // tn, j, 3 * MW // tn))),
                  pl.BlockSpec((D, tn), lambda r, j: (0, 0)),
                  pl.BlockSpec((tt, D), lambda r, j: (jnp.maximum(j - n_main, 0), 0))],
        out_specs=[pl.BlockSpec((tm, tn), lambda r, j: (tile_c(r), main_j(r, j))),
                   pl.BlockSpec((1, tt, tm), lambda r, j: (tile_c(r) // per_b, t_j(r, j), tile_c(r) % per_b)),
                   pl.BlockSpec((NG, tm), lambda r, j: (0, tile_c(r)))],
        out_shape=[jax.ShapeDtypeStruct((N, P), BF16),
                   jax.ShapeDtypeStruct((B, TR, S), BF16),
                   jax.ShapeDtypeStruct((NG, N), F32)],
        scratch_shapes=[pltpu.VMEM((tm, D), F32),
                        pltpu.VMEM((2, tm, D), BF16),
                        pltpu.VMEM((2, tm, hd), F32),
                        pltpu.VMEM((2, tm, hd), F32),
                        pltpu.VMEM((2, PREP_SLABS, hd, tm // PREP_SLABS), F32),
                        pltpu.VMEM((2, PREP_SLABS, hd, tm // PREP_SLABS), F32),
                        pltpu.SemaphoreType.DMA(())],
        compiler_params=_params(("arbitrary", "arbitrary")),
        name="inproj",
    )(x2, mod3, g1, pos2, inv2, sgn, w_all, w_aq, w_t)


_R, _MT, _A, _LIM, _E, _SC = range(6)
_NVEC = 6


def _log_sigmoid(x):
    return jnp.minimum(x, 0.0) - jnp.log1p(jnp.exp(-jnp.abs(x)))


def _lane_scan(x, reverse, op, ident):
    L = x.shape[1]
    lane = lax.broadcasted_iota(jnp.int32, x.shape, 1)
    sh = 1
    while sh < L:
        if reverse:
            x = op(x, jnp.where(lane < L - sh, pltpu.roll(x, L - sh, 1), ident))
        else:
            x = op(x, jnp.where(lane >= sh, pltpu.roll(x, sh, 1), ident))
        sh *= 2
    return x


def _chunk_scan(x, reverse, op, ident):
    n = x.shape[1] // LANES
    parts = [_lane_scan(x[:, i * LANES:(i + 1) * LANES], reverse, op, ident) for i in range(n)]
    order = range(n - 2, -1, -1) if reverse else range(1, n)
    for i in order:
        prev = parts[i + 1][:, 0:1] if reverse else parts[i - 1][:, LANES - 1:LANES]
        parts[i] = op(parts[i], prev)
    return jnp.concatenate(parts, axis=1)


def _gate_vectors(li, lf_pre, reverse, nc):
    rows, Lc = li.shape
    hb = rows // nc
    lf = _log_sigmoid(lf_pre)
    b = _chunk_scan(lf, reverse, jnp.add, 0.0)
    r = li - b
    rcm = _chunk_scan(r, reverse, jnp.maximum, NEG_INF)
    end = slice(0, 1) if reverse else slice(Lc - 1, Lc)
    bl = jnp.broadcast_to(b[:, end], (rows, Lc))
    rmax = jnp.broadcast_to(rcm[:, end], (rows, Lc))
    m = jnp.zeros((hb, Lc), F32)
    m_parts = [None] * nc
    for j in (range(nc - 1, -1, -1) if reverse else range(nc)):
        m_parts[j] = m
        sl = slice(j * hb, (j + 1) * hb)
        m = bl[sl, :] + jnp.maximum(m, rmax[sl, :])
    m_all = jnp.concatenate(m_parts, axis=0)
    mt = jnp.maximum(m_all, rcm)
    m_end = jnp.maximum(m_all, rmax)
    return r, mt, jnp.exp(m_all - mt), jnp.exp(-(b + mt)), jnp.exp(r - m_end), jnp.exp(m_all - m_end)


def _gatevec_kernel(g_ref, gb_ref, o_ref, *, nc):
    for dr in range(2):
        li = g_ref[2 * dr] + gb_ref[2 * dr]
        lf_pre = g_ref[2 * dr + 1] + gb_ref[2 * dr + 1]
        for idx, val in enumerate(_gate_vectors(li, lf_pre, dr == 1, nc)):
            o_ref[dr * _NVEC + idx] = val


def _gatevec(gates, gbias, nc):
    _, rows, Lc = gates.shape
    return pl.pallas_call(
        functools.partial(_gatevec_kernel, nc=nc),
        out_shape=jax.ShapeDtypeStruct((2 * _NVEC, rows, Lc), F32),
        compiler_params=pltpu.CompilerParams(vmem_limit_bytes=VMEM_LIMIT),
        name="gatevec",
    )(gates, gbias)


CONV_ROWS = 128


def _mlstm_kernel(q_ref, k_ref, vt_ref, mo_ref, vec_ref, wq_ref, wk_ref, bq_ref, bk_ref, go_ref, wo_ref,
                  o_ref, wob_ref, pad_scr, qs_scr, big_scr, vt_scr, ns_scr, ct_scr, n_scr, p0_scr, p1_scr,
                  *, S, d, Lc, CW):
    wob_ref[...] = wo_ref[...].astype(BF16)
    nc = S // Lc
    half_w = CW // 2
    PADR = 8
    CR = CONV_ROWS
    N0 = Lc
    C0 = Lc + BF16_ROWS

    zero_rows = jnp.zeros((PADR, d), F32)

    def conv_silu(src_ref, w_ref, b_ref, store, scale):
        pad_scr[0:PADR, :] = zero_rows
        pad_scr[PADR + S:2 * PADR + S, :] = zero_rows

        def fill(j, c):
            r0 = pl.multiple_of(j * CR, CR)
            pad_scr[pl.ds(PADR + r0, CR), :] = src_ref[0, pl.ds(r0, CR), :].astype(F32)
            return c
        lax.fori_loop(0, S // CR, fill, 0)
        w = w_ref[...]
        bias = b_ref[...]

        for j in range(S // CR):
            r0 = j * CR
            y = bias
            for i in range(CW):
                o = r0 + PADR - half_w + i
                y = y + pad_scr[o:o + CR, :] * w[i:i + 1, :]
            y = y * _sigmoid(y)
            if scale != 1.0:
                y = y * scale
            store(j, r0, y.astype(BF16))

    def store_q(j, r0, y):
        qs_scr[r0:r0 + CR, :] = y

    per = Lc // CR

    def store_k(j, r0, y):
        big_scr[j // per, (j % per) * CR:(j % per + 1) * CR, :] = y

    conv_silu(q_ref, wq_ref, bq_ref, store_q, 1.0)
    conv_silu(k_ref, wk_ref, bk_ref, store_k, d ** -0.5)

    for j in range(nc):
        vt_scr[j] = vt_ref[0, :, j * Lc:(j + 1) * Lc]

    ns_scr[...] = jnp.zeros(ns_scr.shape, F32)
    ct_scr[...] = jnp.zeros(ct_scr.shape, F32)
    n_scr[...] = jnp.zeros(n_scr.shape, F32)

    def state_step(dr, jc):
        base = dr * _NVEC
        e_row = vec_ref[0, 0, base + _E, pl.ds(jc, 1), :]
        sc = vec_ref[0, 0, base + _SC, pl.ds(jc, 1), :]
        lhs = jnp.concatenate([vt_scr[jc].astype(F32) * e_row, jnp.broadcast_to(e_row, (BF16_ROWS, Lc))], axis=0)
        upd = jnp.dot(lhs.astype(BF16), big_scr[jc, 0:Lc, :], preferred_element_type=F32)
        ct = ct_scr[dr]
        n = n_scr[dr]
        big_scr[jc, C0 + dr * d:C0 + (dr + 1) * d, :] = ct.astype(BF16)
        ns_scr[jc, dr:dr + 1, :] = n[0:1, :]
        ct_scr[dr] = sc * ct + upd[0:d, :]
        n_scr[dr] = sc * n + upd[d:d + 8, :]

    for j in range(nc):
        state_step(0, j)
        state_step(1, nc - 1 - j)
    for j in range(nc):
        big_scr[j, N0:N0 + BF16_ROWS, :] = ns_scr[j].astype(BF16)

    rr = lax.broadcasted_iota(jnp.int32, (Lc, Lc), 0)
    cc = lax.broadcasted_iota(jnp.int32, (Lc, Lc), 1)
    r1 = lax.broadcasted_iota(jnp.int32, (LANES, LANES), 0)
    c1 = lax.broadcasted_iota(jnp.int32, (LANES, LANES), 1)
    eye = r1 == c1
    gout = go_ref[...]

    def to_col(row):
        return jnp.concatenate(
            [jnp.sum(jnp.where(eye, row[:, i * LANES:(i + 1) * LANES], 0.0), axis=1, keepdims=True)
             for i in range(Lc // LANES)], axis=0)

    hrows = C0 + d

    def head(j, p_scr):
        qj = qs_scr[pl.ds(pl.multiple_of(j * Lc, Lc), Lc), :]
        p_scr[0:hrows, :] = lax.dot_general(big_scr[j, 0:hrows, :], qj, _NT, preferred_element_type=F32)
        p_scr[hrows:hrows + d, :] = lax.dot_general(big_scr[j, hrows:hrows + d, :], qj, _NT,
                                                    preferred_element_type=F32)

    def tail(j, p_scr):
        r0 = pl.multiple_of(j * Lc, Lc)
        st = p_scr[0:Lc, :]
        qkw = []
        inv_den = []
        a_rows = []
        for dr in range(2):
            base = dr * _NVEC
            r_row = vec_ref[0, 0, base + _R, pl.ds(j, 1), :]
            mt_row = vec_ref[0, 0, base + _MT, pl.ds(j, 1), :]
            a_row = vec_ref[0, 0, base + _A, pl.ds(j, 1), :]
            lim_row = vec_ref[0, 0, base + _LIM, pl.ds(j, 1), :]
            valid = (rr <= cc) if dr == 0 else (rr >= cc)
            w = jnp.exp(jnp.where(valid, to_col(r_row) - mt_row, NEG_INF))
            qkw_d = st * w
            den = a_row * p_scr[N0 + dr:N0 + dr + 1, :] + jnp.sum(qkw_d, axis=0, keepdims=True)
            inv_den.append(1.0 / jnp.maximum(jnp.abs(den), lim_row))
            a_rows.append(a_row)
            qkw.append(qkw_d.astype(BF16))
        intra = jnp.dot(vt_scr[j], jnp.concatenate(qkw, axis=1), preferred_element_type=F32)
        hs = None
        for dr in range(2):
            inter = p_scr[C0 + dr * d:C0 + (dr + 1) * d, :]
            hd = (a_rows[dr] * inter + intra[:, dr * Lc:(dr + 1) * Lc]) * inv_den[dr]
            hs = hd if hs is None else hs + hd
        mu = jnp.mean(hs, axis=0, keepdims=True)
        hc = hs - mu
        hn = (hc * lax.rsqrt(jnp.mean(hc * hc, axis=0, keepdims=True) + EPS)).T
        mo = mo_ref[0, pl.ds(r0, Lc), :].astype(F32)
        o_ref[0, pl.ds(r0, Lc), :] = (_sigmoid(mo) * hn * gout).astype(BF16)

    bufs = (p0_scr, p1_scr)
    head(0, bufs[0])
    for j in range(nc):
        if j + 1 < nc:
            head(j + 1, bufs[(j + 1) % 2])
        tail(j, bufs[j % 2])


def _mlstm(proj3, vt3, vec5, w_conv, b_conv, g_out, w_out):
    B, S, _ = proj3.shape
    H = MLSTM_HEADS
    d = g_out.shape[1] // H
    Lc = MLSTM_CHUNK
    nc = S // Lc
    CW = w_conv.shape[0]
    assert Lc % LANES == 0 and Lc % CONV_ROWS == 0 and d == Lc and CW // 2 <= 8 and nc % 2 == 0 and nc >= 4
    wo_slab = w_out.shape[0] // (B * H)
    assert wo_slab * B * H == w_out.shape[0] and wo_slab % BF16_ROWS == 0
    kern = functools.partial(_mlstm_kernel, S=S, d=d, Lc=Lc, CW=CW)
    col = lambda off: (lambda b, h: (b, 0, off + h))
    return pl.pallas_call(
        kern,
        grid=(B, H),
        in_specs=[pl.BlockSpec((1, S, d), col(0)),
                  pl.BlockSpec((1, S, d), col(H)),
                  pl.BlockSpec((1, d, S), lambda b, h: (b, h, 0)),
                  pl.BlockSpec((1, S, d), col(2 * H)),
                  pl.BlockSpec((1, 1, 2 * _NVEC, nc, Lc), lambda b, h: (b, h, 0, 0, 0)),
                  pl.BlockSpec((CW, d), lambda b, h: (0, h)),
                  pl.BlockSpec((CW, d), lambda b, h: (0, H + h)),
                  pl.BlockSpec((1, d), lambda b, h: (0, h)),
                  pl.BlockSpec((1, d), lambda b, h: (0, H + h)),
                  pl.BlockSpec((1, d), lambda b, h: (0, h)),
                  pl.BlockSpec((wo_slab, w_out.shape[1]), lambda b, h: (b * H + h, 0))],
        out_specs=[pl.BlockSpec((1, S, d), lambda b, h: (b, 0, h)),
                   pl.BlockSpec((wo_slab, w_out.shape[1]), lambda b, h: (b * H + h, 0))],
        out_shape=[jax.ShapeDtypeStruct((B, S, H * d), BF16),
                   jax.ShapeDtypeStruct(w_out.shape, BF16)],
        scratch_shapes=[pltpu.VMEM((S + 16, d), F32),
                        pltpu.VMEM((S, d), BF16),
                        pltpu.VMEM((nc, Lc + BF16_ROWS + 2 * d, d), BF16),
                        pltpu.VMEM((nc, d, Lc), BF16),
                        pltpu.VMEM((nc, BF16_ROWS, d), F32),
                        pltpu.VMEM((2, d, d), F32),
                        pltpu.VMEM((2, 8, d), F32),
                        pltpu.VMEM((Lc + BF16_ROWS + 2 * d, Lc), F32),
                        pltpu.VMEM((Lc + BF16_ROWS + 2 * d, Lc), F32)],
        compiler_params=_params(("arbitrary", "arbitrary")),
        name="mlstm",
    )(proj3, proj3, vt3, proj3, vec5, w_conv, w_conv, b_conv, b_conv, g_out, w_out)


LOG2E = 1.4426950408889634
ATTN_QBLOCKS = 4


def _attn_kernel(sink_ref, q_ref, ktp_ref, ktc_ref, ktn_ref, vtp_ref, vtc_ref, vtn_ref, g_ref, o_ref,
                 s0_scr, s1_scr, *, H, G, QB):
    n4 = pl.program_id(1)
    last4 = pl.num_programs(1) - 1
    blk = ATTN_BLOCK
    hd = ATTN_HEAD_DIM
    R = H // G
    c1 = (hd ** -0.5) * LOG2E
    rr = lax.broadcasted_iota(jnp.int32, (blk, blk), 0)
    cc = lax.broadcasted_iota(jnp.int32, (blk, blk), 1)
    ones = jnp.ones((hd, 3 * blk), BF16)

    def kv_window(i, g, prev_ref, cur_ref, next_ref):
        rs = slice(g * hd, (g + 1) * hd)
        lo = prev_ref[0, rs, :] if i == 0 else cur_ref[0, rs, (i - 1) * blk:i * blk]
        hi = next_ref[0, rs, :] if i == QB - 1 else cur_ref[0, rs, (i + 1) * blk:(i + 2) * blk]
        return jnp.concatenate([lo, cur_ref[0, rs, i * blk:(i + 1) * blk], hi], axis=1)

    def head(i, s_scr):
        q = q_ref[0, i * blk:(i + 1) * blk, :]
        for g in range(G):
            qg = jnp.concatenate([q[:, (g * R + r) * hd:(g * R + r + 1) * hd] for r in range(R)], axis=0)
            s_scr[g] = jnp.dot(qg, kv_window(i, g, ktp_ref, ktc_ref, ktn_ref), preferred_element_type=F32)

    def tail(i, s_scr):
        m_prev = (cc >= rr) if i > 0 else jnp.logical_and(cc >= rr, n4 > 0)
        m_next = (cc <= rr) if i < QB - 1 else jnp.logical_and(cc <= rr, n4 < last4)
        outs = []
        for g in range(G):
            vt = kv_window(i, g, vtp_ref, vtc_ref, vtn_ref)
            vt_aug = jnp.concatenate([vt, ones], axis=0)
            ps = []
            extras = []
            for r in range(R):
                sr = s_scr[g, r * blk:(r + 1) * blk, :] * c1
                t0 = jnp.where(m_prev, sr[:, 0:blk], NEG_INF)
                t1 = sr[:, blk:2 * blk]
                t2 = jnp.where(m_next, sr[:, 2 * blk:3 * blk], NEG_INF)
                sink2 = sink_ref[g * R + r] * LOG2E
                m = jnp.maximum(jnp.max(jnp.maximum(jnp.maximum(t0, t1), t2), axis=1, keepdims=True), sink2)
                ps.append(jnp.concatenate([jnp.exp2(t0 - m), jnp.exp2(t1 - m), jnp.exp2(t2 - m)],
                                          axis=1).astype(BF16))
                extras.append(jnp.exp2(sink2 - m))
            oa = lax.dot_general(jnp.concatenate(ps, axis=0), vt_aug, _NT, preferred_element_type=F32)
            for r in range(R):
                o_r = oa[r * blk:(r + 1) * blk, :]
                outs.append(o_r[:, 0:hd] / (o_r[:, hd:2 * hd] + extras[r]))
        oall = jnp.concatenate(outs, axis=1)
        y = oall * lax.rsqrt(jnp.mean(oall * oall, axis=-1, keepdims=True) + EPS) * g_ref[...]
        o_ref[0, i * blk:(i + 1) * blk, :] = y.astype(BF16)

    bufs = (s0_scr, s1_scr)
    head(0, bufs[0])
    for i in range(QB):
        if i + 1 < QB:
            head(i + 1, bufs[(i + 1) % 2])
        tail(i, bufs[i % 2])


def _attn(proj3, t3, sink, g_attn, q_col0, vt_row0, kt_row0):
    B, S, _ = proj3.shape
    blk = ATTN_BLOCK
    QB = ATTN_QBLOCKS
    nb = S // blk
    H = sink.shape[0]
    G = ATTN_KV_HEADS
    AW = H * ATTN_HEAD_DIM
    KW = G * ATTN_HEAD_DIM
    assert WINDOW == blk and q_col0 % AW == 0 and vt_row0 % KW == 0 and kt_row0 % KW == 0 and nb % QB == 0
    qb, vb, kb = q_col0 // AW, vt_row0 // KW, kt_row0 // KW
    kern = functools.partial(_attn_kernel, H=H, G=G, QB=QB)
    prev = lambda rb: (lambda b, n: (b, rb, jnp.maximum(n * QB - 1, 0)))
    cur = lambda rb: (lambda b, n: (b, rb, n))
    nxt = lambda rb: (lambda b, n: (b, rb, jnp.minimum(n * QB + QB, nb - 1)))
    return pl.pallas_call(
        kern,
        grid=(B, nb // QB),
        in_specs=[pl.BlockSpec(memory_space=pltpu.SMEM),
                  pl.BlockSpec((1, QB * blk, AW), lambda b, n: (b, n, qb)),
                  pl.BlockSpec((1, KW, blk), prev(kb)),
                  pl.BlockSpec((1, KW, QB * blk), cur(kb)),
                  pl.BlockSpec((1, KW, blk), nxt(kb)),
                  pl.BlockSpec((1, KW, blk), prev(vb)),
                  pl.BlockSpec((1, KW, QB * blk), cur(vb)),
                  pl.BlockSpec((1, KW, blk), nxt(vb)),
                  pl.BlockSpec((1, AW), lambda b, n: (0, 0))],
        out_specs=pl.BlockSpec((1, QB * blk, AW), lambda b, n: (b, n, 0)),
        out_shape=jax.ShapeDtypeStruct((B, S, AW), BF16),
        scratch_shapes=[pltpu.VMEM((G, (H // G) * blk, 3 * blk), F32),
                        pltpu.VMEM((G, (H // G) * blk, 3 * blk), F32)],
        compiler_params=_params(("arbitrary", "arbitrary")),
        name="attn",
    )(sink, proj3, t3, t3, t3, t3, t3, t3, g_attn)


def _outproj_kernel(m_ref, a_ref, x_ref, mod_ref, g2_ref, wt_ref, wb_ref, o_ref, h_ref, *, D):
    acc = jnp.dot(m_ref[...], wt_ref[...], preferred_element_type=F32)
    acc = acc + jnp.dot(a_ref[...], wb_ref[...], preferred_element_type=F32)
    gt = mod_ref[0, :, 2 * D:3 * D]
    x1 = x_ref[...] + gt * acc
    o_ref[...] = x1
    sh = mod_ref[0, :, 3 * D:4 * D]
    gsc = g2_ref[...] * (1.0 + mod_ref[0, :, 4 * D:5 * D])
    h_ref[...] = (x1 * lax.rsqrt(jnp.mean(x1 * x1, axis=-1, keepdims=True) + EPS) * gsc + sh).astype(BF16)


def _outproj(m2, a2, x2, mod3, g2, w_o, S, tm=512):
    N, D = x2.shape
    K = m2.shape[1]
    per_b = S // tm
    return pl.pallas_call(
        functools.partial(_outproj_kernel, D=D),
        grid=(N // tm,),
        in_specs=[pl.BlockSpec((tm, K), lambda i: (i, 0)),
                  pl.BlockSpec((tm, K), lambda i: (i, 0)),
                  pl.BlockSpec((tm, D), lambda i: (i, 0)),
                  pl.BlockSpec((1, 1, mod3.shape[2]), lambda i: (i // per_b, 0, 0)),
                  pl.BlockSpec((1, D), lambda i: (0, 0)),
                  pl.BlockSpec((K, D), lambda i: (0, 0)),
                  pl.BlockSpec((K, D), lambda i: (1, 0))],
        out_specs=[pl.BlockSpec((tm, D), lambda i: (i, 0)),
                   pl.BlockSpec((tm, D), lambda i: (i, 0))],
        out_shape=[jax.ShapeDtypeStruct((N, D), F32),
                   jax.ShapeDtypeStruct((N, D), BF16)],
        compiler_params=_params(("arbitrary",)),
        name="outproj",
    )(m2, a2, x2, mod3, g2, w_o, w_o)


def _ffn_up_kernel(h_ref, wg_ref, wu_ref, wd_ref, o_ref, wdb_ref):
    hb = h_ref[...]
    tf = o_ref.shape[1]
    cw = 2 * LANES
    for c in range(tf // cw):
        cs = slice(c * cw, (c + 1) * cw)
        w = jnp.concatenate([wg_ref[:, cs].astype(BF16), wu_ref[:, cs].astype(BF16)], axis=1)
        gu = jnp.dot(hb, w, preferred_element_type=F32)
        gate = gu[:, 0:cw]
        o_ref[:, cs] = (gate * _sigmoid(gate) * gu[:, cw:2 * cw]).astype(BF16)
    wdb_ref[...] = wd_ref[...].astype(BF16)


def _ffn_up(h2, w_gate, w_up, w_down, tm=2048, tf=512):
    N, D = h2.shape
    FF = w_gate.shape[1]
    n_i, n_j = N // tm, FF // tf
    slab = FF // (n_i * n_j)
    assert slab * n_i * n_j == FF and slab % BF16_ROWS == 0
    step = lambda i, j: i * n_j + j
    return pl.pallas_call(
        _ffn_up_kernel,
        grid=(n_i, n_j),
        in_specs=[pl.BlockSpec((tm, D), lambda i, j: (i, 0)),
                  pl.BlockSpec((D, tf), lambda i, j: (0, j)),
                  pl.BlockSpec((D, tf), lambda i, j: (0, j)),
                  pl.BlockSpec((slab, D), lambda i, j: (step(i, j), 0))],
        out_specs=[pl.BlockSpec((tm, tf), lambda i, j: (i, j)),
                   pl.BlockSpec((slab, D), lambda i, j: (step(i, j), 0))],
        out_shape=[jax.ShapeDtypeStruct((N, FF), BF16),
                   jax.ShapeDtypeStruct(w_down.shape, BF16)],
        compiler_params=_params(("arbitrary", "arbitrary")),
        name="ffn_up",
    )(h2, w_gate, w_up, w_down)


def _ffn_down_kernel(u_ref, x_hbm, mod_ref, gf_ref, wd_ref, o_ref, xbuf, sem, *, D, tm):
    i = pl.program_id(0)
    j = pl.program_id(1)

    def x_copy():
        return pltpu.make_async_copy(x_hbm.at[pl.ds(pl.multiple_of(i * tm, tm), tm), :], xbuf, sem)

    last = pl.num_programs(1) - 1

    @pl.when(j == 0)
    def _():
        x_copy().start()
        o_ref[...] = jnp.dot(u_ref[...], wd_ref[...], preferred_element_type=F32)

    @pl.when(jnp.logical_and(j > 0, j < last))
    def _():
        o_ref[...] += jnp.dot(u_ref[...], wd_ref[...], preferred_element_type=F32)

    @pl.when(j == last)
    def _():
        x_copy().wait()
        gt = mod_ref[0, :, 5 * D:6 * D]
        for rows in (slice(0, tm // 2), slice(tm // 2, tm)):
            acc = o_ref[rows, :] + jnp.dot(u_ref[rows, :], wd_ref[...], preferred_element_type=F32)
            y = xbuf[rows, :] + gt * acc
            o_ref[rows, :] = y * lax.rsqrt(jnp.mean(y * y, axis=-1, keepdims=True) + EPS) * gf_ref[...]


def _ffn_down(u2, x2, mod3, gf, w_down, S, tm=1024, tk=1408):
    N, D = x2.shape
    FF = u2.shape[1]
    per_b = S // tm
    assert FF // tk >= 2, "the first and the last K step are distinct branches"
    return pl.pallas_call(
        functools.partial(_ffn_down_kernel, D=D, tm=tm),
        grid=(N // tm, FF // tk),
        in_specs=[pl.BlockSpec((tm, tk), lambda i, j: (i, j)),
                  pl.BlockSpec(memory_space=pl.ANY),
                  pl.BlockSpec((1, 1, mod3.shape[2]), lambda i, j: (i // per_b, 0, 0)),
                  pl.BlockSpec((1, D), lambda i, j: (0, 0)),
                  pl.BlockSpec((tk, D), lambda i, j: (j, 0))],
        out_specs=pl.BlockSpec((tm, D), lambda i, j: (i, 0)),
        out_shape=jax.ShapeDtypeStruct((N, D), F32),
        scratch_shapes=[pltpu.VMEM((tm, D), F32), pltpu.SemaphoreType.DMA(())],
        compiler_params=_params(("arbitrary", "arbitrary")),
        name="ffn_down",
    )(u2, x2, mod3, gf, w_down)


def kernel(x, c, positions, w_ada, b_ada, g_norm1, g_norm2, w_in, b_gates, w_conv, b_conv, g_mlstm_out,
           sink, g_attn_out, w_out, w_gate, w_up, w_down, g_final):
    B, S, D = x.shape
    depth = w_ada.shape[0]
    H = MLSTM_HEADS
    MW = D // 2
    KW = ATTN_KV_HEADS * ATTN_HEAD_DIM
    NG = 4 * H
    L = MLSTM_CHUNK
    nc = S // L
    g0 = 4 * MW
    x2 = x.reshape(B * S, D)
    pos2 = positions.reshape(B * S, 1)
    for l in range(depth):
        mod3 = _ada(c, w_ada[l], b_ada[l]).reshape(B, 1, 6 * D)
        wl = w_in[l].astype(BF16)
        a0 = g0 + NG
        t_rows = MW + 2 * KW + NG
        t_pad = -t_rows % (2 * BF16_ROWS)
        w_t = jnp.concatenate([wl[:, 2 * MW:3 * MW], wl[:, a0 + MW + KW:], wl[:, a0 + MW:a0 + MW + KW],
                               wl[:, g0:a0], jnp.zeros((D, t_pad), BF16)], axis=1).T
        proj, t3, gates_t = _inproj(x2, mod3, g_norm1[l].reshape(1, D), pos2, wl, wl[:, a0:a0 + MW], w_t, B, S, NG)
        proj3 = proj.reshape(B, S, proj.shape[1])
        gates = gates_t.reshape(4, H, B, nc, L).transpose(0, 3, 1, 2, 4).reshape(4, nc * H * B, L)
        gbias = jnp.broadcast_to(b_gates[l].reshape(4, 1, H, 1), (4, nc, H, B)).reshape(4, nc * H * B, 1)
        vec = _gatevec(gates, gbias, nc)
        vec5 = vec.reshape(2 * _NVEC, nc, H, B, L).transpose(3, 2, 0, 1, 4)
        m_out, w_out_b = _mlstm(proj3, t3, vec5, w_conv[l], b_conv[l].reshape(1, 2 * MW),
                                g_mlstm_out[l].reshape(1, MW), w_out[l])
        a_out = _attn(proj3, t3, sink[l], g_attn_out[l].reshape(1, MW), q_col0=3 * MW,
                      vt_row0=MW, kt_row0=MW + KW)
        x2, h2 = _outproj(m_out.reshape(B * S, MW), a_out.reshape(B * S, MW), x2, mod3,
                          g_norm2[l].reshape(1, D), w_out_b, S)
        assert l == depth - 1, "final norm is fused into the last layer's FFN"
        u2, w_down_b = _ffn_up(h2, w_gate[l], w_up[l], w_down[l])
        x2 = _ffn_down(u2, x2, mod3, g_final.reshape(1, D), w_down_b, S)
    return x2.reshape(B, S, D)
```

```python
import functools

import jax
import jax.numpy as jnp
from jax import lax
from jax.experimental import pallas as pl
from jax.experimental.pallas import tpu as pltpu

F32 = jnp.float32
BF16 = jnp.bfloat16

EPS = 1e-6
NEG_INF = -1e30
ROPE_THETA = 10000.0

MLSTM_HEADS = 4
MLSTM_CHUNK = 256
ATTN_HEAD_DIM = 128
ATTN_KV_HEADS = 2
WINDOW = 128
ATTN_BLOCK = 128

LANES = 128
BF16_ROWS = 16
VMEM_LIMIT = 56 * 1024 * 1024

_NT = (((1,), (1,)), ((), ()))


def _sigmoid(x):
    return 1.0 / (1.0 + jnp.exp(-x))


def _params(sem, vmem=VMEM_LIMIT):
    return pltpu.CompilerParams(dimension_semantics=sem, vmem_limit_bytes=vmem)


def _split_bf16(v):
    hi = v.astype(BF16)
    return hi, (v - hi.astype(F32)).astype(BF16)


def _ada_kernel(c_ref, w_ref, b_ref, o_ref):
    c = c_ref[...]
    s_hi, s_lo = _split_bf16(c * _sigmoid(c))
    w_hi, w_lo = _split_bf16(w_ref[...])
    dot = functools.partial(jnp.dot, preferred_element_type=F32)
    o_ref[...] = dot(s_hi, w_hi) + (dot(s_lo, w_hi) + dot(s_hi, w_lo)) + b_ref[...]


def _ada(c, w_ada, b_ada, tn=1024):
    B, D = c.shape
    N = w_ada.shape[1]
    return pl.pallas_call(
        _ada_kernel,
        grid=(N // tn,),
        in_specs=[pl.BlockSpec((B, D), lambda j: (0, 0)),
                  pl.BlockSpec((D, tn), lambda j: (0, j)),
                  pl.BlockSpec((1, tn), lambda j: (0, j))],
        out_specs=pl.BlockSpec((B, tn), lambda j: (0, j)),
        out_shape=jax.ShapeDtypeStruct((B, N), F32),
        compiler_params=_params(("arbitrary",)),
        name="ada",
    )(c, w_ada, b_ada.reshape(1, N))


def _rope(t, cos2, sin2):
    return t * cos2 + pltpu.roll(t, ATTN_HEAD_DIM // 2, 1) * sin2


PREP_SLABS = 4
PREP_STEP0 = 2


def _inproj_kernel(x_hbm, mod_ref, g_ref, pos_ref, inv_ref, sgn_ref, w_ref, wq_ref, wt_ref,
                   o_ref, t_ref, gt_ref, xbuf, h_scr, cos_scr, sin_scr, cost_scr, sint_scr, sem,
                   *, D, tm, n_tiles, n_main, rope_j0, n_t, ak_rows, gate_rows):
    r = pl.program_id(0)
    j = pl.program_id(1)
    hd = ATTN_HEAD_DIM
    rows = tm // PREP_SLABS
    ps = r % 2
    cs = 1 - ps
    tile_p = jnp.minimum(r, n_tiles - 1)

    def x_copy():
        return pltpu.make_async_copy(x_hbm.at[pl.ds(pl.multiple_of(tile_p * tm, tm), tm), :], xbuf, sem)

    @pl.when(j == 0)
    def _():
        x_copy().start()

    @pl.when(j == PREP_STEP0 - 1)
    def _():
        x_copy().wait()

    def prep_slab():
        k = j - PREP_STEP0
        r0 = pl.multiple_of(k * rows, rows)
        x = xbuf[pl.ds(r0, rows), :]
        sh = mod_ref[0, :, 0:D]
        gsc = g_ref[...] * (1.0 + mod_ref[0, :, D:2 * D])
        h_scr[ps, pl.ds(r0, rows), :] = (
            x * lax.rsqrt(jnp.mean(x * x, axis=-1, keepdims=True) + EPS) * gsc + sh).astype(BF16)
        posf = pos_ref[pl.ds(r0, rows), :].astype(F32)
        lane = lax.broadcasted_iota(jnp.int32, (rows // 2, hd), 1)
        low = lane < hd // 2
        ang = jnp.where(low, posf[0:rows // 2, :], posf[rows // 2:rows, :]) * inv_ref[...]
        cp = jnp.cos(ang)
        sp = jnp.sin(ang)
        cpr = pltpu.roll(cp, hd // 2, 1)
        spr = pltpu.roll(sp, hd // 2, 1)
        cos2 = jnp.concatenate([jnp.where(low, cp, cpr), jnp.where(low, cpr, cp)], axis=0)
        sin2 = jnp.concatenate([jnp.where(low, sp, spr), jnp.where(low, spr, sp)], axis=0) * sgn_ref[...]
        cos_scr[ps, pl.ds(r0, rows), :] = cos2
        sin_scr[ps, pl.ds(r0, rows), :] = sin2
        cost_scr[ps, k] = cos2.T
        sint_scr[ps, k] = sin2.T

    def plain():
        o_ref[...] = jnp.dot(h_scr[cs], w_ref[...], preferred_element_type=F32).astype(BF16)

    def roped():
        acc = jnp.dot(h_scr[cs], wq_ref[...], preferred_element_type=F32)
        cos2 = cos_scr[cs]
        sin2 = sin_scr[cs]
        parts = [_rope(acc[:, hh * hd:(hh + 1) * hd], cos2, sin2) for hh in range(acc.shape[1] // hd)]
        o_ref[...] = jnp.concatenate(parts, axis=1).astype(BF16)

    tt = t_ref.shape[1]

    def transposed(ti):
        acc = lax.dot_general(wt_ref[...], h_scr[cs], _NT, preferred_element_type=F32)
        lo, hi = ti * tt, (ti + 1) * tt
        pieces = []
        cur = lo
        for a0 in range(ak_rows[0], ak_rows[1], hd):
            if a0 < lo or a0 + hd > hi:
                assert a0 + hd <= lo or a0 >= hi, "a k head straddles two transposed tiles"
                continue
            if a0 > cur:
                pieces.append(acc[cur - lo:a0 - lo, :])
            t = acc[a0 - lo:a0 - lo + hd, :]
            rot = jnp.concatenate([t[hd // 2:, :], t[:hd // 2, :]], axis=0)
            cost = jnp.concatenate([cost_scr[cs, k] for k in range(PREP_SLABS)], axis=1)
            sint = jnp.concatenate([sint_scr[cs, k] for k in range(PREP_SLABS)], axis=1)
            pieces.append(t * cost + rot * sint)
            cur = a0 + hd
        if cur < hi:
            pieces.append(acc[cur - lo:, :])
        t_ref[0] = jnp.concatenate(pieces, axis=0).astype(BF16) if len(pieces) > 1 else acc.astype(BF16)
        if lo <= gate_rows[0] and gate_rows[1] <= hi:
            gt_ref[...] = acc[gate_rows[0] - lo:gate_rows[1] - lo, :]

    mult = r > 0
    both = jnp.logical_and

    @pl.when(both(mult, j < PREP_STEP0))
    def _():
        plain()

    @pl.when(both(mult, both(j >= PREP_STEP0, j < rope_j0)))
    def _():
        plain()
        prep_slab()

    @pl.when(both(mult, both(j >= rope_j0, j < n_main)))
    def _():
        roped()
        prep_slab()

    for ti in range(n_t):
        @pl.when(both(mult, j == n_main + ti))
        def _(ti=ti):
            transposed(ti)
            prep_slab()

    @pl.when(both(jnp.logical_not(mult), j >= PREP_STEP0))
    def _():
        prep_slab()


def _inproj(x2, mod3, g1, pos2, w_all, w_aq, w_t, B, S, NG, tm=1024, tn=1024, n_t=2):
    N, D = x2.shape
    P = 4 * (D // 2)
    TR = w_t.shape[0]
    hd = ATTN_HEAD_DIM
    half = hd // 2
    inv = ROPE_THETA ** (-jnp.arange(half, dtype=F32) * 2.0 / hd)
    inv2 = jnp.concatenate([inv, inv]).reshape(1, hd)
    sgn = jnp.concatenate([-jnp.ones((half,), F32), jnp.ones((half,), F32)]).reshape(1, hd)
    MW = D // 2
    KW = ATTN_KV_HEADS * hd
    aq0 = 3 * MW
    ak_rows = (MW + KW, MW + 2 * KW)
    gate_rows = (MW + 2 * KW, MW + 2 * KW + NG)
    tt = TR // n_t
    n_tiles = N // tm
    n_main = P // tn
    n_j = n_main + n_t
    rope_j0 = aq0 // tn
    assert aq0 % tn == 0 and MW == tn and w_aq.shape == (D, MW) and TR % n_t == 0 and tt % BF16_ROWS == 0
    assert gate_rows[1] <= TR and tm % PREP_SLABS == 0 and n_j == PREP_STEP0 + PREP_SLABS
    assert PREP_STEP0 <= rope_j0
    kern = functools.partial(_inproj_kernel, D=D, tm=tm, n_tiles=n_tiles, n_main=n_main, rope_j0=rope_j0,
                             n_t=n_t, ak_rows=ak_rows, gate_rows=gate_rows)
    per_b = S // tm
    tile_p = lambda r: jnp.minimum(r, n_tiles - 1)
    tile_c = lambda r: jnp.maximum(r - 1, 0)
    live = lambda r: jnp.minimum(r, 1)
    main_j = lambda r, j: jnp.minimum(j, n_main - 1) * live(r)
    t_j = lambda r, j: jnp.maximum(j - n_main, 0) * live(r)
    return pl.pallas_call(
        kern,
        grid=(n_tiles + 1, n_j),
        in_specs=[pl.BlockSpec(memory_space=pl.ANY),
                  pl.BlockSpec((1, 1, mod3.shape[2]), lambda r, j: (tile_p(r) // per_b, 0, 0)),
                  pl.BlockSpec((1, D), lambda r, j: (0, 0)),
                  pl.BlockSpec((tm, 1), lambda r, j: (tile_p(r), 0)),
                  pl.BlockSpec((1, hd), lambda r, j: (0, 0)),
                  pl.BlockSpec((1, hd), lambda r, j: (0, 0)),
                  pl.BlockSpec((D, tn), lambda r, j: (0, jnp.where(j < 2 * MW // tn, j, 3 * MW // tn))),
                  pl.BlockSpec((D, tn), lambda r, j: (0, 0)),
                  pl.BlockSpec((tt, D), lambda r, j: (jnp.maximum(j - n_main, 0), 0))],
        out_specs=[pl.BlockSpec((tm, tn), lambda r, j: (tile_c(r), main_j(r, j))),
                   pl.BlockSpec((1, tt, tm), lambda r, j: (tile_c(r) // per_b, t_j(r, j), tile_c(r) % per_b)),
                   pl.BlockSpec((NG, tm), lambda r, j: (0, tile_c(r)))],
        out_shape=[jax.ShapeDtypeStruct((N, P), BF16),
                   jax.ShapeDtypeStruct((B, TR, S), BF16),
                   jax.ShapeDtypeStruct((NG, N), F32)],
        scratch_shapes=[pltpu.VMEM((tm, D), F32),
                        pltpu.VMEM((2, tm, D), BF16),
                        pltpu.VMEM((2, tm, hd), F32),
                        pltpu.VMEM((2, tm, hd), F32),
                        pltpu.VMEM((2, PREP_SLABS, hd, tm // PREP_SLABS), F32),
                        pltpu.VMEM((2, PREP_SLABS, hd, tm // PREP_SLABS), F32),
                        pltpu.SemaphoreType.DMA(())],
        compiler_params=_params(("arbitrary", "arbitrary")),
        name="inproj",
    )(x2, mod3, g1, pos2, inv2, sgn, w_all, w_aq, w_t)


_R, _MT, _A, _LIM, _E, _SC = range(6)
_NVEC = 6


def _log_sigmoid(x):
    return jnp.minimum(x, 0.0) - jnp.log1p(jnp.exp(-jnp.abs(x)))


def _lane_scan(x, reverse, op, ident):
    L = x.shape[1]
    lane = lax.broadcasted_iota(jnp.int32, x.shape, 1)
    sh = 1
    while sh < L:
        if reverse:
            x = op(x, jnp.where(lane < L - sh, pltpu.roll(x, L - sh, 1), ident))
        else:
            x = op(x, jnp.where(lane >= sh, pltpu.roll(x, sh, 1), ident))
        sh *= 2
    return x


def _chunk_scan(x, reverse, op, ident):
    n = x.shape[1] // LANES
    parts = [_lane_scan(x[:, i * LANES:(i + 1) * LANES], reverse, op, ident) for i in range(n)]
    order = range(n - 2, -1, -1) if reverse else range(1, n)
    for i in order:
        prev = parts[i + 1][:, 0:1] if reverse else parts[i - 1][:, LANES - 1:LANES]
        parts[i] = op(parts[i], prev)
    return jnp.concatenate(parts, axis=1)


def _gate_vectors(li, lf_pre, reverse, nc):
    rows, Lc = li.shape
    hb = rows // nc
    lf = _log_sigmoid(lf_pre)
    b = _chunk_scan(lf, reverse, jnp.add, 0.0)
    r = li - b
    rcm = _chunk_scan(r, reverse, jnp.maximum, NEG_INF)
    end = slice(0, 1) if reverse else slice(Lc - 1, Lc)
    bl = jnp.broadcast_to(b[:, end], (rows, Lc))
    rmax = jnp.broadcast_to(rcm[:, end], (rows, Lc))
    m = jnp.zeros((hb, Lc), F32)
    m_parts = [None] * nc
    for j in (range(nc - 1, -1, -1) if reverse else range(nc)):
        m_parts[j] = m
        sl = slice(j * hb, (j + 1) * hb)
        m = bl[sl, :] + jnp.maximum(m, rmax[sl, :])
    m_all = jnp.concatenate(m_parts, axis=0)
    mt = jnp.maximum(m_all, rcm)
    m_end = jnp.maximum(m_all, rmax)
    return r, mt, jnp.exp(m_all - mt), jnp.exp(-(b + mt)), jnp.exp(r - m_end), jnp.exp(m_all - m_end)


def _gatevec_kernel(g_ref, gb_ref, o_ref, *, nc):
    for dr in range(2):
        li = g_ref[2 * dr] + gb_ref[2 * dr]
        lf_pre = g_ref[2 * dr + 1] + gb_ref[2 * dr + 1]
        for idx, val in enumerate(_gate_vectors(li, lf_pre, dr == 1, nc)):
            o_ref[dr * _NVEC + idx] = val


def _gatevec(gates, gbias, nc):
    _, rows, Lc = gates.shape
    return pl.pallas_call(
        functools.partial(_gatevec_kernel, nc=nc),
        out_shape=jax.ShapeDtypeStruct((2 * _NVEC, rows, Lc), F32),
        compiler_params=pltpu.CompilerParams(vmem_limit_bytes=VMEM_LIMIT),
        name="gatevec",
    )(gates, gbias)


CONV_ROWS = 128


def _mlstm_kernel(q_ref, k_ref, vt_ref, mo_ref, vec_ref, wq_ref, wk_ref, bq_ref, bk_ref, go_ref, wo_ref,
                  o_ref, wob_ref, pad_scr, qs_scr, big_scr, vt_scr, ns_scr, ct_scr, n_scr, p0_scr, p1_scr,
                  *, S, d, Lc, CW):
    wob_ref[...] = wo_ref[...].astype(BF16)
    nc = S // Lc
    half_w = CW // 2
    PADR = 8
    CR = CONV_ROWS
    N0 = Lc
    C0 = Lc + BF16_ROWS

    zero_rows = jnp.zeros((PADR, d), F32)

    def conv_silu(src_ref, w_ref, b_ref, store, scale):
        pad_scr[0:PADR, :] = zero_rows
        pad_scr[PADR + S:2 * PADR + S, :] = zero_rows

        def fill(j, c):
            r0 = pl.multiple_of(j * CR, CR)
            pad_scr[pl.ds(PADR + r0, CR), :] = src_ref[0, pl.ds(r0, CR), :].astype(F32)
            return c
        lax.fori_loop(0, S // CR, fill, 0)
        w = w_ref[...]
        bias = b_ref[...]

        for j in range(S // CR):
            r0 = j * CR
            y = bias
            for i in range(CW):
                o = r0 + PADR - half_w + i
                y = y + pad_scr[o:o + CR, :] * w[i:i + 1, :]
            y = y * _sigmoid(y)
            if scale != 1.0:
                y = y * scale
            store(j, r0, y.astype(BF16))

    def store_q(j, r0, y):
        qs_scr[r0:r0 + CR, :] = y

    per = Lc // CR

    def store_k(j, r0, y):
        big_scr[j // per, (j % per) * CR:(j % per + 1) * CR, :] = y

    conv_silu(q_ref, wq_ref, bq_ref, store_q, 1.0)
    conv_silu(k_ref, wk_ref, bk_ref, store_k, d ** -0.5)

    for j in range(nc):
        vt_scr[j] = vt_ref[0, :, j * Lc:(j + 1) * Lc]

    ns_scr[...] = jnp.zeros(ns_scr.shape, F32)
    ct_scr[...] = jnp.zeros(ct_scr.shape, F32)
    n_scr[...] = jnp.zeros(n_scr.shape, F32)

    def state_step(dr, jc):
        base = dr * _NVEC
        e_row = vec_ref[0, 0, base + _E, pl.ds(jc, 1), :]
        sc = vec_ref[0, 0, base + _SC, pl.ds(jc, 1), :]
        lhs = jnp.concatenate([vt_scr[jc].astype(F32) * e_row, jnp.broadcast_to(e_row, (BF16_ROWS, Lc))], axis=0)
        upd = jnp.dot(lhs.astype(BF16), big_scr[jc, 0:Lc, :], preferred_element_type=F32)
        ct = ct_scr[dr]
        n = n_scr[dr]
        big_scr[jc, C0 + dr * d:C0 + (dr + 1) * d, :] = ct.astype(BF16)
        ns_scr[jc, dr:dr + 1, :] = n[0:1, :]
        ct_scr[dr] = sc * ct + upd[0:d, :]
        n_scr[dr] = sc * n + upd[d:d + 8, :]

    for j in range(nc):
        state_step(0, j)
        state_step(1, nc - 1 - j)
    for j in range(nc):
        big_scr[j, N0:N0 + BF16_ROWS, :] = ns_scr[j].astype(BF16)

    rr = lax.broadcasted_iota(jnp.int32, (Lc, Lc), 0)
    cc = lax.broadcasted_iota(jnp.int32, (Lc, Lc), 1)
    r1 = lax.broadcasted_iota(jnp.int32, (LANES, LANES), 0)
    c1 = lax.broadcasted_iota(jnp.int32, (LANES, LANES), 1)
    eye = r1 == c1
    gout = go_ref[...]

    def to_col(row):
        return jnp.concatenate(
            [jnp.sum(jnp.where(eye, row[:, i * LANES:(i + 1) * LANES], 0.0), axis=1, keepdims=True)
             for i in range(Lc // LANES)], axis=0)

    hrows = C0 + d

    def head(j, p_scr):
        qj = qs_scr[pl.ds(pl.multiple_of(j * Lc, Lc), Lc), :]
        p_scr[0:hrows, :] = lax.dot_general(big_scr[j, 0:hrows, :], qj, _NT, preferred_element_type=F32)
        p_scr[hrows:hrows + d, :] = lax.dot_general(big_scr[j, hrows:hrows + d, :], qj, _NT,
                                                    preferred_element_type=F32)

    def tail(j, p_scr):
        r0 = pl.multiple_of(j * Lc, Lc)
        st = p_scr[0:Lc, :]
        qkw = []
        inv_den = []
        a_rows = []
        for dr in range(2):
            base = dr * _NVEC
            r_row = vec_ref[0, 0, base + _R, pl.ds(j, 1), :]
            mt_row = vec_ref[0, 0, base + _MT, pl.ds(j, 1), :]
            a_row = vec_ref[0, 0, base + _A, pl.ds(j, 1), :]
            lim_row = vec_ref[0, 0, base + _LIM, pl.ds(j, 1), :]
            valid = (rr <= cc) if dr == 0 else (rr >= cc)
            w = jnp.exp(jnp.where(valid, to_col(r_row) - mt_row, NEG_INF))
            qkw_d = st * w
            den = a_row * p_scr[N0 + dr:N0 + dr + 1, :] + jnp.sum(qkw_d, axis=0, keepdims=True)
            inv_den.append(1.0 / jnp.maximum(jnp.abs(den), lim_row))
            a_rows.append(a_row)
            qkw.append(qkw_d.astype(BF16))
        intra = jnp.dot(vt_scr[j], jnp.concatenate(qkw, axis=1), preferred_element_type=F32)
        hs = None
        for dr in range(2):
            inter = p_scr[C0 + dr * d:C0 + (dr + 1) * d, :]
            hd = (a_rows[dr] * inter + intra[:, dr * Lc:(dr + 1) * Lc]) * inv_den[dr]
            hs = hd if hs is None else hs + hd
        mu = jnp.mean(hs, axis=0, keepdims=True)
        hc = hs - mu
        hn = (hc * lax.rsqrt(jnp.mean(hc * hc, axis=0, keepdims=True) + EPS)).T
        mo = mo_ref[0, pl.ds(r0, Lc), :].astype(F32)
        o_ref[0, pl.ds(r0, Lc), :] = (_sigmoid(mo) * hn * gout).astype(BF16)

    bufs = (p0_scr, p1_scr)
    head(0, bufs[0])
    for j in range(nc):
        if j + 1 < nc:
            head(j + 1, bufs[(j + 1) % 2])
        tail(j, bufs[j % 2])


def _mlstm(proj3, vt3, vec5, w_conv, b_conv, g_out, w_out):
    B, S, _ = proj3.shape
    H = MLSTM_HEADS
    d = g_out.shape[1] // H
    Lc = MLSTM_CHUNK
    nc = S // Lc
    CW = w_conv.shape[0]
    assert Lc % LANES == 0 and Lc % CONV_ROWS == 0 and d == Lc and CW // 2 <= 8 and nc % 2 == 0 and nc >= 4
    wo_slab = w_out.shape[0] // (B * H)
    assert wo_slab * B * H == w_out.shape[0] and wo_slab % BF16_ROWS == 0
    kern = functools.partial(_mlstm_kernel, S=S, d=d, Lc=Lc, CW=CW)
    col = lambda off: (lambda b, h: (b, 0, off + h))
    return pl.pallas_call(
        kern,
        grid=(B, H),
        in_specs=[pl.BlockSpec((1, S, d), col(0)),
                  pl.BlockSpec((1, S, d), col(H)),
                  pl.BlockSpec((1, d, S), lambda b, h: (b, h, 0)),
                  pl.BlockSpec((1, S, d), col(2 * H)),
                  pl.BlockSpec((1, 1, 2 * _NVEC, nc, Lc), lambda b, h: (b, h, 0, 0, 0)),
                  pl.BlockSpec((CW, d), lambda b, h: (0, h)),
                  pl.BlockSpec((CW, d), lambda b, h: (0, H + h)),
                  pl.BlockSpec((1, d), lambda b, h: (0, h)),
                  pl.BlockSpec((1, d), lambda b, h: (0, H + h)),
                  pl.BlockSpec((1, d), lambda b, h: (0, h)),
                  pl.BlockSpec((wo_slab, w_out.shape[1]), lambda b, h: (b * H + h, 0))],
        out_specs=[pl.BlockSpec((1, S, d), lambda b, h: (b, 0, h)),
                   pl.BlockSpec((wo_slab, w_out.shape[1]), lambda b, h: (b * H + h, 0))],
        out_shape=[jax.ShapeDtypeStruct((B, S, H * d), BF16),
                   jax.ShapeDtypeStruct(w_out.shape, BF16)],
        scratch_shapes=[pltpu.VMEM((S + 16, d), F32),
                        pltpu.VMEM((S, d), BF16),
                        pltpu.VMEM((nc, Lc + BF16_ROWS + 2 * d, d), BF16),
                        pltpu.VMEM((nc, d, Lc), BF16),
                        pltpu.VMEM((nc, BF16_ROWS, d), F32),
                        pltpu.VMEM((2, d, d), F32),
                        pltpu.VMEM((2, 8, d), F32),
                        pltpu.VMEM((Lc + BF16_ROWS + 2 * d, Lc), F32),
                        pltpu.VMEM((Lc + BF16_ROWS + 2 * d, Lc), F32)],
        compiler_params=_params(("arbitrary", "arbitrary")),
        name="mlstm",
    )(proj3, proj3, vt3, proj3, vec5, w_conv, w_conv, b_conv, b_conv, g_out, w_out)


LOG2E = 1.4426950408889634
ATTN_QBLOCKS = 4


def _attn_kernel(sink_ref, q_ref, ktp_ref, ktc_ref, ktn_ref, vtp_ref, vtc_ref, vtn_ref, g_ref, o_ref,
                 s0_scr, s1_scr, p0_scr, p1_scr, e0_scr, e1_scr, y_scr, *, H, G, QB):
    n4 = pl.program_id(1)
    last4 = pl.num_programs(1) - 1
    blk = ATTN_BLOCK
    hd = ATTN_HEAD_DIM
    R = H // G
    c1 = (hd ** -0.5) * LOG2E
    rr = lax.broadcasted_iota(jnp.int32, (blk, blk), 0)
    cc = lax.broadcasted_iota(jnp.int32, (blk, blk), 1)
    ones = jnp.ones((hd, 3 * blk), BF16)

    def kv_window(i, g, prev_ref, cur_ref, next_ref):
        rs = slice(g * hd, (g + 1) * hd)
        lo = prev_ref[0, rs, :] if i == 0 else cur_ref[0, rs, (i - 1) * blk:i * blk]
        hi = next_ref[0, rs, :] if i == QB - 1 else cur_ref[0, rs, (i + 1) * blk:(i + 2) * blk]
        return jnp.concatenate([lo, cur_ref[0, rs, i * blk:(i + 1) * blk], hi], axis=1)

    def head(i, s_scr):
        q = q_ref[0, i * blk:(i + 1) * blk, :]
        for g in range(G):
            qg = jnp.concatenate([q[:, (g * R + r) * hd:(g * R + r + 1) * hd] for r in range(R)], axis=0)
            s_scr[g] = jnp.dot(qg, kv_window(i, g, ktp_ref, ktc_ref, ktn_ref), preferred_element_type=F32)

    def mid(i, s_scr, p_scr, e_scr):
        m_prev = (cc >= rr) if i > 0 else jnp.logical_and(cc >= rr, n4 > 0)
        m_next = (cc <= rr) if i < QB - 1 else jnp.logical_and(cc <= rr, n4 < last4)
        for g in range(G):
            for r in range(R):
                sr = s_scr[g, r * blk:(r + 1) * blk, :] * c1
                t0 = jnp.where(m_prev, sr[:, 0:blk], NEG_INF)
                t1 = sr[:, blk:2 * blk]
                t2 = jnp.where(m_next, sr[:, 2 * blk:3 * blk], NEG_INF)
                sink2 = sink_ref[g * R + r] * LOG2E
                m = jnp.maximum(jnp.max(jnp.maximum(jnp.maximum(t0, t1), t2), axis=1, keepdims=True), sink2)
                p_scr[g, r * blk:(r + 1) * blk, :] = jnp.concatenate(
                    [jnp.exp2(t0 - m), jnp.exp2(t1 - m), jnp.exp2(t2 - m)], axis=1).astype(BF16)
                e_scr[g * R + r] = jnp.broadcast_to(jnp.exp2(sink2 - m), (blk, hd))

    def tail(i, p_scr, e_scr):
        ssq = jnp.zeros((blk, 1), F32)
        for g in range(G):
            vt = kv_window(i, g, vtp_ref, vtc_ref, vtn_ref)
            vt_aug = jnp.concatenate([vt, ones], axis=0)
            oa = lax.dot_general(p_scr[g], vt_aug, _NT, preferred_element_type=F32)
            for r in range(R):
                o_r = oa[r * blk:(r + 1) * blk, :]
                o_h = o_r[:, 0:hd] / (o_r[:, hd:2 * hd] + e_scr[g * R + r])
                ssq = ssq + jnp.sum(o_h * o_h, axis=-1, keepdims=True)
                y_scr[:, (g * R + r) * hd:(g * R + r + 1) * hd] = o_h
        rstd = lax.rsqrt(ssq * (1.0 / (H * hd)) + EPS)
        o_ref[0, i * blk:(i + 1) * blk, :] = (y_scr[...] * rstd * g_ref[...]).astype(BF16)

    sb, pb, eb = (s0_scr, s1_scr), (p0_scr, p1_scr), (e0_scr, e1_scr)
    head(0, sb[0])
    if QB > 1:
        head(1, sb[1])
    mid(0, sb[0], pb[0], eb[0])
    for i in range(QB):
        if i + 2 < QB:
            head(i + 2, sb[i % 2])
        if i + 1 < QB:
            mid(i + 1, sb[(i + 1) % 2], pb[(i + 1) % 2], eb[(i + 1) % 2])
        tail(i, pb[i % 2], eb[i % 2])


def _attn(proj3, t3, sink, g_attn, q_col0, vt_row0, kt_row0):
    B, S, _ = proj3.shape
    blk = ATTN_BLOCK
    QB = ATTN_QBLOCKS
    nb = S // blk
    H = sink.shape[0]
    G = ATTN_KV_HEADS
    AW = H * ATTN_HEAD_DIM
    KW = G * ATTN_HEAD_DIM
    assert WINDOW == blk and q_col0 % AW == 0 and vt_row0 % KW == 0 and kt_row0 % KW == 0 and nb % QB == 0
    qb, vb, kb = q_col0 // AW, vt_row0 // KW, kt_row0 // KW
    kern = functools.partial(_attn_kernel, H=H, G=G, QB=QB)
    prev = lambda rb: (lambda b, n: (b, rb, jnp.maximum(n * QB - 1, 0)))
    cur = lambda rb: (lambda b, n: (b, rb, n))
    nxt = lambda rb: (lambda b, n: (b, rb, jnp.minimum(n * QB + QB, nb - 1)))
    return pl.pallas_call(
        kern,
        grid=(B, nb // QB),
        in_specs=[pl.BlockSpec(memory_space=pltpu.SMEM),
                  pl.BlockSpec((1, QB * blk, AW), lambda b, n: (b, n, qb)),
                  pl.BlockSpec((1, KW, blk), prev(kb)),
                  pl.BlockSpec((1, KW, QB * blk), cur(kb)),
                  pl.BlockSpec((1, KW, blk), nxt(kb)),
                  pl.BlockSpec((1, KW, blk), prev(vb)),
                  pl.BlockSpec((1, KW, QB * blk), cur(vb)),
                  pl.BlockSpec((1, KW, blk), nxt(vb)),
                  pl.BlockSpec((1, AW), lambda b, n: (0, 0))],
        out_specs=pl.BlockSpec((1, QB * blk, AW), lambda b, n: (b, n, 0)),
        out_shape=jax.ShapeDtypeStruct((B, S, AW), BF16),
        scratch_shapes=[pltpu.VMEM((G, (H // G) * blk, 3 * blk), F32),
                        pltpu.VMEM((G, (H // G) * blk, 3 * blk), F32),
                        pltpu.VMEM((G, (H // G) * blk, 3 * blk), BF16),
                        pltpu.VMEM((G, (H // G) * blk, 3 * blk), BF16),
                        pltpu.VMEM((H, blk, ATTN_HEAD_DIM), F32),
                        pltpu.VMEM((H, blk, ATTN_HEAD_DIM), F32),
                        pltpu.VMEM((blk, AW), F32)],
        compiler_params=_params(("arbitrary", "arbitrary")),
        name="attn",
    )(sink, proj3, t3, t3, t3, t3, t3, t3, g_attn)


def _outproj_kernel(m_ref, a_ref, x_ref, mod_ref, g2_ref, wt_ref, wb_ref, o_ref, h_ref, *, D):
    acc = jnp.dot(m_ref[...], wt_ref[...], preferred_element_type=F32)
    acc = acc + jnp.dot(a_ref[...], wb_ref[...], preferred_element_type=F32)
    gt = mod_ref[0, :, 2 * D:3 * D]
    x1 = x_ref[...] + gt * acc
    o_ref[...] = x1
    sh = mod_ref[0, :, 3 * D:4 * D]
    gsc = g2_ref[...] * (1.0 + mod_ref[0, :, 4 * D:5 * D])
    h_ref[...] = (x1 * lax.rsqrt(jnp.mean(x1 * x1, axis=-1, keepdims=True) + EPS) * gsc + sh).astype(BF16)


def _outproj(m2, a2, x2, mod3, g2, w_o, S, tm=512):
    N, D = x2.shape
    K = m2.shape[1]
    per_b = S // tm
    return pl.pallas_call(
        functools.partial(_outproj_kernel, D=D),
        grid=(N // tm,),
        in_specs=[pl.BlockSpec((tm, K), lambda i: (i, 0)),
                  pl.BlockSpec((tm, K), lambda i: (i, 0)),
                  pl.BlockSpec((tm, D), lambda i: (i, 0)),
                  pl.BlockSpec((1, 1, mod3.shape[2]), lambda i: (i // per_b, 0, 0)),
                  pl.BlockSpec((1, D), lambda i: (0, 0)),
                  pl.BlockSpec((K, D), lambda i: (0, 0)),
                  pl.BlockSpec((K, D), lambda i: (1, 0))],
        out_specs=[pl.BlockSpec((tm, D), lambda i: (i, 0)),
                   pl.BlockSpec((tm, D), lambda i: (i, 0))],
        out_shape=[jax.ShapeDtypeStruct((N, D), F32),
                   jax.ShapeDtypeStruct((N, D), BF16)],
        compiler_params=_params(("arbitrary",)),
        name="outproj",
    )(m2, a2, x2, mod3, g2, w_o, w_o)


def _ffn_up_kernel(h_ref, wg_ref, wu_ref, wd_ref, o_ref, wdb_ref):
    hb = h_ref[...]
    tf = o_ref.shape[1]
    cw = 2 * LANES
    for c in range(tf // cw):
        cs = slice(c * cw, (c + 1) * cw)
        w = jnp.concatenate([wg_ref[:, cs].astype(BF16), wu_ref[:, cs].astype(BF16)], axis=1)
        gu = jnp.dot(hb, w, preferred_element_type=F32)
        gate = gu[:, 0:cw]
        o_ref[:, cs] = (gate * _sigmoid(gate) * gu[:, cw:2 * cw]).astype(BF16)
    wdb_ref[...] = wd_ref[...].astype(BF16)


def _ffn_up(h2, w_gate, w_up, w_down, tm=2048, tf=512):
    N, D = h2.shape
    FF = w_gate.shape[1]
    n_i, n_j = N // tm, FF // tf
    slab = FF // (n_i * n_j)
    assert slab * n_i * n_j == FF and slab % BF16_ROWS == 0
    step = lambda i, j: i * n_j + j
    return pl.pallas_call(
        _ffn_up_kernel,
        grid=(n_i, n_j),
        in_specs=[pl.BlockSpec((tm, D), lambda i, j: (i, 0)),
                  pl.BlockSpec((D, tf), lambda i, j: (0, j)),
                  pl.BlockSpec((D, tf), lambda i, j: (0, j)),
                  pl.BlockSpec((slab, D), lambda i, j: (step(i, j), 0))],
        out_specs=[pl.BlockSpec((tm, tf), lambda i, j: (i, j)),
                   pl.BlockSpec((slab, D), lambda i, j: (step(i, j), 0))],
        out_shape=[jax.ShapeDtypeStruct((N, FF), BF16),
                   jax.ShapeDtypeStruct(w_down.shape, BF16)],
        compiler_params=_params(("arbitrary", "arbitrary")),
        name="ffn_up",
    )(h2, w_gate, w_up, w_down)


def _ffn_down_kernel(u_ref, x_hbm, mod_ref, gf_ref, wd_ref, o_ref, xbuf, sem, *, D, tm):
    i = pl.program_id(0)
    j = pl.program_id(1)

    def x_copy():
        return pltpu.make_async_copy(x_hbm.at[pl.ds(pl.multiple_of(i * tm, tm), tm), :], xbuf, sem)

    last = pl.num_programs(1) - 1

    @pl.when(j == 0)
    def _():
        x_copy().start()
        o_ref[...] = jnp.dot(u_ref[...], wd_ref[...], preferred_element_type=F32)

    @pl.when(jnp.logical_and(j > 0, j < last))
    def _():
        o_ref[...] += jnp.dot(u_ref[...], wd_ref[...], preferred_element_type=F32)

    @pl.when(j == last)
    def _():
        x_copy().wait()
        gt = mod_ref[0, :, 5 * D:6 * D]
        for rows in (slice(0, tm // 2), slice(tm // 2, tm)):
            acc = o_ref[rows, :] + jnp.dot(u_ref[rows, :], wd_ref[...], preferred_element_type=F32)
            y = xbuf[rows, :] + gt * acc
            o_ref[rows, :] = y * lax.rsqrt(jnp.mean(y * y, axis=-1, keepdims=True) + EPS) * gf_ref[...]


def _ffn_down(u2, x2, mod3, gf, w_down, S, tm=1024, tk=1408):
    N, D = x2.shape
    FF = u2.shape[1]
    per_b = S // tm
    assert FF // tk >= 2, "the first and the last K step are distinct branches"
    return pl.pallas_call(
        functools.partial(_ffn_down_kernel, D=D, tm=tm),
        grid=(N // tm, FF // tk),
        in_specs=[pl.BlockSpec((tm, tk), lambda i, j: (i, j)),
                  pl.BlockSpec(memory_space=pl.ANY),
                  pl.BlockSpec((1, 1, mod3.shape[2]), lambda i, j: (i // per_b, 0, 0)),
                  pl.BlockSpec((1, D), lambda i, j: (0, 0)),
                  pl.BlockSpec((tk, D), lambda i, j: (j, 0))],
        out_specs=pl.BlockSpec((tm, D), lambda i, j: (i, 0)),
        out_shape=jax.ShapeDtypeStruct((N, D), F32),
        scratch_shapes=[pltpu.VMEM((tm, D), F32), pltpu.SemaphoreType.DMA(())],
        compiler_params=_params(("arbitrary", "arbitrary")),
        name="ffn_down",
    )(u2, x2, mod3, gf, w_down)


def kernel(x, c, positions, w_ada, b_ada, g_norm1, g_norm2, w_in, b_gates, w_conv, b_conv, g_mlstm_out,
           sink, g_attn_out, w_out, w_gate, w_up, w_down, g_final):
    B, S, D = x.shape
    depth = w_ada.shape[0]
    H = MLSTM_HEADS
    MW = D // 2
    KW = ATTN_KV_HEADS * ATTN_HEAD_DIM
    NG = 4 * H
    L = MLSTM_CHUNK
    nc = S // L
    g0 = 4 * MW
    x2 = x.reshape(B * S, D)
    pos2 = positions.reshape(B * S, 1)
    for l in range(depth):
        mod3 = _ada(c, w_ada[l], b_ada[l]).reshape(B, 1, 6 * D)
        wl = w_in[l].astype(BF16)
        a0 = g0 + NG
        t_rows = MW + 2 * KW + NG
        t_pad = -t_rows % (2 * BF16_ROWS)
        w_t = jnp.concatenate([wl[:, 2 * MW:3 * MW], wl[:, a0 + MW + KW:], wl[:, a0 + MW:a0 + MW + KW],
                               wl[:, g0:a0], jnp.zeros((D, t_pad), BF16)], axis=1).T
        proj, t3, gates_t = _inproj(x2, mod3, g_norm1[l].reshape(1, D), pos2, wl, wl[:, a0:a0 + MW], w_t, B, S, NG)
        proj3 = proj.reshape(B, S, proj.shape[1])
        gates = gates_t.reshape(4, H, B, nc, L).transpose(0, 3, 1, 2, 4).reshape(4, nc * H * B, L)
        gbias = jnp.broadcast_to(b_gates[l].reshape(4, 1, H, 1), (4, nc, H, B)).reshape(4, nc * H * B, 1)
        vec = _gatevec(gates, gbias, nc)
        vec5 = vec.reshape(2 * _NVEC, nc, H, B, L).transpose(3, 2, 0, 1, 4)
        m_out, w_out_b = _mlstm(proj3, t3, vec5, w_conv[l], b_conv[l].reshape(1, 2 * MW),
                                g_mlstm_out[l].reshape(1, MW), w_out[l])
        a_out = _attn(proj3, t3, sink[l], g_attn_out[l].reshape(1, MW), q_col0=3 * MW,
                      vt_row0=MW, kt_row0=MW + KW)
        x2, h2 = _outproj(m_out.reshape(B * S, MW), a_out.reshape(B * S, MW), x2, mod3,
                          g_norm2[l].reshape(1, D), w_out_b, S)
        assert l == depth - 1, "final norm is fused into the last layer's FFN"
        u2, w_down_b = _ffn_up(h2, w_gate[l], w_up[l], w_down[l])
        x2 = _ffn_down(u2, x2, mod3, g_final.reshape(1, D), w_down_b, S)
    return x2.reshape(B, S, D)
```
